```python
import jax, jax.numpy as jnp
from jax import lax
import numpy as np

D_MODEL = 2048
BATCH = 4
SEQ = 2048
DEPTH = 2
DEC_BATCH = 128
DEC_SEQ = 1
PAST_LEN = 16384
PAGE_SIZE = 128

N_MIXERS = 2
N_HGRN = (DEPTH + 1) // 2
N_CONV = DEPTH // 2
HGRN_DK = 128
HGRN_HEADS = D_MODEL // HGRN_DK
HGRN_F = HGRN_HEADS * HGRN_DK
HGRN_DV = D_MODEL // HGRN_HEADS
HGRN_CHUNK = 64
CONV_WIDTH = 31
CONV_CH = D_MODEL
D_FF = 5632
N_SUB = 3
EPS = 1e-6

kernel_name = 'hgrn2_conformer_macaron_adaln_step'


def rmsnorm(x, g):
    xf = x.astype(jnp.float32)
    y = xf * lax.rsqrt(jnp.mean(xf * xf, axis=-1, keepdims=True) + EPS)
    return (y * g.astype(jnp.float32)).astype(x.dtype)


def layernorm(x, g, b):
    xf = x.astype(jnp.float32)
    mu = jnp.mean(xf, axis=-1, keepdims=True)
    xc = xf - mu
    y = xc * lax.rsqrt(jnp.mean(xc * xc, axis=-1, keepdims=True) + EPS)
    return (y * g.astype(jnp.float32) + b.astype(jnp.float32)).astype(x.dtype)


def adaln_params(c, w_ada, b_ada):
    m = jax.nn.silu(c) @ w_ada + b_ada
    return m.reshape(c.shape[0], N_SUB, 3, D_MODEL)


def modulate(x, g, shift, scale):
    return rmsnorm(x, g) * (1 + scale[:, None, :]) + shift[:, None, :]


def swiglu(h, w_in, w_out):
    a, u = jnp.split(h @ w_in, 2, axis=-1)
    return (jax.nn.silu(a) * u) @ w_out


def hgrn2_scan(q, k, v, g, s0):
    B, T = q.shape[0], q.shape[1]
    C = min(HGRN_CHUNK, T)
    n = -(-T // C)
    pad = n * C - T

    def prep(a):
        a = jnp.pad(a.astype(jnp.float32), ((0, 0), (0, pad), (0, 0), (0, 0)))
        return a.reshape(B, n, C, a.shape[2], a.shape[3]).transpose(1, 0, 3, 2, 4)

    qc, kc, vc, gc = prep(q), prep(k), prep(v), prep(g)
    causal = jnp.tril(jnp.ones((C, C), dtype=bool))[:, :, None]

    def step(S, inp):
        qi, ki, vi, gi = inp
        b = jnp.cumsum(gi, axis=2)
        diff = b[:, :, :, None, :] - b[:, :, None, :, :]
        decay = jnp.exp(jnp.where(causal, diff, -jnp.inf))
        att = jnp.einsum('bhtk,bhsk,bhtsk->bhts', qi, ki, decay)
        o = jnp.einsum('bhts,bhsv->bhtv', att, vi) + jnp.einsum('bhtk,bhkv->bhtv', qi * jnp.exp(b), S)
        b_last = b[:, :, -1:, :]
        S = jnp.exp(b_last[:, :, 0, :])[..., None] * S + jnp.einsum('bhsk,bhsv->bhkv', ki * jnp.exp(b_last - b), vi)
        return S, o

    S, o = lax.scan(step, s0.astype(jnp.float32), (qc, kc, vc, gc))
    o = o.transpose(1, 0, 3, 2, 4).reshape(B, n * C, q.shape[2], v.shape[3])[:, :T]
    return o, S


def hgrn2_mixer(h, s0, lb, w_in, b_f, g_norm, w_out):
    B, T, _ = h.shape
    q, fr, i, gt = jnp.split(h @ w_in, [HGRN_F, 2 * HGRN_F, 2 * HGRN_F + D_MODEL], axis=-1)
    z = (fr + b_f).astype(jnp.float32)
    f = lb + (1.0 - lb) * jax.nn.sigmoid(z)
    logf = jnp.log(f)
    k = (1.0 - lb) * jax.nn.sigmoid(-z)
    hd = lambda a, d: a.reshape(B, T, HGRN_HEADS, d)
    o, S = hgrn2_scan(hd(q, HGRN_DK), hd(k, HGRN_DK), hd(i, HGRN_DV), hd(logf, HGRN_DK), s0)
    o = rmsnorm(o.astype(h.dtype), g_norm.reshape(HGRN_HEADS, HGRN_DV)).reshape(B, T, D_MODEL)
    return (o * jax.nn.silu(gt)) @ w_out, S


def conv_module(h, buf, w_pw1, b_pw1, w_dw, b_dw, ln_g, ln_b, w_pw2, b_pw2):
    a, gate = jnp.split(h @ w_pw1 + b_pw1, 2, axis=-1)
    u = a * jax.nn.sigmoid(gate)
    full = jnp.concatenate([buf.astype(u.dtype), u], axis=1)
    y = lax.conv_general_dilated(full, w_dw[:, None, :].astype(u.dtype), window_strides=(1,),
                                 padding='VALID', dimension_numbers=('NWC', 'WIO', 'NWC'),
                                 feature_group_count=CONV_CH) + b_dw
    y = jax.nn.silu(layernorm(y, ln_g, ln_b))
    return y @ w_pw2 + b_pw2, full[:, -(CONV_WIDTH - 1):]


def run_group(x, c, s_hgrn, s_conv, lbs, hgrn_w_in, hgrn_b_f, hgrn_g_norm, hgrn_w_out,
              conv_w_pw1, conv_b_pw1, conv_w_dw, conv_b_dw, conv_ln_g, conv_ln_b, conv_w_pw2, conv_b_pw2,
              ada_w, ada_b, norm_g, ffn_w_in, ffn_w_out, final_g):
    new_h, new_c = [], []
    for i in range(DEPTH):
        mod = adaln_params(c, ada_w[i], ada_b[i])
        h = modulate(x, norm_g[i, 0], mod[:, 0, 0], mod[:, 0, 1])
        x = x + 0.5 * mod[:, 0, 2][:, None, :] * swiglu(h, ffn_w_in[i, 0], ffn_w_out[i, 0])
        h = modulate(x, norm_g[i, 1], mod[:, 1, 0], mod[:, 1, 1])
        j = i // N_MIXERS
        if i % N_MIXERS == 0:
            out, s = hgrn2_mixer(h, s_hgrn[j], lbs[i], hgrn_w_in[j], hgrn_b_f[j], hgrn_g_norm[j], hgrn_w_out[j])
            new_h.append(s.astype(s_hgrn.dtype))
        else:
            out, b = conv_module(h, s_conv[j], conv_w_pw1[j], conv_b_pw1[j], conv_w_dw[j], conv_b_dw[j],
                                 conv_ln_g[j], conv_ln_b[j], conv_w_pw2[j], conv_b_pw2[j])
            new_c.append(b.astype(s_conv.dtype))
        x = x + mod[:, 1, 2][:, None, :] * out
        h = modulate(x, norm_g[i, 2], mod[:, 2, 0], mod[:, 2, 1])
        x = x + 0.5 * mod[:, 2, 2][:, None, :] * swiglu(h, ffn_w_in[i, 1], ffn_w_out[i, 1])
    return rmsnorm(x, final_g), jnp.stack(new_h), jnp.stack(new_c)


def setup_inputs(seed: int = 0) -> dict:
    key = jax.random.key(seed)
    ks = iter(jax.random.split(key, 32))
    nrm = lambda shape, s: jax.random.normal(next(ks), shape, jnp.float32) * s
    D = D_MODEL
    return {
        'x_prompt': nrm((BATCH, SEQ, D), 1.0),
        'x_sample': nrm((DEC_BATCH, DEC_SEQ, D), 1.0),
        'state_hgrn': nrm((N_HGRN, DEC_BATCH, HGRN_HEADS, HGRN_DK, HGRN_DV), 0.5),
        'state_conv': nrm((N_CONV, DEC_BATCH, CONV_WIDTH - 1, CONV_CH), 0.5),
        'c_prompt': nrm((BATCH, D), 1.0),
        'c_sample': nrm((DEC_BATCH, D), 1.0),
        'hgrn_lb': nrm((DEPTH + 1, HGRN_F), 1.0),
        'hgrn_w_in': nrm((N_HGRN, D, 2 * HGRN_F + 2 * D), D ** -0.5),
        'hgrn_b_f': nrm((N_HGRN, HGRN_F), 0.1),
        'hgrn_g_norm': 1.0 + nrm((N_HGRN, D), 0.05),
        'hgrn_w_out': nrm((N_HGRN, D, D), D ** -0.5),
        'conv_w_pw1': nrm((N_CONV, D, 2 * CONV_CH), D ** -0.5),
        'conv_b_pw1': nrm((N_CONV, 2 * CONV_CH), 0.02),
        'conv_w_dw': nrm((N_CONV, CONV_WIDTH, CONV_CH), CONV_WIDTH ** -0.5),
        'conv_b_dw': nrm((N_CONV, CONV_CH), 0.02),
        'conv_ln_g': 1.0 + nrm((N_CONV, CONV_CH), 0.05),
        'conv_ln_b': nrm((N_CONV, CONV_CH), 0.02),
        'conv_w_pw2': nrm((N_CONV, CONV_CH, D), CONV_CH ** -0.5),
        'conv_b_pw2': nrm((N_CONV, D), 0.02),
        'ada_w': nrm((DEPTH, D, N_SUB * 3 * D), 0.5 * D ** -0.5),
        'ada_b': nrm((DEPTH, N_SUB * 3 * D), 0.02),
        'norm_g': 1.0 + nrm((DEPTH, N_SUB, D), 0.05),
        'ffn_w_in': nrm((DEPTH, 2, D, 2 * D_FF), D ** -0.5),
        'ffn_w_out': nrm((DEPTH, 2, D_FF, D), D_FF ** -0.5),
        'final_g': 1.0 + nrm((D,), 0.05),
    }


def reference(x_prompt, x_sample, state_hgrn, state_conv, c_prompt, c_sample, hgrn_lb, hgrn_w_in, hgrn_b_f,
              hgrn_g_norm, hgrn_w_out, conv_w_pw1, conv_b_pw1, conv_w_dw, conv_b_dw, conv_ln_g, conv_ln_b,
              conv_w_pw2, conv_b_pw2, ada_w, ada_b, norm_g, ffn_w_in, ffn_w_out, final_g):
    lbs = jnp.cumsum(jax.nn.softmax(hgrn_lb.astype(jnp.float32), axis=0), axis=0)
    weights = (lbs, hgrn_w_in, hgrn_b_f, hgrn_g_norm, hgrn_w_out, conv_w_pw1, conv_b_pw1, conv_w_dw, conv_b_dw,
               conv_ln_g, conv_ln_b, conv_w_pw2, conv_b_pw2, ada_w, ada_b, norm_g, ffn_w_in, ffn_w_out, final_g)
    B = x_prompt.shape[0]
    s_hgrn0 = jnp.zeros((N_HGRN, B, HGRN_HEADS, HGRN_DK, HGRN_DV), state_hgrn.dtype)
    s_conv0 = jnp.zeros((N_CONV, B, CONV_WIDTH - 1, CONV_CH), state_conv.dtype)
    y_prompt, state_hgrn_prompt, state_conv_prompt = run_group(x_prompt, c_prompt, s_hgrn0, s_conv0, *weights)
    y_sample, state_hgrn_sample, state_conv_sample = run_group(x_sample, c_sample, state_hgrn, state_conv, *weights)
    return (y_prompt, y_sample, state_hgrn_prompt, state_hgrn_sample, state_conv_prompt, state_conv_sample)
```

```python
import functools

import jax
import jax.numpy as jnp
from jax import lax
from jax.experimental import pallas as pl
from jax.experimental.pallas import tpu as pltpu

F32 = jnp.float32
BF16 = jnp.bfloat16
EPS = 1e-6
N_SUB = 3
LANES = 128
SUBLANES = 8
VMEM_LIMIT_BYTES = 56 * 1024 * 1024

SCAN_CHUNK = 64
EXP_CLAMP = 80.0
CONV_PAD = 32
CONV_ROWS = 32
CONV_LANES = 512
NORM_ROWS = 16


def _cparams(*sem):
    return pltpu.CompilerParams(dimension_semantics=sem, vmem_limit_bytes=VMEM_LIMIT_BYTES)


def _dot(a, b):
    return jnp.dot(a, b, preferred_element_type=F32)


def _dot_nt(a, b):
    return lax.dot_general(a, b, (((1,), (1,)), ((), ())), preferred_element_type=F32)


def _dot_tn(a, b):
    return lax.dot_general(a, b, (((0,), (0,)), ((), ())), preferred_element_type=F32)


def _sigmoid(x):
    return 1.0 / (1.0 + jnp.exp(-x))


def _silu(x):
    return x * _sigmoid(x)


def _rmsnorm(x, g):
    return x * lax.rsqrt(jnp.mean(x * x, axis=-1, keepdims=True) + EPS) * g


def _row_loop(n_rows, fn):
    rows = min(NORM_ROWS, n_rows)
    steps = n_rows // rows
    if steps == 1:
        fn(pl.ds(0, rows))
        return

    def body(i, carry):
        fn(pl.ds(pl.multiple_of(i * rows, rows), rows))
        return carry

    lax.fori_loop(0, steps, body, 0)


def _mod_rows(ref, rows):
    return ref[...] if ref.shape[0] == 1 else ref[rows, :]


def _modulate_into(h_ref, x_ref, g_ref, sh_ref, sc_ref):
    def fn(rows):
        x = x_ref[rows, :]
        y = _rmsnorm(x, g_ref[...]) * (1.0 + _mod_rows(sc_ref, rows)) + _mod_rows(sh_ref, rows)
        h_ref[rows, :] = y.astype(BF16)

    _row_loop(x_ref.shape[0], fn)


def _ada_kernel(c_ref, w_ref, b_ref, o_ref):
    s = _silu(c_ref[...]).astype(BF16)
    o_ref[...] = _dot(s, w_ref[...].astype(BF16)) + b_ref[...]


def _ada_call(c, ada_w, ada_b, tn=1024):
    depth, d, n = ada_w.shape
    r = c.shape[0]
    return pl.pallas_call(
        _ada_kernel,
        grid=(depth, n // tn),
        in_specs=[
            pl.BlockSpec((r, d), lambda l, j: (0, 0)),
            pl.BlockSpec((None, d, tn), lambda l, j: (l, 0, j)),
            pl.BlockSpec((None, 1, tn), lambda l, j: (l, 0, j)),
        ],
        out_specs=pl.BlockSpec((None, r, tn), lambda l, j: (l, 0, j)),
        out_shape=jax.ShapeDtypeStruct((depth, r, n), F32),
        compiler_params=_cparams("arbitrary", "arbitrary"),
        name="adaln_params",
    )(c, ada_w, ada_b.reshape(depth, 1, n))


def _ffn_kernel(x_ref, sh_ref, sc_ref, gt_ref, g_ref, wa_ref, wu_ref, wo_ref, *rest, final):
    if final:
        fg_ref, o_ref, h_ref = rest
    else:
        o_ref, h_ref = rest
    f = pl.program_id(1)

    @pl.when(f == 0)
    def _():
        _modulate_into(h_ref, x_ref, g_ref, sh_ref, sc_ref)
        o_ref[...] = jnp.zeros_like(o_ref)

    h = h_ref[...]
    a = _dot(h, wa_ref[...].astype(BF16))
    u = _dot(h, wu_ref[...].astype(BF16))
    act = (_silu(a) * u).astype(BF16)
    o_ref[...] += _dot(act, wo_ref[...].astype(BF16))

    @pl.when(f == pl.num_programs(1) - 1)
    def _():
        def fn(rows):
            y = x_ref[rows, :] + 0.5 * _mod_rows(gt_ref, rows) * o_ref[rows, :]
            if final:
                y = _rmsnorm(y, fg_ref[...])
            o_ref[rows, :] = y

        _row_loop(x_ref.shape[0], fn)


def _ffn_call(x, mod, norm_g, w_in, w_out, final_g, *, layer, sub, j, tm, tf, final):
    m, d = x.shape
    groups, r = mod.shape[1], mod.shape[2]
    dff = w_out.shape[2]
    nf = dff // tf
    tiles_per_group = (m // tm) // groups

    def mspec(sec):
        return pl.BlockSpec((None, None, r, d), lambda i, f: (layer, i // tiles_per_group, 0, 3 * sub + sec))

    in_specs = [
        pl.BlockSpec((tm, d), lambda i, f: (i, 0)),
        mspec(0), mspec(1), mspec(2),
        pl.BlockSpec((None, None, 1, d), lambda i, f: (layer, sub, 0, 0)),
        pl.BlockSpec((None, None, d, tf), lambda i, f: (layer, j, 0, f)),
        pl.BlockSpec((None, None, d, tf), lambda i, f: (layer, j, 0, nf + f)),
        pl.BlockSpec((None, None, tf, d), lambda i, f: (layer, j, f, 0)),
    ]
    args = [x, mod, mod, mod, norm_g, w_in, w_in, w_out]
    if final:
        in_specs.append(pl.BlockSpec((1, d), lambda i, f: (0, 0)))
        args.append(final_g.reshape(1, d))
    return pl.pallas_call(
        functools.partial(_ffn_kernel, final=final),
        grid=(m // tm, nf),
        in_specs=in_specs,
        out_specs=pl.BlockSpec((tm, d), lambda i, f: (i, 0)),
        out_shape=jax.ShapeDtypeStruct((m, d), F32),
        scratch_shapes=[pltpu.VMEM((tm, d), BF16)],
        compiler_params=_cparams("arbitrary", "arbitrary"),
        name="ffn",
    )(*args)


def _hgrn_in_kernel(x_ref, sh_ref, sc_ref, g_ref, wq_ref, wf_ref, wi_ref, wg_ref, bf_ref, lb_ref,
                    q_ref, k_ref, v_ref, lf_ref, sg_ref, h_ref, *, lb_row):
    @pl.when(pl.program_id(1) == 0)
    def _():
        _modulate_into(h_ref, x_ref, g_ref, sh_ref, sc_ref)

    raw = [lb_ref[i:i + 1, :] for i in range(lb_ref.shape[0])]
    top = functools.reduce(jnp.maximum, raw)
    e = [jnp.exp(r - top) for r in raw]
    lb = sum(e[:lb_row + 1]) / sum(e)

    h = h_ref[...]
    q_ref[...] = _dot(h, wq_ref[...].astype(BF16))
    z = _dot(h, wf_ref[...].astype(BF16)) + bf_ref[...]
    lf_ref[...] = jnp.log(lb + (1.0 - lb) * _sigmoid(z))
    k_ref[...] = (1.0 - lb) * _sigmoid(-z)
    v_ref[...] = _dot(h, wi_ref[...].astype(BF16)).astype(BF16)
    sg_ref[...] = _silu(_dot(h, wg_ref[...].astype(BF16)))


def _hgrn_in_call(x, mod, norm_g, w_in, b_f, lb_raw, *, layer, j, lb_row, tm, tn):
    m, d = x.shape
    groups, r = mod.shape[1], mod.shape[2]
    nn = d // tn
    tiles_per_group = (m // tm) // groups
    sub = 1

    def mspec(sec):
        return pl.BlockSpec((None, None, r, d), lambda i, n: (layer, i // tiles_per_group, 0, 3 * sub + sec))

    def wspec(sec):
        return pl.BlockSpec((None, d, tn), lambda i, n: (j, 0, sec * nn + n))

    out_spec = pl.BlockSpec((tm, tn), lambda i, n: (i, n))
    return pl.pallas_call(
        functools.partial(_hgrn_in_kernel, lb_row=lb_row),
        grid=(m // tm, nn),
        in_specs=[
            pl.BlockSpec((tm, d), lambda i, n: (i, 0)),
            mspec(0), mspec(1),
            pl.BlockSpec((None, None, 1, d), lambda i, n: (layer, sub, 0, 0)),
            wspec(0), wspec(1), wspec(2), wspec(3),
            pl.BlockSpec((None, 1, tn), lambda i, n: (j, 0, n)),
            pl.BlockSpec((lb_raw.shape[0], tn), lambda i, n: (0, n)),
        ],
        out_specs=[out_spec] * 5,
        out_shape=[
            jax.ShapeDtypeStruct((m, d), F32),
            jax.ShapeDtypeStruct((m, d), F32),
            jax.ShapeDtypeStruct((m, d), BF16),
            jax.ShapeDtypeStruct((m, d), F32),
            jax.ShapeDtypeStruct((m, d), F32),
        ],
        scratch_shapes=[pltpu.VMEM((tm, d), BF16)],
        compiler_params=_cparams("arbitrary", "arbitrary"),
        name="hgrn_in",
    )(x, mod, mod, norm_g, w_in, w_in, w_in, w_in, b_f.reshape(b_f.shape[0], 1, d), lb_raw)


def _hgrn_scan_kernel(q_ref, k_ref, v_ref, lf_ref, sg_ref, gn_ref, y_ref, sfin_ref, st_ref, b_ref, *, heads):
    c = pl.program_id(1)
    chunk = q_ref.shape[0]
    dk = q_ref.shape[1] // heads

    @pl.when(c == 0)
    def _():
        st_ref[...] = jnp.zeros_like(st_ref)

    row = lax.broadcasted_iota(jnp.int32, (chunk, chunk), 0)
    col = lax.broadcasted_iota(jnp.int32, (chunk, chunk), 1)
    causal = col <= row
    tri = jnp.where(causal, 1.0, 0.0).astype(BF16)
    lf = lf_ref[...]
    p1 = lf.astype(BF16)
    r1 = lf - p1.astype(F32)
    p2 = r1.astype(BF16)
    p3 = (r1 - p2.astype(F32)).astype(BF16)
    b_ref[...] = _dot(tri, p1) + _dot(tri, p2) + _dot(tri, p3)

    mid = chunk // 2
    for h in range(heads):
        sl = slice(h * dk, (h + 1) * dk)
        b = b_ref[:, sl]
        b_mid = b_ref[mid:mid + 1, sl]
        b_last = b_ref[chunk - 1:chunk, sl]
        q = q_ref[:, sl]
        k = k_ref[:, sl]
        v = v_ref[:, sl]
        rel = b - b_mid
        q_in = (q * jnp.exp(jnp.clip(rel, -EXP_CLAMP, EXP_CLAMP))).astype(BF16)
        k_in = (k * jnp.exp(jnp.clip(-rel, -EXP_CLAMP, EXP_CLAMP))).astype(BF16)
        q_st = (q * jnp.exp(b)).astype(BF16)
        k_st = (k * jnp.exp(b_last - b)).astype(BF16)
        att = jnp.where(causal, _dot_nt(q_in, k_in), 0.0).astype(BF16)
        st = st_ref[h]
        o = _dot(att, v) + _dot_nt(q_st, st.astype(BF16))
        st_ref[h] = st * jnp.exp(b_last) + _dot_tn(v, k_st)
        y = _rmsnorm(o, gn_ref[:, sl]) * sg_ref[:, sl]
        y_ref[:, sl] = y.astype(BF16)

    @pl.when(c == pl.num_programs(1) - 1)
    def _():
        for h in range(heads):
            sfin_ref[h] = st_ref[h].T


def _hgrn_scan_call(q, k, v, lf, sg, g_norm, *, batch, heads):
    m, d = q.shape
    t = m // batch
    chunk = min(SCAN_CHUNK, t)
    nc = t // chunk
    dk = d // heads
    spec = pl.BlockSpec((chunk, d), lambda b, c: (b * nc + c, 0))
    return pl.pallas_call(
        functools.partial(_hgrn_scan_kernel, heads=heads),
        grid=(batch, nc),
        in_specs=[spec, spec, spec, spec, spec, pl.BlockSpec((1, d), lambda b, c: (0, 0))],
        out_specs=[spec, pl.BlockSpec((None, heads, dk, dk), lambda b, c: (b, 0, 0, 0))],
        out_shape=[jax.ShapeDtypeStruct((m, d), BF16), jax.ShapeDtypeStruct((batch, heads, dk, dk), F32)],
        scratch_shapes=[pltpu.VMEM((heads, dk, dk), F32), pltpu.VMEM((chunk, d), F32)],
        compiler_params=_cparams("arbitrary", "arbitrary"),
        name="hgrn_scan",
    )(q, k, v, lf, sg, g_norm.reshape(1, d))


def _hgrn_step_kernel(q_ref, k_ref, v_ref, lf_ref, sg_ref, gn_ref, s0_ref, y_ref, s1_ref, o_ref):
    nb = q_ref.shape[0]
    q_t = q_ref[...].T
    k_t = k_ref[...].T
    f_t = jnp.exp(lf_ref[...]).T
    v = v_ref[...].astype(F32)
    for b in range(nb):
        s1 = s0_ref[b] * f_t[:, b:b + 1] + k_t[:, b:b + 1] * v[b:b + 1, :]
        s1_ref[b] = s1
        o_ref[b:b + 1, :] = jnp.sum(s1 * q_t[:, b:b + 1], axis=0, keepdims=True)
    y = _rmsnorm(o_ref[...], gn_ref[...]) * sg_ref[...]
    y_ref[...] = y.astype(BF16)


def _hgrn_step_call(q, k, v, lf, sg, g_norm, s0, *, heads):
    nb, d = q.shape
    dk = d // heads
    spec = pl.BlockSpec((nb, dk), lambda h: (0, h))
    sspec = pl.BlockSpec((nb, None, dk, dk), lambda h: (0, h, 0, 0))
    return pl.pallas_call(
        _hgrn_step_kernel,
        grid=(heads,),
        in_specs=[spec, spec, spec, spec, spec, pl.BlockSpec((1, dk), lambda h: (0, h)), sspec],
        out_specs=[spec, sspec],
        out_shape=[jax.ShapeDtypeStruct((nb, d), BF16), jax.ShapeDtypeStruct(s0.shape, F32)],
        scratch_shapes=[pltpu.VMEM((nb, dk), F32)],
        compiler_params=_cparams("arbitrary"),
        name="hgrn_step",
    )(q, k, v, lf, sg, g_norm.reshape(1, d), s0)


def _proj_kernel(y_ref, w_ref, *rest, has_bias):
    if has_bias:
        b_ref, x_ref, gt_ref, o_ref = rest
    else:
        x_ref, gt_ref, o_ref = rest
    out = _dot(y_ref[...], w_ref[...].astype(BF16))
    if has_bias:
        out = out + b_ref[...]
    o_ref[...] = x_ref[...] + gt_ref[...] * out


def _proj_call(y, w, bias, x, mod, *, layer, j, tm, tn):
    m, kdim = y.shape
    d = x.shape[1]
    groups, r = mod.shape[1], mod.shape[2]
    nn = d // tn
    tiles_per_group = (m // tm) // groups
    gate_sec = 3 * 1 + 2
    has_bias = bias is not None
    in_specs = [
        pl.BlockSpec((tm, kdim), lambda n, i: (i, 0)),
        pl.BlockSpec((None, kdim, tn), lambda n, i: (j, 0, n)),
    ]
    args = [y, w]
    if has_bias:
        in_specs.append(pl.BlockSpec((None, 1, tn), lambda n, i: (j, 0, n)))
        args.append(bias.reshape(bias.shape[0], 1, d))
    in_specs += [
        pl.BlockSpec((tm, tn), lambda n, i: (i, n)),
        pl.BlockSpec((None, None, r, tn), lambda n, i: (layer, i // tiles_per_group, 0, gate_sec * nn + n)),
    ]
    args += [x, mod]
    return pl.pallas_call(
        functools.partial(_proj_kernel, has_bias=has_bias),
        grid=(nn, m // tm),
        in_specs=in_specs,
        out_specs=pl.BlockSpec((tm, tn), lambda n, i: (i, n)),
        out_shape=jax.ShapeDtypeStruct((m, d), F32),
        compiler_params=_cparams("arbitrary", "arbitrary"),
        name="proj_residual",
    )(*args)


def _pw1_kernel(x_ref, sh_ref, sc_ref, g_ref, wa_ref, wg_ref, ba_ref, bg_ref, u_ref, h_ref):
    @pl.when(pl.program_id(1) == 0)
    def _():
        _modulate_into(h_ref, x_ref, g_ref, sh_ref, sc_ref)

    h = h_ref[...]
    a = _dot(h, wa_ref[...].astype(BF16)) + ba_ref[...]
    gate = _dot(h, wg_ref[...].astype(BF16)) + bg_ref[...]
    u_ref[...] = a * _sigmoid(gate)


def _pw1_call(x, mod, norm_g, w, bias, *, layer, j, tm, tn):
    m, d = x.shape
    groups, r = mod.shape[1], mod.shape[2]
    ch = w.shape[2] // 2
    nn = ch // tn
    tiles_per_group = (m // tm) // groups
    sub = 1

    def mspec(sec):
        return pl.BlockSpec((None, None, r, d), lambda i, n: (layer, i // tiles_per_group, 0, 3 * sub + sec))

    def wspec(sec):
        return pl.BlockSpec((None, d, tn), lambda i, n: (j, 0, sec * nn + n))

    def bspec(sec):
        return pl.BlockSpec((None, 1, tn), lambda i, n: (j, 0, sec * nn + n))

    bias3 = bias.reshape(bias.shape[0], 1, 2 * ch)
    return pl.pallas_call(
        _pw1_kernel,
        grid=(m // tm, nn),
        in_specs=[
            pl.BlockSpec((tm, d), lambda i, n: (i, 0)),
            mspec(0), mspec(1),
            pl.BlockSpec((None, None, 1, d), lambda i, n: (layer, sub, 0, 0)),
            wspec(0), wspec(1), bspec(0), bspec(1),
        ],
        out_specs=pl.BlockSpec((tm, tn), lambda i, n: (i, n)),
        out_shape=jax.ShapeDtypeStruct((m, ch), F32),
        scratch_shapes=[pltpu.VMEM((tm, d), BF16)],
        compiler_params=_cparams("arbitrary", "arbitrary"),
        name="conv_pw1",
    )(x, mod, mod, norm_g, w, w, bias3, bias3)


def _ln_silu_into(y_ref, acc_ref, lng_ref, lnb_ref):
    def fn(rows):
        a = acc_ref[rows, :]
        mu = jnp.mean(a, axis=-1, keepdims=True)
        xc = a - mu
        yn = xc * lax.rsqrt(jnp.mean(xc * xc, axis=-1, keepdims=True) + EPS) * lng_ref[...] + lnb_ref[...]
        y_ref[rows, :] = _silu(yn).astype(BF16)

    _row_loop(acc_ref.shape[0], fn)


def _conv_seq_kernel(u_ref, w_ref, bdw_ref, lng_ref, lnb_ref, y_ref, tail_ref, win_ref, acc_ref, *, width):
    t = pl.program_id(1)
    tt, ch = u_ref.shape
    off = CONV_PAD - (width - 1)

    @pl.when(t == 0)
    def _():
        win_ref[0:CONV_PAD, :] = jnp.zeros((CONV_PAD, ch), F32)
        win_ref[CONV_PAD + tt:CONV_PAD + tt + SUBLANES, :] = jnp.zeros((SUBLANES, ch), F32)

    win_ref[CONV_PAD:CONV_PAD + tt, :] = u_ref[...]

    def body(i, carry):
        r0 = pl.multiple_of(i * CONV_ROWS, CONV_ROWS)
        for l0 in range(0, ch, CONV_LANES):
            ls = slice(l0, l0 + CONV_LANES)
            acc = jnp.zeros((CONV_ROWS, CONV_LANES), F32)
            for d in range(SUBLANES):
                part = None
                for jj in range((off + width - 1) // SUBLANES + 1):
                    tap = SUBLANES * jj + d - off
                    if 0 <= tap < width:
                        term = win_ref[pl.ds(r0 + SUBLANES * jj, CONV_ROWS + SUBLANES), ls] * w_ref[tap:tap + 1, ls]
                        part = term if part is None else part + term
                if part is not None:
                    acc = acc + part[d:d + CONV_ROWS, :]
            acc_ref[pl.ds(r0, CONV_ROWS), ls] = acc + bdw_ref[:, ls]
        return carry

    lax.fori_loop(0, tt // CONV_ROWS, body, 0)

    tail_ref[...] = win_ref[CONV_PAD + tt - (width - 1):CONV_PAD + tt, :]
    win_ref[0:CONV_PAD, :] = win_ref[tt:tt + CONV_PAD, :]
    _ln_silu_into(y_ref, acc_ref, lng_ref, lnb_ref)


def _conv_seq_call(u, w_dw, b_dw, ln_g, ln_b, *, j, batch, tt):
    m, ch = u.shape
    t = m // batch
    nt = t // tt
    width = w_dw.shape[1]
    vec = lambda a: a.reshape(a.shape[0], 1, ch)
    vspec = pl.BlockSpec((None, 1, ch), lambda b, i: (j, 0, 0))
    return pl.pallas_call(
        functools.partial(_conv_seq_kernel, width=width),
        grid=(batch, nt),
        in_specs=[
            pl.BlockSpec((tt, ch), lambda b, i: (b * nt + i, 0)),
            pl.BlockSpec((None, width, ch), lambda b, i: (j, 0, 0)),
            vspec, vspec, vspec,
        ],
        out_specs=[
            pl.BlockSpec((tt, ch), lambda b, i: (b * nt + i, 0)),
            pl.BlockSpec((None, width - 1, ch), lambda b, i: (b, 0, 0)),
        ],
        out_shape=[jax.ShapeDtypeStruct((m, ch), BF16), jax.ShapeDtypeStruct((batch, width - 1, ch), F32)],
        scratch_shapes=[pltpu.VMEM((CONV_PAD + tt + SUBLANES, ch), F32), pltpu.VMEM((tt, ch), F32)],
        compiler_params=_cparams("arbitrary", "arbitrary"),
        name="conv_seq",
    )(u, w_dw, vec(b_dw), vec(ln_g), vec(ln_b))


def _conv_step_kernel(u_ref, buf_ref, w_ref, bdw_ref, lng_ref, lnb_ref, y_ref, new_ref, acc_ref, *, width):
    u = u_ref[...]
    acc = u * w_ref[width - 1:width, :] + bdw_ref[...]
    for tap in range(width - 1):
        acc = acc + buf_ref[:, tap, :] * w_ref[tap:tap + 1, :]
    acc_ref[...] = acc
    new_ref[:, 0:width - 2, :] = buf_ref[:, 1:width - 1, :]
    new_ref[:, width - 2, :] = u
    _ln_silu_into(y_ref, acc_ref, lng_ref, lnb_ref)


def _conv_step_call(u, buf, w_dw, b_dw, ln_g, ln_b, *, j, tb):
    nb, ch = u.shape
    width = w_dw.shape[1]
    vec = lambda a: a.reshape(a.shape[0], 1, ch)
    vspec = pl.BlockSpec((None, 1, ch), lambda i: (j, 0, 0))
    return pl.pallas_call(
        functools.partial(_conv_step_kernel, width=width),
        grid=(nb // tb,),
        in_specs=[
            pl.BlockSpec((tb, ch), lambda i: (i, 0)),
            pl.BlockSpec((tb, width - 1, ch), lambda i: (i, 0, 0)),
            pl.BlockSpec((None, width, ch), lambda i: (j, 0, 0)),
            vspec, vspec, vspec,
        ],
        out_specs=[
            pl.BlockSpec((tb, ch), lambda i: (i, 0)),
            pl.BlockSpec((tb, width - 1, ch), lambda i: (i, 0, 0)),
        ],
        out_shape=[jax.ShapeDtypeStruct((nb, ch), BF16), jax.ShapeDtypeStruct(buf.shape, F32)],
        scratch_shapes=[pltpu.VMEM((tb, ch), F32)],
        compiler_params=_cparams("arbitrary"),
        name="conv_step",
    )(u, buf, w_dw, vec(b_dw), vec(ln_g), vec(ln_b))


def _run_group(x, mod, s_hgrn, s_conv, w, *, batch, tm):
    (hgrn_lb, hgrn_w_in, hgrn_b_f, hgrn_g_norm, hgrn_w_out, conv_w_pw1, conv_b_pw1, conv_w_dw, conv_b_dw,
     conv_ln_g, conv_ln_b, conv_w_pw2, conv_b_pw2, norm_g, ffn_w_in, ffn_w_out, final_g) = w
    depth = norm_g.shape[0]
    m, d = x.shape
    heads = d // LANES
    norm_g4 = norm_g.reshape(depth, N_SUB, 1, d)
    tf = 256
    tn_in = 256
    tn_out = min(1024, d)
    new_h, new_c = [], []
    for layer in range(depth):
        x = _ffn_call(x, mod, norm_g4, ffn_w_in, ffn_w_out, final_g,
                      layer=layer, sub=0, j=0, tm=tm, tf=tf, final=False)
        j = layer // 2
        if layer % 2 == 0:
            q, k, v, lf, sg = _hgrn_in_call(x, mod, norm_g4, hgrn_w_in, hgrn_b_f, hgrn_lb,
                                            layer=layer, j=j, lb_row=layer, tm=tm, tn=tn_in)
            if s_hgrn is None:
                y, s = _hgrn_scan_call(q, k, v, lf, sg, hgrn_g_norm[j], batch=batch, heads=heads)
            else:
                y, s = _hgrn_step_call(q, k, v, lf, sg, hgrn_g_norm[j], s_hgrn[j], heads=heads)
            new_h.append(s)
            x = _proj_call(y, hgrn_w_out, None, x, mod, layer=layer, j=j, tm=tm, tn=tn_out)
        else:
            u = _pw1_call(x, mod, norm_g4, conv_w_pw1, conv_b_pw1, layer=layer, j=j, tm=tm, tn=tn_in)
            if s_conv is None:
                y, s = _conv_seq_call(u, conv_w_dw, conv_b_dw, conv_ln_g, conv_ln_b, j=j, batch=batch, tt=256)
            else:
                y, s = _conv_step_call(u, s_conv[j], conv_w_dw, conv_b_dw, conv_ln_g, conv_ln_b, j=j, tb=SUBLANES)
            new_c.append(s)
            x = _proj_call(y, conv_w_pw2, conv_b_pw2, x, mod, layer=layer, j=j, tm=tm, tn=tn_out)
        x = _ffn_call(x, mod, norm_g4, ffn_w_in, ffn_w_out, final_g,
                      layer=layer, sub=2, j=1, tm=tm, tf=tf, final=(layer == depth - 1))
    return x, jnp.stack(new_h), jnp.stack(new_c)


def kernel(x_prompt, x_sample, state_hgrn, state_conv, c_prompt, c_sample, hgrn_lb, hgrn_w_in, hgrn_b_f, hgrn_g_norm, hgrn_w_out, conv_w_pw1, conv_b_pw1, conv_w_dw, conv_b_dw, conv_ln_g, conv_ln_b, conv_w_pw2, conv_b_pw2, ada_w, ada_b, norm_g, ffn_w_in, ffn_w_out, final_g):
    bp, t, d = x_prompt.shape
    bs = x_sample.shape[0]
    depth = ada_w.shape[0]
    weights = (hgrn_lb, hgrn_w_in, hgrn_b_f, hgrn_g_norm, hgrn_w_out, conv_w_pw1, conv_b_pw1, conv_w_dw, conv_b_dw,
               conv_ln_g, conv_ln_b, conv_w_pw2, conv_b_pw2, norm_g, ffn_w_in, ffn_w_out, final_g)

    mod = _ada_call(jnp.concatenate([c_sample, c_prompt], axis=0), ada_w, ada_b)
    mod_s = mod[:, :bs].reshape(depth, 1, bs, N_SUB * 3 * d)
    mod_p = mod[:, bs:].reshape(depth, bp, 1, N_SUB * 3 * d)

    yp, hp, cp = _run_group(x_prompt.reshape(bp * t, d), mod_p, None, None, weights, batch=bp, tm=1024)
    ys, hs, cs = _run_group(x_sample.reshape(bs, d), mod_s, state_hgrn, state_conv, weights, batch=bs, tm=bs)
    return (yp.reshape(bp, t, d), ys.reshape(bs, 1, d), hp, hs, cp, cs)
```

```python
import functools

import jax
import jax.numpy as jnp
from jax import lax
from jax.experimental import pallas as pl
from jax.experimental.pallas import tpu as pltpu

F32 = jnp.float32
BF16 = jnp.bfloat16
EPS = 1e-6
N_SUB = 3
LANES = 128
SUBLANES = 8
VMEM_LIMIT_BYTES = 60 * 1024 * 1024

SCAN_CHUNK = 64
EXP_CLAMP = 80.0
CONV_PAD = 32
CONV_ROWS = 32
CONV_LANES = 512
NORM_ROWS = 16
NORM_UNROLL = 4


def _cparams(*sem):
    return pltpu.CompilerParams(dimension_semantics=sem, vmem_limit_bytes=VMEM_LIMIT_BYTES)


def _dot(a, b):
    return jnp.dot(a, b, preferred_element_type=F32)


def _dot_nt(a, b):
    return lax.dot_general(a, b, (((1,), (1,)), ((), ())), preferred_element_type=F32)


def _dot_tn(a, b):
    return lax.dot_general(a, b, (((0,), (0,)), ((), ())), preferred_element_type=F32)


def _sigmoid(x):
    return 1.0 / (1.0 + jnp.exp(-x))


def _silu(x):
    return x * _sigmoid(x)


def _rmsnorm(x, g):
    return x * lax.rsqrt(jnp.mean(x * x, axis=-1, keepdims=True) + EPS) * g


def _row_loop(n_rows, fn):
    rows = min(NORM_ROWS, n_rows)
    steps = n_rows // rows
    if steps == 1:
        fn(pl.ds(0, rows))
        return

    def body(i, carry):
        fn(pl.ds(pl.multiple_of(i * rows, rows), rows))
        return carry

    lax.fori_loop(0, steps, body, 0, unroll=NORM_UNROLL if steps % NORM_UNROLL == 0 else 1)


def _mod_rows(ref, rows):
    return ref[...] if ref.shape[0] == 1 else ref[rows, :]


def _modulate_into(h_ref, x_ref, g_ref, sh_ref, sc_ref):
    per_tile = sc_ref.shape[0] == 1
    gain_tile = g_ref[...] * (1.0 + sc_ref[...]) if per_tile else None

    def fn(rows):
        x = x_ref[rows, :]
        gain = gain_tile if per_tile else g_ref[...] * (1.0 + sc_ref[rows, :])
        y = x * lax.rsqrt(jnp.mean(x * x, axis=-1, keepdims=True) + EPS) * gain + _mod_rows(sh_ref, rows)
        h_ref[rows, :] = y.astype(BF16)

    _row_loop(x_ref.shape[0], fn)


def _ada_kernel(cs_ref, cp_ref, w_ref, b_ref, os_ref, op_ref):
    w = w_ref[...].astype(BF16)
    os_ref[...] = _dot(_silu(cs_ref[...]).astype(BF16), w) + b_ref[...]
    op_ref[...] = _dot(_silu(cp_ref[...]).astype(BF16), w) + b_ref[...]


def _ada_call(c_sample, c_prompt, ada_w, ada_b, tn=1024):
    depth, d, n = ada_w.shape
    rs, rp = c_sample.shape[0], c_prompt.shape[0]
    return pl.pallas_call(
        _ada_kernel,
        grid=(depth, n // tn),
        in_specs=[
            pl.BlockSpec((rs, d), lambda l, j: (0, 0)),
            pl.BlockSpec((rp, d), lambda l, j: (0, 0)),
            pl.BlockSpec((None, d, tn), lambda l, j: (l, 0, j)),
            pl.BlockSpec((None, 1, tn), lambda l, j: (l, 0, j)),
        ],
        out_specs=[
            pl.BlockSpec((None, rs, tn), lambda l, j: (l, 0, j)),
            pl.BlockSpec((None, rp, tn), lambda l, j: (l, 0, j)),
        ],
        out_shape=[jax.ShapeDtypeStruct((depth, rs, n), F32), jax.ShapeDtypeStruct((depth, rp, n), F32)],
        compiler_params=_cparams("arbitrary", "arbitrary"),
        name="adaln_params",
    )(c_sample, c_prompt, ada_w, ada_b.reshape(depth, 1, n))


def _ffn_kernel(x_ref, sh_ref, sc_ref, gt_ref, g_ref, wa_ref, wu_ref, wo_ref, *rest, final):
    if final:
        fg_ref, o_ref, h_ref = rest
    else:
        o_ref, h_ref = rest
    f = pl.program_id(1)

    @pl.when(f == 0)
    def _():
        _modulate_into(h_ref, x_ref, g_ref, sh_ref, sc_ref)
        o_ref[...] = jnp.zeros_like(o_ref)

    h = h_ref[...]
    a = _dot(h, wa_ref[...].astype(BF16))
    u = _dot(h, wu_ref[...].astype(BF16))
    act = (_silu(a) * u).astype(BF16)
    o_ref[...] += _dot(act, wo_ref[...].astype(BF16))

    @pl.when(f == pl.num_programs(1) - 1)
    def _():
        def fn(rows):
            y = x_ref[rows, :] + 0.5 * _mod_rows(gt_ref, rows) * o_ref[rows, :]
            if final:
                y = _rmsnorm(y, fg_ref[...])
            o_ref[rows, :] = y

        _row_loop(x_ref.shape[0], fn)


def _ffn_call(x, mod, norm_g, w_in, w_out, final_g, *, layer, sub, j, tm, tf, final, single_buffer=False):
    m, d = x.shape
    groups, r = mod.shape[1], mod.shape[2]
    dff = w_out.shape[2]
    nf = dff // tf
    tiles_per_group = (m // tm) // groups
    row_mode = dict(pipeline_mode=pl.Buffered(1)) if single_buffer else {}

    def mspec(sec):
        return pl.BlockSpec((None, None, r, d), lambda i, f: (layer, i // tiles_per_group, 0, 3 * sub + sec))

    in_specs = [
        pl.BlockSpec((tm, d), lambda i, f: (i, 0), **row_mode),
        mspec(0), mspec(1), mspec(2),
        pl.BlockSpec((None, None, 1, d), lambda i, f: (layer, sub, 0, 0)),
        pl.BlockSpec((None, None, d, tf), lambda i, f: (layer, j, 0, f)),
        pl.BlockSpec((None, None, d, tf), lambda i, f: (layer, j, 0, nf + f)),
        pl.BlockSpec((None, None, tf, d), lambda i, f: (layer, j, f, 0)),
    ]
    args = [x, mod, mod, mod, norm_g, w_in, w_in, w_out]
    if final:
        in_specs.append(pl.BlockSpec((1, d), lambda i, f: (0, 0)))
        args.append(final_g.reshape(1, d))
    return pl.pallas_call(
        functools.partial(_ffn_kernel, final=final),
        grid=(m // tm, nf),
        in_specs=in_specs,
        out_specs=pl.BlockSpec((tm, d), lambda i, f: (i, 0), **row_mode),
        out_shape=jax.ShapeDtypeStruct((m, d), F32),
        scratch_shapes=[pltpu.VMEM((tm, d), BF16)],
        compiler_params=_cparams("arbitrary", "arbitrary"),
        name="ffn",
    )(*args)


def _hgrn_in_kernel(x_ref, sh_ref, sc_ref, g_ref, wq_ref, wf_ref, wi_ref, wg_ref, bf_ref, lb_ref,
                    q_ref, k_ref, v_ref, lf_ref, sg_ref, h_ref, *, lb_row):
    @pl.when(pl.program_id(1) == 0)
    def _():
        _modulate_into(h_ref, x_ref, g_ref, sh_ref, sc_ref)

    raw = [lb_ref[i:i + 1, :] for i in range(lb_ref.shape[0])]
    top = functools.reduce(jnp.maximum, raw)
    e = [jnp.exp(r - top) for r in raw]
    lb = sum(e[:lb_row + 1]) / sum(e)

    h = h_ref[...]
    q_ref[...] = _dot(h, wq_ref[...].astype(BF16))
    z = _dot(h, wf_ref[...].astype(BF16)) + bf_ref[...]
    lf_ref[...] = jnp.log(lb + (1.0 - lb) * _sigmoid(z))
    k_ref[...] = (1.0 - lb) * _sigmoid(-z)
    v_ref[...] = _dot(h, wi_ref[...].astype(BF16)).astype(BF16)
    sg_ref[...] = _silu(_dot(h, wg_ref[...].astype(BF16)))


def _hgrn_in_call(x, mod, norm_g, w_in, b_f, lb_raw, *, layer, j, lb_row, tm, tn):
    m, d = x.shape
    groups, r = mod.shape[1], mod.shape[2]
    nn = d // tn
    tiles_per_group = (m // tm) // groups
    sub = 1

    def mspec(sec):
        return pl.BlockSpec((None, None, r, d), lambda i, n: (layer, i // tiles_per_group, 0, 3 * sub + sec))

    def wspec(sec):
        return pl.BlockSpec((None, d, tn), lambda i, n: (j, 0, sec * nn + n))

    out_spec = pl.BlockSpec((tm, tn), lambda i, n: (i, n))
    return pl.pallas_call(
        functools.partial(_hgrn_in_kernel, lb_row=lb_row),
        grid=(m // tm, nn),
        in_specs=[
            pl.BlockSpec((tm, d), lambda i, n: (i, 0)),
            mspec(0), mspec(1),
            pl.BlockSpec((None, None, 1, d), lambda i, n: (layer, sub, 0, 0)),
            wspec(0), wspec(1), wspec(2), wspec(3),
            pl.BlockSpec((None, 1, tn), lambda i, n: (j, 0, n)),
            pl.BlockSpec((lb_raw.shape[0], tn), lambda i, n: (0, n)),
        ],
        out_specs=[out_spec] * 5,
        out_shape=[
            jax.ShapeDtypeStruct((m, d), F32),
            jax.ShapeDtypeStruct((m, d), F32),
            jax.ShapeDtypeStruct((m, d), BF16),
            jax.ShapeDtypeStruct((m, d), F32),
            jax.ShapeDtypeStruct((m, d), F32),
        ],
        scratch_shapes=[pltpu.VMEM((tm, d), BF16)],
        compiler_params=_cparams("arbitrary", "arbitrary"),
        name="hgrn_in",
    )(x, mod, mod, norm_g, w_in, w_in, w_in, w_in, b_f.reshape(b_f.shape[0], 1, d), lb_raw)


def _hgrn_scan_kernel(q_ref, k_ref, v_ref, lf_ref, sg_ref, gn_ref, y_ref, sfin_ref, st_ref, b_ref, *, heads):
    c = pl.program_id(1)
    chunk = q_ref.shape[0]
    dk = q_ref.shape[1] // heads

    @pl.when(c == 0)
    def _():
        st_ref[...] = jnp.zeros_like(st_ref)

    row = lax.broadcasted_iota(jnp.int32, (chunk, chunk), 0)
    col = lax.broadcasted_iota(jnp.int32, (chunk, chunk), 1)
    causal = col <= row
    tri = jnp.where(causal, 1.0, 0.0).astype(BF16)
    lf = lf_ref[...]
    p1 = lf.astype(BF16)
    r1 = lf - p1.astype(F32)
    p2 = r1.astype(BF16)
    p3 = (r1 - p2.astype(F32)).astype(BF16)
    b_ref[...] = _dot(tri, p1) + _dot(tri, p2) + _dot(tri, p3)

    mid = chunk // 2
    for h in range(heads):
        sl = slice(h * dk, (h + 1) * dk)
        b = b_ref[:, sl]
        b_mid = b_ref[mid:mid + 1, sl]
        b_last = b_ref[chunk - 1:chunk, sl]
        q = q_ref[:, sl]
        k = k_ref[:, sl]
        v = v_ref[:, sl]
        rel = b - b_mid
        q_in = (q * jnp.exp(jnp.clip(rel, -EXP_CLAMP, EXP_CLAMP))).astype(BF16)
        k_in = (k * jnp.exp(jnp.clip(-rel, -EXP_CLAMP, EXP_CLAMP))).astype(BF16)
        q_st = (q * jnp.exp(b)).astype(BF16)
        k_st = (k * jnp.exp(b_last - b)).astype(BF16)
        att = jnp.where(causal, _dot_nt(q_in, k_in), 0.0).astype(BF16)
        st = st_ref[h]
        o = _dot(att, v) + _dot_nt(q_st, st.astype(BF16))
        st_ref[h] = st * jnp.exp(b_last) + _dot_tn(v, k_st)
        y = _rmsnorm(o, gn_ref[:, sl]) * sg_ref[:, sl]
        y_ref[:, sl] = y.astype(BF16)

    @pl.when(c == pl.num_programs(1) - 1)
    def _():
        for h in range(heads):
            sfin_ref[h] = st_ref[h].T


def _hgrn_scan_call(q, k, v, lf, sg, g_norm, *, batch, heads):
    m, d = q.shape
    t = m // batch
    chunk = min(SCAN_CHUNK, t)
    nc = t // chunk
    dk = d // heads
    spec = pl.BlockSpec((chunk, d), lambda b, c: (b * nc + c, 0))
    return pl.pallas_call(
        functools.partial(_hgrn_scan_kernel, heads=heads),
        grid=(batch, nc),
        in_specs=[spec, spec, spec, spec, spec, pl.BlockSpec((1, d), lambda b, c: (0, 0))],
        out_specs=[spec, pl.BlockSpec((None, heads, dk, dk), lambda b, c: (b, 0, 0, 0))],
        out_shape=[jax.ShapeDtypeStruct((m, d), BF16), jax.ShapeDtypeStruct((batch, heads, dk, dk), F32)],
        scratch_shapes=[pltpu.VMEM((heads, dk, dk), F32), pltpu.VMEM((chunk, d), F32)],
        compiler_params=_cparams("arbitrary", "arbitrary"),
        name="hgrn_scan",
    )(q, k, v, lf, sg, g_norm.reshape(1, d))


def _hgrn_step_kernel(q_ref, k_ref, v_ref, lf_ref, sg_ref, gn_ref, s0_ref, y_ref, s1_ref, o_ref):
    nb = q_ref.shape[0]
    q_t = q_ref[...].T
    k_t = k_ref[...].T
    f_t = jnp.exp(lf_ref[...]).T
    v = v_ref[...].astype(F32)
    for b in range(nb):
        s1 = s0_ref[b] * f_t[:, b:b + 1] + k_t[:, b:b + 1] * v[b:b + 1, :]
        s1_ref[b] = s1
        o_ref[b:b + 1, :] = jnp.sum(s1 * q_t[:, b:b + 1], axis=0, keepdims=True)
    y = _rmsnorm(o_ref[...], gn_ref[...]) * sg_ref[...]
    y_ref[...] = y.astype(BF16)


def _hgrn_step_call(q, k, v, lf, sg, g_norm, s0, *, heads):
    nb, d = q.shape
    dk = d // heads
    spec = pl.BlockSpec((nb, dk), lambda h: (0, h))
    sspec = pl.BlockSpec((nb, None, dk, dk), lambda h: (0, h, 0, 0))
    return pl.pallas_call(
        _hgrn_step_kernel,
        grid=(heads,),
        in_specs=[spec, spec, spec, spec, spec, pl.BlockSpec((1, dk), lambda h: (0, h)), sspec],
        out_specs=[spec, sspec],
        out_shape=[jax.ShapeDtypeStruct((nb, d), BF16), jax.ShapeDtypeStruct(s0.shape, F32)],
        scratch_shapes=[pltpu.VMEM((nb, dk), F32)],
        compiler_params=_cparams("arbitrary"),
        name="hgrn_step",
    )(q, k, v, lf, sg, g_norm.reshape(1, d), s0)


def _proj_kernel(y_ref, w_ref, *rest, has_bias):
    if has_bias:
        b_ref, x_ref, gt_ref, o_ref = rest
    else:
        x_ref, gt_ref, o_ref = rest
    out = _dot(y_ref[...], w_ref[...].astype(BF16))
    if has_bias:
        out = out + b_ref[...]
    o_ref[...] = x_ref[...] + gt_ref[...] * out


def _proj_call(y, w, bias, x, mod, *, layer, j, tm, tn):
    m, kdim = y.shape
    d = x.shape[1]
    groups, r = mod.shape[1], mod.shape[2]
    nn = d // tn
    tiles_per_group = (m // tm) // groups
    gate_sec = 3 * 1 + 2
    has_bias = bias is not None
    in_specs = [
        pl.BlockSpec((tm, kdim), lambda n, i: (i, 0)),
        pl.BlockSpec((None, kdim, tn), lambda n, i: (j, 0, n)),
    ]
    args = [y, w]
    if has_bias:
        in_specs.append(pl.BlockSpec((None, 1, tn), lambda n, i: (j, 0, n)))
        args.append(bias.reshape(bias.shape[0], 1, d))
    in_specs += [
        pl.BlockSpec((tm, tn), lambda n, i: (i, n)),
        pl.BlockSpec((None, None, r, tn), lambda n, i: (layer, i // tiles_per_group, 0, gate_sec * nn + n)),
    ]
    args += [x, mod]
    return pl.pallas_call(
        functools.partial(_proj_kernel, has_bias=has_bias),
        grid=(nn, m // tm),
        in_specs=in_specs,
        out_specs=pl.BlockSpec((tm, tn), lambda n, i: (i, n)),
        out_shape=jax.ShapeDtypeStruct((m, d), F32),
        compiler_params=_cparams("arbitrary", "arbitrary"),
        name="proj_residual",
    )(*args)


def _pw1_kernel(x_ref, sh_ref, sc_ref, g_ref, wa_ref, wg_ref, ba_ref, bg_ref, u_ref, h_ref):
    @pl.when(pl.program_id(1) == 0)
    def _():
        _modulate_into(h_ref, x_ref, g_ref, sh_ref, sc_ref)

    h = h_ref[...]
    a = _dot(h, wa_ref[...].astype(BF16)) + ba_ref[...]
    gate = _dot(h, wg_ref[...].astype(BF16)) + bg_ref[...]
    u_ref[...] = a * _sigmoid(gate)


def _pw1_call(x, mod, norm_g, w, bias, *, layer, j, tm, tn):
    m, d = x.shape
    groups, r = mod.shape[1], mod.shape[2]
    ch = w.shape[2] // 2
    nn = ch // tn
    tiles_per_group = (m // tm) // groups
    sub = 1

    def mspec(sec):
        return pl.BlockSpec((None, None, r, d), lambda i, n: (layer, i // tiles_per_group, 0, 3 * sub + sec))

    def wspec(sec):
        return pl.BlockSpec((None, d, tn), lambda i, n: (j, 0, sec * nn + n))

    def bspec(sec):
        return pl.BlockSpec((None, 1, tn), lambda i, n: (j, 0, sec * nn + n))

    bias3 = bias.reshape(bias.shape[0], 1, 2 * ch)
    return pl.pallas_call(
        _pw1_kernel,
        grid=(m // tm, nn),
        in_specs=[
            pl.BlockSpec((tm, d), lambda i, n: (i, 0)),
            mspec(0), mspec(1),
            pl.BlockSpec((None, None, 1, d), lambda i, n: (layer, sub, 0, 0)),
            wspec(0), wspec(1), bspec(0), bspec(1),
        ],
        out_specs=pl.BlockSpec((tm, tn), lambda i, n: (i, n)),
        out_shape=jax.ShapeDtypeStruct((m, ch), F32),
        scratch_shapes=[pltpu.VMEM((tm, d), BF16)],
        compiler_params=_cparams("arbitrary", "arbitrary"),
        name="conv_pw1",
    )(x, mod, mod, norm_g, w, w, bias3, bias3)


def _ln_silu_into(y_ref, acc_ref, lng_ref, lnb_ref):
    def fn(rows):
        a = acc_ref[rows, :]
        mu = jnp.mean(a, axis=-1, keepdims=True)
        xc = a - mu
        yn = xc * lax.rsqrt(jnp.mean(xc * xc, axis=-1, keepdims=True) + EPS) * lng_ref[...] + lnb_ref[...]
        y_ref[rows, :] = _silu(yn).astype(BF16)

    _row_loop(acc_ref.shape[0], fn)


def _conv_seq_kernel(u_ref, w_ref, bdw_ref, lng_ref, lnb_ref, y_ref, tail_ref, win_ref, acc_ref, *, width):
    t = pl.program_id(1)
    tt, ch = u_ref.shape
    off = CONV_PAD - (width - 1)

    @pl.when(t == 0)
    def _():
        win_ref[0:CONV_PAD, :] = jnp.zeros((CONV_PAD, ch), F32)
        win_ref[CONV_PAD + tt:CONV_PAD + tt + SUBLANES, :] = jnp.zeros((SUBLANES, ch), F32)

    win_ref[CONV_PAD:CONV_PAD + tt, :] = u_ref[...]

    def body(i, carry):
        r0 = pl.multiple_of(i * CONV_ROWS, CONV_ROWS)
        for l0 in range(0, ch, CONV_LANES):
            ls = slice(l0, l0 + CONV_LANES)
            acc = jnp.zeros((CONV_ROWS, CONV_LANES), F32)
            for d in range(SUBLANES):
                part = None
                for jj in range((off + width - 1) // SUBLANES + 1):
                    tap = SUBLANES * jj + d - off
                    if 0 <= tap < width:
                        term = win_ref[pl.ds(r0 + SUBLANES * jj, CONV_ROWS + SUBLANES), ls] * w_ref[tap:tap + 1, ls]
                        part = term if part is None else part + term
                if part is not None:
                    acc = acc + part[d:d + CONV_ROWS, :]
            acc_ref[pl.ds(r0, CONV_ROWS), ls] = acc + bdw_ref[:, ls]
        return carry

    lax.fori_loop(0, tt // CONV_ROWS, body, 0)

    tail_ref[...] = win_ref[CONV_PAD + tt - (width - 1):CONV_PAD + tt, :]
    win_ref[0:CONV_PAD, :] = win_ref[tt:tt + CONV_PAD, :]
    _ln_silu_into(y_ref, acc_ref, lng_ref, lnb_ref)


def _conv_seq_call(u, w_dw, b_dw, ln_g, ln_b, *, j, batch, tt):
    m, ch = u.shape
    t = m // batch
    nt = t // tt
    width = w_dw.shape[1]
    vec = lambda a: a.reshape(a.shape[0], 1, ch)
    vspec = pl.BlockSpec((None, 1, ch), lambda b, i: (j, 0, 0))
    return pl.pallas_call(
        functools.partial(_conv_seq_kernel, width=width),
        grid=(batch, nt),
        in_specs=[
            pl.BlockSpec((tt, ch), lambda b, i: (b * nt + i, 0)),
            pl.BlockSpec((None, width, ch), lambda b, i: (j, 0, 0)),
            vspec, vspec, vspec,
        ],
        out_specs=[
            pl.BlockSpec((tt, ch), lambda b, i: (b * nt + i, 0)),
            pl.BlockSpec((None, width - 1, ch), lambda b, i: (b, 0, 0)),
        ],
        out_shape=[jax.ShapeDtypeStruct((m, ch), BF16), jax.ShapeDtypeStruct((batch, width - 1, ch), F32)],
        scratch_shapes=[pltpu.VMEM((CONV_PAD + tt + SUBLANES, ch), F32), pltpu.VMEM((tt, ch), F32)],
        compiler_params=_cparams("arbitrary", "arbitrary"),
        name="conv_seq",
    )(u, w_dw, vec(b_dw), vec(ln_g), vec(ln_b))


def _conv_step_kernel(u_ref, buf_ref, w_ref, bdw_ref, lng_ref, lnb_ref, y_ref, new_ref, acc_ref, *, width):
    u = u_ref[...]
    acc = u * w_ref[width - 1:width, :] + bdw_ref[...]
    for tap in range(width - 1):
        acc = acc + buf_ref[tap] * w_ref[tap:tap + 1, :]
    acc_ref[...] = acc
    for tap in range(width - 2):
        new_ref[tap] = buf_ref[tap + 1]
    new_ref[width - 2] = u
    _ln_silu_into(y_ref, acc_ref, lng_ref, lnb_ref)


def _conv_step_call(u, buf, w_dw, b_dw, ln_g, ln_b, *, j, tb):
    nb, ch = u.shape
    width = w_dw.shape[1]
    vec = lambda a: a.reshape(a.shape[0], 1, ch)
    vspec = pl.BlockSpec((None, 1, ch), lambda i: (j, 0, 0))
    return pl.pallas_call(
        functools.partial(_conv_step_kernel, width=width),
        grid=(nb // tb,),
        in_specs=[
            pl.BlockSpec((tb, ch), lambda i: (i, 0)),
            pl.BlockSpec((None, width - 1, tb, ch), lambda i: (j, 0, i, 0)),
            pl.BlockSpec((None, width, ch), lambda i: (j, 0, 0)),
            vspec, vspec, vspec,
        ],
        out_specs=[
            pl.BlockSpec((tb, ch), lambda i: (i, 0)),
            pl.BlockSpec((width - 1, tb, ch), lambda i: (0, i, 0)),
        ],
        out_shape=[jax.ShapeDtypeStruct((nb, ch), BF16), jax.ShapeDtypeStruct((width - 1, nb, ch), F32)],
        scratch_shapes=[pltpu.VMEM((tb, ch), F32)],
        compiler_params=_cparams("arbitrary"),
        name="conv_step",
    )(u, buf, w_dw, vec(b_dw), vec(ln_g), vec(ln_b))


def _run_group(x, mod, s_hgrn, s_conv, w, *, batch, tm, ffn_cfg=None):
    ffn_cfg = ffn_cfg or {}
    (hgrn_lb, hgrn_w_in, hgrn_b_f, hgrn_g_norm, hgrn_w_out, conv_w_pw1, conv_b_pw1, conv_w_dw, conv_b_dw,
     conv_ln_g, conv_ln_b, conv_w_pw2, conv_b_pw2, norm_g, ffn_w_in, ffn_w_out, final_g) = w
    depth = norm_g.shape[0]
    m, d = x.shape
    heads = d // LANES
    norm_g4 = norm_g.reshape(depth, N_SUB, 1, d)
    tf = 256
    tn_in = 256
    tn_out = min(1024, d)
    new_h, new_c = [], []
    for layer in range(depth):
        cfg = dict(tm=tm, tf=tf)
        cfg.update(ffn_cfg.get((layer, 0), {}))
        x = _ffn_call(x, mod, norm_g4, ffn_w_in, ffn_w_out, final_g, layer=layer, sub=0, j=0, final=False, **cfg)
        j = layer // 2
        if layer % 2 == 0:
            q, k, v, lf, sg = _hgrn_in_call(x, mod, norm_g4, hgrn_w_in, hgrn_b_f, hgrn_lb,
                                            layer=layer, j=j, lb_row=layer, tm=tm, tn=tn_in)
            if s_hgrn is None:
                y, s = _hgrn_scan_call(q, k, v, lf, sg, hgrn_g_norm[j], batch=batch, heads=heads)
            else:
                y, s = _hgrn_step_call(q, k, v, lf, sg, hgrn_g_norm[j], s_hgrn[j], heads=heads)
            new_h.append(s)
            x = _proj_call(y, hgrn_w_out, None, x, mod, layer=layer, j=j, tm=tm, tn=tn_out)
        else:
            u = _pw1_call(x, mod, norm_g4, conv_w_pw1, conv_b_pw1, layer=layer, j=j, tm=tm, tn=tn_in)
            if s_conv is None:
                y, s = _conv_seq_call(u, conv_w_dw, conv_b_dw, conv_ln_g, conv_ln_b, j=j, batch=batch, tt=256)
                s = jnp.transpose(s, (1, 0, 2))
            else:
                y, s = _conv_step_call(u, s_conv, conv_w_dw, conv_b_dw, conv_ln_g, conv_ln_b, j=j, tb=4 * SUBLANES)
            new_c.append(s)
            x = _proj_call(y, conv_w_pw2, conv_b_pw2, x, mod, layer=layer, j=j, tm=tm, tn=tn_out)
        cfg = dict(tm=tm, tf=tf)
        cfg.update(ffn_cfg.get((layer, 2), {}))
        x = _ffn_call(x, mod, norm_g4, ffn_w_in, ffn_w_out, final_g,
                      layer=layer, sub=2, j=1, final=(layer == depth - 1), **cfg)
    return x, jnp.stack(new_h), jnp.stack(new_c)


def kernel(x_prompt, x_sample, state_hgrn, state_conv, c_prompt, c_sample, hgrn_lb, hgrn_w_in, hgrn_b_f, hgrn_g_norm, hgrn_w_out, conv_w_pw1, conv_b_pw1, conv_w_dw, conv_b_dw, conv_ln_g, conv_ln_b, conv_w_pw2, conv_b_pw2, ada_w, ada_b, norm_g, ffn_w_in, ffn_w_out, final_g):
    bp, t, d = x_prompt.shape
    bs = x_sample.shape[0]
    depth = ada_w.shape[0]
    weights = (hgrn_lb, hgrn_w_in, hgrn_b_f, hgrn_g_norm, hgrn_w_out, conv_w_pw1, conv_b_pw1, conv_w_dw, conv_b_dw,
               conv_ln_g, conv_ln_b, conv_w_pw2, conv_b_pw2, norm_g, ffn_w_in, ffn_w_out, final_g)

    mod_s, mod_p = _ada_call(c_sample, c_prompt, ada_w, ada_b)
    mod_s = mod_s.reshape(depth, 1, bs, N_SUB * 3 * d)
    mod_p = mod_p.reshape(depth, bp, 1, N_SUB * 3 * d)

    ffn_cfg = {
        (0, 2): dict(tm=2048, tf=256, single_buffer=True),
        (1, 0): dict(tm=1024, tf=512, single_buffer=True),
        (1, 2): dict(tm=512, tf=512),
    }
    time_major = lambda s: jnp.transpose(s, (0, 2, 1, 3))
    yp, hp, cp = _run_group(x_prompt.reshape(bp * t, d), mod_p, None, None, weights, batch=bp, tm=1024,
                            ffn_cfg=ffn_cfg)
    ys, hs, cs = _run_group(x_sample.reshape(bs, d), mod_s, state_hgrn, time_major(state_conv), weights,
                            batch=bs, tm=bs)
    return (yp.reshape(bp, t, d), ys.reshape(bs, 1, d), hp, hs, time_major(cp), time_major(cs))
```

```python
import functools

import jax
import jax.numpy as jnp
from jax import lax
from jax.experimental import pallas as pl
from jax.experimental.pallas import tpu as pltpu

F32 = jnp.float32
BF16 = jnp.bfloat16
EPS = 1e-6
N_SUB = 3
LANES = 128
SUBLANES = 8
VMEM_LIMIT_BYTES = 60 * 1024 * 1024

ROW_TILE = 1024
FFN_TILE = 256
IN_TILE = 256
OUT_TILE = 1024
SCAN_ROWS = 256
SCAN_SUB = 64
SCAN_SPAN_MAX = 120.0
SCAN_SLOW_ROWS = 16
CONV_PAD = 32
CONV_TILE = 256
NORM_ROWS = 16
NORM_UNROLL = 4


def _cparams(*sem):
    return pltpu.CompilerParams(dimension_semantics=sem, vmem_limit_bytes=VMEM_LIMIT_BYTES)


def _dot(a, b):
    return jnp.dot(a, b, preferred_element_type=F32)


def _dot_nt(a, b):
    return lax.dot_general(a, b, (((1,), (1,)), ((), ())), preferred_element_type=F32)


def _dot_tn(a, b):
    return lax.dot_general(a, b, (((0,), (0,)), ((), ())), preferred_element_type=F32)


def _sigmoid(x):
    return 1.0 / (1.0 + jnp.exp(-x))


def _silu(x):
    return x * _sigmoid(x)


def _rmsnorm(x, g):
    return x * lax.rsqrt(jnp.mean(x * x, axis=-1, keepdims=True) + EPS) * g


def _row_loop(n_rows, fn):
    rows = min(NORM_ROWS, n_rows)
    steps = n_rows // rows
    if steps == 1:
        fn(pl.ds(0, rows))
        return

    def body(i, carry):
        fn(pl.ds(pl.multiple_of(i * rows, rows), rows))
        return carry

    lax.fori_loop(0, steps, body, 0, unroll=NORM_UNROLL if steps % NORM_UNROLL == 0 else 1)


def _mod_rows(ref, rows):
    return ref[...] if ref.shape[0] == 1 else ref[rows, :]


def _modulate_into(h_ref, x_ref, g_ref, sh_ref, sc_ref):
    per_tile = sc_ref.shape[0] == 1
    gain_tile = g_ref[...] * (1.0 + sc_ref[...]) if per_tile else None

    def fn(rows):
        x = x_ref[rows, :]
        gain = gain_tile if per_tile else g_ref[...] * (1.0 + sc_ref[rows, :])
        y = x * lax.rsqrt(jnp.mean(x * x, axis=-1, keepdims=True) + EPS) * gain + _mod_rows(sh_ref, rows)
        h_ref[rows, :] = y.astype(BF16)

    _row_loop(x_ref.shape[0], fn)


def _once(shape, index_map):
    return pl.BlockSpec(shape, index_map, pipeline_mode=pl.Buffered(1))


def _mod_specs(mod, layer, secs, width, col_of, row_tiles):
    groups, r = mod.shape[1], mod.shape[2]
    tiles_per_group = max(row_tiles // groups, 1)
    ncol = mod.shape[3] // (N_SUB * 3) // width

    def spec(sec):
        def index_map(*idx):
            i, n = col_of(*idx)
            return (layer, i // tiles_per_group, 0, sec * ncol + n)
        if groups == 1 and ncol == 1:
            return _once((None, None, r, width), index_map)
        return pl.BlockSpec((None, None, r, width), index_map)

    return [spec(s) for s in secs]


def _ada_kernel(cs_ref, cp_ref, w_ref, b_ref, os_ref, op_ref):
    w = w_ref[...].astype(BF16)
    os_ref[...] = _dot(_silu(cs_ref[...]).astype(BF16), w) + b_ref[...]
    op_ref[...] = _dot(_silu(cp_ref[...]).astype(BF16), w) + b_ref[...]


def _ada_call(c_sample, c_prompt, ada_w, ada_b, tn=1024):
    depth, d, n = ada_w.shape
    rs, rp = c_sample.shape[0], c_prompt.shape[0]
    return pl.pallas_call(
        _ada_kernel,
        grid=(depth, n // tn),
        in_specs=[
            pl.BlockSpec((rs, d), lambda l, j: (0, 0)),
            pl.BlockSpec((rp, d), lambda l, j: (0, 0)),
            pl.BlockSpec((None, d, tn), lambda l, j: (l, 0, j)),
            pl.BlockSpec((None, 1, tn), lambda l, j: (l, 0, j)),
        ],
        out_specs=[
            pl.BlockSpec((None, rs, tn), lambda l, j: (l, 0, j)),
            pl.BlockSpec((None, rp, tn), lambda l, j: (l, 0, j)),
        ],
        out_shape=[jax.ShapeDtypeStruct((depth, rs, n), F32), jax.ShapeDtypeStruct((depth, rp, n), F32)],
        compiler_params=_cparams("arbitrary", "arbitrary"),
        name="adaln_params",
    )(c_sample, c_prompt, ada_w, ada_b.reshape(depth, 1, n))


def _ffn_rows(x_ref, sh_ref, sc_ref, gt_ref, g_ref, fg_ref, o_ref, h_ref, wa, wu, wo):
    f = pl.program_id(1)

    @pl.when(f == 0)
    def _():
        _modulate_into(h_ref, x_ref, g_ref, sh_ref, sc_ref)
        o_ref[...] = jnp.zeros_like(o_ref)

    h = h_ref[...]
    act = (_silu(_dot(h, wa)) * _dot(h, wu)).astype(BF16)
    o_ref[...] += _dot(act, wo)

    @pl.when(f == pl.num_programs(1) - 1)
    def _():
        def fn(rows):
            y = x_ref[rows, :] + 0.5 * _mod_rows(gt_ref, rows) * o_ref[rows, :]
            if fg_ref is not None:
                y = _rmsnorm(y, fg_ref[...])
            o_ref[rows, :] = y

        _row_loop(x_ref.shape[0], fn)


def _ffn_kernel(x_ref, sh_ref, sc_ref, gt_ref, xs_ref, shs_ref, scs_ref, gts_ref, g_ref, wa_ref, wu_ref, wo_ref,
                *rest, final):
    fg_ref = rest[0] if final else None
    o_ref, os_ref, h_ref, hs_ref = rest[-4:]
    wa = wa_ref[...].astype(BF16)
    wu = wu_ref[...].astype(BF16)
    wo = wo_ref[...].astype(BF16)
    _ffn_rows(x_ref, sh_ref, sc_ref, gt_ref, g_ref, fg_ref, o_ref, h_ref, wa, wu, wo)

    @pl.when(pl.program_id(0) == 0)
    def _():
        _ffn_rows(xs_ref, shs_ref, scs_ref, gts_ref, g_ref, fg_ref, os_ref, hs_ref, wa, wu, wo)


def _ffn_call(x, xs, mod, mod_s, norm_g, w_in, w_out, final_g, *, layer, sub, j, final):
    m, d = x.shape
    ms = xs.shape[0]
    tm, tf = ROW_TILE, FFN_TILE
    nf = w_out.shape[2] // tf
    secs = [3 * sub, 3 * sub + 1, 3 * sub + 2]
    in_specs = (
        [pl.BlockSpec((tm, d), lambda i, f: (i, 0))]
        + _mod_specs(mod, layer, secs, d, lambda i, f: (i, 0), m // tm)
        + [_once((ms, d), lambda i, f: (0, 0))]
        + _mod_specs(mod_s, layer, secs, d, lambda i, f: (0, 0), 1)
        + [
            pl.BlockSpec((None, None, 1, d), lambda i, f: (layer, sub, 0, 0)),
            pl.BlockSpec((None, None, d, tf), lambda i, f: (layer, j, 0, f)),
            pl.BlockSpec((None, None, d, tf), lambda i, f: (layer, j, 0, nf + f)),
            pl.BlockSpec((None, None, tf, d), lambda i, f: (layer, j, f, 0)),
        ]
    )
    args = [x, mod, mod, mod, xs, mod_s, mod_s, mod_s, norm_g, w_in, w_in, w_out]
    if final:
        in_specs.append(pl.BlockSpec((1, d), lambda i, f: (0, 0)))
        args.append(final_g.reshape(1, d))
    return pl.pallas_call(
        functools.partial(_ffn_kernel, final=final),
        grid=(m // tm, nf),
        in_specs=in_specs,
        out_specs=[pl.BlockSpec((tm, d), lambda i, f: (i, 0)), _once((ms, d), lambda i, f: (0, 0))],
        out_shape=[jax.ShapeDtypeStruct((m, d), F32), jax.ShapeDtypeStruct((ms, d), F32)],
        scratch_shapes=[pltpu.VMEM((tm, d), BF16), pltpu.VMEM((ms, d), BF16)],
        compiler_params=_cparams("arbitrary", "arbitrary"),
        name="ffn",
    )(*args)


def _hgrn_in_rows(x_ref, sh_ref, sc_ref, g_ref, bf_ref, h_ref, outs, ws, lb):
    q_ref, k_ref, v_ref, lf_ref, sg_ref = outs
    wq, wf, wi, wg = ws

    @pl.when(pl.program_id(1) == 0)
    def _():
        _modulate_into(h_ref, x_ref, g_ref, sh_ref, sc_ref)

    h = h_ref[...]
    q_ref[...] = _dot(h, wq)
    z = _dot(h, wf) + bf_ref[...]
    lf_ref[...] = jnp.log(lb + (1.0 - lb) * _sigmoid(z))
    k_ref[...] = (1.0 - lb) * _sigmoid(-z)
    v_ref[...] = _dot(h, wi).astype(BF16)
    sg_ref[...] = _silu(_dot(h, wg))


def _hgrn_in_kernel(x_ref, sh_ref, sc_ref, xs_ref, shs_ref, scs_ref, g_ref, wq_ref, wf_ref, wi_ref, wg_ref,
                    bf_ref, lb_ref, *rest, lb_row):
    outs, outs_s, (h_ref, hs_ref) = rest[0:5], rest[5:10], rest[10:12]
    raw = [lb_ref[i:i + 1, :] for i in range(lb_ref.shape[0])]
    top = functools.reduce(jnp.maximum, raw)
    e = [jnp.exp(r - top) for r in raw]
    lb = sum(e[:lb_row + 1]) / sum(e)
    ws = [w[...].astype(BF16) for w in (wq_ref, wf_ref, wi_ref, wg_ref)]
    _hgrn_in_rows(x_ref, sh_ref, sc_ref, g_ref, bf_ref, h_ref, outs, ws, lb)

    @pl.when(pl.program_id(0) == 0)
    def _():
        _hgrn_in_rows(xs_ref, shs_ref, scs_ref, g_ref, bf_ref, hs_ref, outs_s, ws, lb)


def _side_col(nn):
    return lambda i, n: (0, jnp.where(i == 0, n, nn - 1))


def _hgrn_in_call(x, xs, mod, mod_s, norm_g, w_in, b_f, lb_raw, *, layer, j, lb_row):
    m, d = x.shape
    ms = xs.shape[0]
    tm, tn = ROW_TILE, IN_TILE
    nn = d // tn
    sub = 1
    secs = [3 * sub, 3 * sub + 1]

    def wspec(sec):
        return pl.BlockSpec((None, d, tn), lambda i, n: (j, 0, sec * nn + n))

    def out_shapes(rows):
        return [jax.ShapeDtypeStruct((rows, d), dt) for dt in (F32, F32, BF16, F32, F32)]

    return pl.pallas_call(
        functools.partial(_hgrn_in_kernel, lb_row=lb_row),
        grid=(m // tm, nn),
        in_specs=(
            [pl.BlockSpec((tm, d), lambda i, n: (i, 0))]
            + _mod_specs(mod, layer, secs, d, lambda i, n: (i, 0), m // tm)
            + [_once((ms, d), lambda i, n: (0, 0))]
            + _mod_specs(mod_s, layer, secs, d, lambda i, n: (0, 0), 1)
            + [
                pl.BlockSpec((None, None, 1, d), lambda i, n: (layer, sub, 0, 0)),
                wspec(0), wspec(1), wspec(2), wspec(3),
                pl.BlockSpec((None, 1, tn), lambda i, n: (j, 0, n)),
                pl.BlockSpec((lb_raw.shape[0], tn), lambda i, n: (0, n)),
            ]
        ),
        out_specs=[pl.BlockSpec((tm, tn), lambda i, n: (i, n))] * 5 + [pl.BlockSpec((ms, tn), _side_col(nn))] * 5,
        out_shape=out_shapes(m) + out_shapes(ms),
        scratch_shapes=[pltpu.VMEM((tm, d), BF16), pltpu.VMEM((ms, d), BF16)],
        compiler_params=_cparams("arbitrary", "arbitrary"),
        name="hgrn_in",
    )(x, mod, mod, xs, mod_s, mod_s, norm_g, w_in, w_in, w_in, w_in, b_f.reshape(b_f.shape[0], 1, d), lb_raw)


def _hgrn_scan_kernel(q_ref, k_ref, v_ref, lf_ref, sg_ref, gn_ref, y_ref, sfin_ref, st_ref, b_ref, o_ref, *,
                      heads, span_max):
    c = pl.program_id(1)
    rows, d = q_ref.shape
    dk = d // heads
    sub = min(SCAN_SUB, rows)
    n_sub = rows // sub
    n_levels = n_sub.bit_length() - 1
    sub_shift = sub.bit_length() - 1

    @pl.when(c == 0)
    def _():
        st_ref[...] = jnp.zeros_like(st_ref)

    row = lax.broadcasted_iota(jnp.int32, (rows, rows), 0)
    col = lax.broadcasted_iota(jnp.int32, (rows, rows), 1)
    tri = jnp.where(col <= row, 1.0, 0.0).astype(BF16)
    lf = lf_ref[...]
    p1 = lf.astype(BF16)
    r1 = lf - p1.astype(F32)
    p2 = r1.astype(BF16)
    p3 = (r1 - p2.astype(F32)).astype(BF16)
    b_ref[...] = _dot(tri, p1) + _dot(tri, p2) + _dot(tri, p3)

    def first_row(j):
        return slice(j * sub, j * sub + 1)

    def last_row(j):
        return slice((j + 1) * sub - 1, (j + 1) * sub)

    spans = [b_ref[first_row(j), :] - b_ref[last_row(j), :] for j in range(n_sub)]
    safe = jnp.max(functools.reduce(jnp.maximum, spans)) <= span_max

    def gated_out(o, sl):
        return (_rmsnorm(o, gn_ref[:, sl]) * sg_ref[:, sl]).astype(BF16)

    @pl.when(safe)
    def _():
        srl = lax.shift_right_logical
        level = jnp.where((srl(row, sub_shift) == srl(col, sub_shift)) & (col <= row), 0, -1)
        later_half = []
        row_d = lax.broadcasted_iota(jnp.int32, (rows, dk), 0)
        for l in range(1, n_levels + 1):
            s = sub_shift + l - 1
            hit = (srl(row, s + 1) == srl(col, s + 1)) & ((srl(row, s) & 1) == 1) & ((srl(col, s) & 1) == 0)
            level = jnp.where(hit, l, level)
            later_half.append((srl(row_d, s) & 1) == 1)

        def rows_of(pieces, n):
            return jnp.concatenate([jnp.broadcast_to(p, (n, dk)) for p in pieces], axis=0)

        for h in range(heads):
            sl = slice(h * dk, (h + 1) * dk)
            b = b_ref[:, sl]
            b_last = b_ref[rows - 1:rows, sl]
            q = q_ref[:, sl]
            k = k_ref[:, sl]
            v = v_ref[:, sl]
            rel = b - rows_of([0.5 * (b_ref[first_row(j), sl] + b_ref[last_row(j), sl]) for j in range(n_sub)], sub)
            att = jnp.where(level == 0,
                            _dot_nt((q * jnp.exp(rel)).astype(BF16), (k * jnp.exp(-rel)).astype(BF16)), 0.0)
            for l in range(1, n_levels + 1):
                m = sub << (l - 1)
                rel = b - rows_of([b_ref[p * 2 * m + m:p * 2 * m + m + 1, sl] for p in range(rows // (2 * m))], 2 * m)
                e = jnp.exp(jnp.minimum(jnp.where(later_half[l - 1], rel, -rel), 0.0))
                att = jnp.where(level == l, _dot_nt((q * e).astype(BF16), (k * e).astype(BF16)), att)
            q_st = (q * jnp.exp(b)).astype(BF16)
            k_st = (k * jnp.exp(b_last - b)).astype(BF16)
            st = st_ref[h]
            o = _dot(att.astype(BF16), v) + _dot_nt(q_st, st.astype(BF16))
            st_ref[h] = st * jnp.exp(b_last) + _dot_tn(v, k_st)
            y_ref[:, sl] = gated_out(o, sl)

    @pl.when(jnp.logical_not(safe))
    def _():
        grp_rows = min(SCAN_SLOW_ROWS, rows)
        grp_shift = grp_rows.bit_length() - 1
        pick = lax.broadcasted_iota(jnp.int32, (grp_rows, dk), 0)
        o_ref[...] = jnp.zeros_like(o_ref)

        def token(t, carry):
            r0 = pl.multiple_of(lax.shift_left(lax.shift_right_logical(t, grp_shift), grp_shift), grp_rows)
            sel = pick == t - r0
            grp = pl.ds(r0, grp_rows)
            for h in range(heads):
                sl = slice(h * dk, (h + 1) * dk)
                f = jnp.sum(jnp.where(sel, jnp.exp(lf_ref[grp, sl]), 0.0), axis=0, keepdims=True)
                q1 = jnp.where(sel, q_ref[grp, sl], 0.0).astype(BF16)
                k1 = jnp.where(sel, k_ref[grp, sl], 0.0).astype(BF16)
                v1 = jnp.where(sel, v_ref[grp, sl].astype(F32), 0.0).astype(BF16)
                st = st_ref[h] * f + _dot_tn(v1, k1)
                st_ref[h] = st
                o_ref[grp, sl] = jnp.where(sel, _dot_nt(q1, st.astype(BF16)), o_ref[grp, sl])
            return carry

        lax.fori_loop(0, rows, token, 0)
        for h in range(heads):
            sl = slice(h * dk, (h + 1) * dk)
            y_ref[:, sl] = gated_out(o_ref[:, sl], sl)

    @pl.when(c == pl.num_programs(1) - 1)
    def _():
        for h in range(heads):
            sfin_ref[h] = st_ref[h].T


def _hgrn_scan_call(q, k, v, lf, sg, g_norm, *, batch, heads, span_max=SCAN_SPAN_MAX):
    m, d = q.shape
    t = m // batch
    rows = min(SCAN_ROWS, t)
    nc = t // rows
    dk = d // heads
    spec = pl.BlockSpec((rows, d), lambda b, c: (b * nc + c, 0))
    return pl.pallas_call(
        functools.partial(_hgrn_scan_kernel, heads=heads, span_max=span_max),
        grid=(batch, nc),
        in_specs=[spec, spec, spec, spec, spec, pl.BlockSpec((1, d), lambda b, c: (0, 0))],
        out_specs=[spec, pl.BlockSpec((None, heads, dk, dk), lambda b, c: (b, 0, 0, 0))],
        out_shape=[jax.ShapeDtypeStruct((m, d), BF16), jax.ShapeDtypeStruct((batch, heads, dk, dk), F32)],
        scratch_shapes=[pltpu.VMEM((heads, dk, dk), F32), pltpu.VMEM((rows, d), F32), pltpu.VMEM((rows, d), F32)],
        compiler_params=_cparams("arbitrary", "arbitrary"),
        name="hgrn_scan",
    )(q, k, v, lf, sg, g_norm.reshape(1, d))


def _hgrn_step_kernel(q_ref, k_ref, v_ref, lf_ref, sg_ref, gn_ref, s0_ref, y_ref, s1_ref, o_ref):
    nb = q_ref.shape[0]
    q_t = q_ref[...].T
    k_t = k_ref[...].T
    f_t = jnp.exp(lf_ref[...]).T
    v = v_ref[...].astype(F32)
    for b in range(nb):
        s1 = s0_ref[b] * f_t[:, b:b + 1] + k_t[:, b:b + 1] * v[b:b + 1, :]
        s1_ref[b] = s1
        o_ref[b:b + 1, :] = jnp.sum(s1 * q_t[:, b:b + 1], axis=0, keepdims=True)
    y = _rmsnorm(o_ref[...], gn_ref[...]) * sg_ref[...]
    y_ref[...] = y.astype(BF16)


def _hgrn_step_call(q, k, v, lf, sg, g_norm, s0, *, heads):
    nb, d = q.shape
    dk = d // heads
    spec = pl.BlockSpec((nb, dk), lambda h: (0, h))
    sspec = pl.BlockSpec((nb, None, dk, dk), lambda h: (0, h, 0, 0))
    return pl.pallas_call(
        _hgrn_step_kernel,
        grid=(heads,),
        in_specs=[spec, spec, spec, spec, spec, pl.BlockSpec((1, dk), lambda h: (0, h)), sspec],
        out_specs=[spec, sspec],
        out_shape=[jax.ShapeDtypeStruct((nb, d), BF16), jax.ShapeDtypeStruct(s0.shape, F32)],
        scratch_shapes=[pltpu.VMEM((nb, dk), F32)],
        compiler_params=_cparams("arbitrary"),
        name="hgrn_step",
    )(q, k, v, lf, sg, g_norm.reshape(1, d), s0)


def _proj_kernel(y_ref, x_ref, gt_ref, ys_ref, xs_ref, gts_ref, w_ref, *rest, has_bias):
    b_ref = rest[0] if has_bias else None
    o_ref, os_ref = rest[-2:]
    w = w_ref[...].astype(BF16)

    def rows(y_ref, x_ref, gt_ref, o_ref):
        out = _dot(y_ref[...], w)
        if has_bias:
            out = out + b_ref[...]
        o_ref[...] = x_ref[...] + gt_ref[...] * out

    rows(y_ref, x_ref, gt_ref, o_ref)

    @pl.when(pl.program_id(1) == 0)
    def _():
        rows(ys_ref, xs_ref, gts_ref, os_ref)


def _proj_call(y, ys, w, bias, x, xs, mod, mod_s, *, layer, j):
    m, kdim = y.shape
    ms = ys.shape[0]
    d = x.shape[1]
    tm, tn = ROW_TILE, OUT_TILE
    gate = [3 * 1 + 2]
    has_bias = bias is not None
    in_specs = (
        [pl.BlockSpec((tm, kdim), lambda n, i: (i, 0)), pl.BlockSpec((tm, tn), lambda n, i: (i, n))]
        + _mod_specs(mod, layer, gate, tn, lambda n, i: (i, n), m // tm)
        + [_once((ms, kdim), lambda n, i: (0, 0)), pl.BlockSpec((ms, tn), lambda n, i: (0, n))]
        + _mod_specs(mod_s, layer, gate, tn, lambda n, i: (0, n), 1)
        + [pl.BlockSpec((None, kdim, tn), lambda n, i: (j, 0, n))]
    )
    args = [y, x, mod, ys, xs, mod_s, w]
    if has_bias:
        in_specs.append(pl.BlockSpec((None, 1, tn), lambda n, i: (j, 0, n)))
        args.append(bias.reshape(bias.shape[0], 1, d))
    return pl.pallas_call(
        functools.partial(_proj_kernel, has_bias=has_bias),
        grid=(d // tn, m // tm),
        in_specs=in_specs,
        out_specs=[pl.BlockSpec((tm, tn), lambda n, i: (i, n)), pl.BlockSpec((ms, tn), lambda n, i: (0, n))],
        out_shape=[jax.ShapeDtypeStruct((m, d), F32), jax.ShapeDtypeStruct((ms, d), F32)],
        compiler_params=_cparams("arbitrary", "arbitrary"),
        name="proj_residual",
    )(*args)


def _pw1_kernel(x_ref, sh_ref, sc_ref, xs_ref, shs_ref, scs_ref, g_ref, wa_ref, wg_ref, ba_ref, bg_ref,
                u_ref, us_ref, h_ref, hs_ref):
    wa = wa_ref[...].astype(BF16)
    wg = wg_ref[...].astype(BF16)

    def rows(x_ref, sh_ref, sc_ref, u_ref, h_ref):
        @pl.when(pl.program_id(1) == 0)
        def _():
            _modulate_into(h_ref, x_ref, g_ref, sh_ref, sc_ref)

        h = h_ref[...]
        u_ref[...] = (_dot(h, wa) + ba_ref[...]) * _sigmoid(_dot(h, wg) + bg_ref[...])

    rows(x_ref, sh_ref, sc_ref, u_ref, h_ref)

    @pl.when(pl.program_id(0) == 0)
    def _():
        rows(xs_ref, shs_ref, scs_ref, us_ref, hs_ref)


def _pw1_call(x, xs, mod, mod_s, norm_g, w, bias, *, layer, j):
    m, d = x.shape
    ms = xs.shape[0]
    tm, tn = ROW_TILE, IN_TILE
    ch = w.shape[2] // 2
    nn = ch // tn
    sub = 1
    secs = [3 * sub, 3 * sub + 1]

    def wspec(sec):
        return pl.BlockSpec((None, d, tn), lambda i, n: (j, 0, sec * nn + n))

    def bspec(sec):
        return pl.BlockSpec((None, 1, tn), lambda i, n: (j, 0, sec * nn + n))

    bias3 = bias.reshape(bias.shape[0], 1, 2 * ch)
    return pl.pallas_call(
        _pw1_kernel,
        grid=(m // tm, nn),
        in_specs=(
            [pl.BlockSpec((tm, d), lambda i, n: (i, 0))]
            + _mod_specs(mod, layer, secs, d, lambda i, n: (i, 0), m // tm)
            + [_once((ms, d), lambda i, n: (0, 0))]
            + _mod_specs(mod_s, layer, secs, d, lambda i, n: (0, 0), 1)
            + [
                pl.BlockSpec((None, None, 1, d), lambda i, n: (layer, sub, 0, 0)),
                wspec(0), wspec(1), bspec(0), bspec(1),
            ]
        ),
        out_specs=[pl.BlockSpec((tm, tn), lambda i, n: (i, n)), pl.BlockSpec((ms, tn), _side_col(nn))],
        out_shape=[jax.ShapeDtypeStruct((m, ch), F32), jax.ShapeDtypeStruct((ms, ch), F32)],
        scratch_shapes=[pltpu.VMEM((tm, d), BF16), pltpu.VMEM((ms, d), BF16)],
        compiler_params=_cparams("arbitrary", "arbitrary"),
        name="conv_pw1",
    )(x, mod, mod, xs, mod_s, mod_s, norm_g, w, w, bias3, bias3)


def _ln_silu_into(y_ref, acc_ref, lng_ref, lnb_ref):
    def fn(rows):
        a = acc_ref[rows, :]
        mu = jnp.mean(a, axis=-1, keepdims=True)
        xc = a - mu
        yn = xc * lax.rsqrt(jnp.mean(xc * xc, axis=-1, keepdims=True) + EPS) * lng_ref[...] + lnb_ref[...]
        y_ref[rows, :] = _silu(yn).astype(BF16)

    _row_loop(acc_ref.shape[0], fn)


def _conv_seq_kernel(u_ref, w_ref, bdw_ref, lng_ref, lnb_ref, y_ref, tail_ref, win_ref, acc_ref, *, width):
    t = pl.program_id(1)
    tt, ch = u_ref.shape
    n_lt = ch // LANES
    off = CONV_PAD - (width - 1)
    blk = SUBLANES * SUBLANES

    @pl.when(t == 0)
    def _():
        win_ref[:, 0:CONV_PAD, :] = jnp.zeros((n_lt, CONV_PAD, LANES), F32)

    for l in range(n_lt):
        win_ref[l, CONV_PAD:CONV_PAD + tt, :] = u_ref[:, l * LANES:(l + 1) * LANES]

    def body(i, carry):
        l = i // (tt // blk)
        r0 = pl.multiple_of((i % (tt // blk)) * blk, blk)
        strided = [win_ref[l, pl.ds(r0 + r, SUBLANES, stride=SUBLANES), :]
                   for r in range(off, off + width - 1 + SUBLANES)]
        accs = [[jnp.broadcast_to(bdw_ref[l], (SUBLANES, LANES)), None] for _ in range(SUBLANES)]
        for tap in range(width):
            w_tap = jnp.broadcast_to(w_ref[l, tap:tap + 1, :], (SUBLANES, LANES))
            for t0 in range(SUBLANES):
                term = strided[t0 + tap] * w_tap
                prev = accs[t0][tap % 2]
                accs[t0][tap % 2] = term if prev is None else prev + term
        for t0 in range(SUBLANES):
            acc_ref[l, pl.ds(r0 + t0, SUBLANES, stride=SUBLANES), :] = accs[t0][0] + accs[t0][1]
        return carry

    lax.fori_loop(0, n_lt * (tt // blk), body, 0, unroll=2)

    for l in range(n_lt):
        tail_ref[:, l * LANES:(l + 1) * LANES] = win_ref[l, CONV_PAD + tt - (width - 1):CONV_PAD + tt, :]
    win_ref[:, 0:CONV_PAD, :] = win_ref[:, tt:tt + CONV_PAD, :]

    def fn(rows):
        a = jnp.concatenate([acc_ref[l, rows, :] for l in range(n_lt)], axis=1)
        mu = jnp.mean(a, axis=-1, keepdims=True)
        xc = a - mu
        yn = xc * lax.rsqrt(jnp.mean(xc * xc, axis=-1, keepdims=True) + EPS) * lng_ref[...] + lnb_ref[...]
        y_ref[rows, :] = _silu(yn).astype(BF16)

    _row_loop(tt, fn)


def _conv_seq_call(u, w_dw, b_dw, ln_g, ln_b, *, j, batch):
    m, ch = u.shape
    t = m // batch
    tt = min(CONV_TILE, t)
    nt = t // tt
    width = w_dw.shape[1]
    n_lt = ch // LANES
    vec = lambda a: a.reshape(a.shape[0], 1, ch)
    vspec = pl.BlockSpec((None, 1, ch), lambda b, i: (j, 0, 0))
    w_lt = jnp.transpose(w_dw[j].reshape(width, n_lt, LANES), (1, 0, 2))
    b_lt = b_dw[j].reshape(n_lt, 1, LANES)
    return pl.pallas_call(
        functools.partial(_conv_seq_kernel, width=width),
        grid=(batch, nt),
        in_specs=[
            pl.BlockSpec((tt, ch), lambda b, i: (b * nt + i, 0)),
            pl.BlockSpec((n_lt, width, LANES), lambda b, i: (0, 0, 0)),
            pl.BlockSpec((n_lt, 1, LANES), lambda b, i: (0, 0, 0)),
            vspec, vspec,
        ],
        out_specs=[
            pl.BlockSpec((tt, ch), lambda b, i: (b * nt + i, 0)),
            pl.BlockSpec((None, width - 1, ch), lambda b, i: (b, 0, 0)),
        ],
        out_shape=[jax.ShapeDtypeStruct((m, ch), BF16), jax.ShapeDtypeStruct((batch, width - 1, ch), F32)],
        scratch_shapes=[pltpu.VMEM((n_lt, CONV_PAD + tt, LANES), F32), pltpu.VMEM((n_lt, tt, LANES), F32)],
        compiler_params=_cparams("arbitrary", "arbitrary"),
        name="conv_seq",
    )(u, w_lt, b_lt, vec(ln_g), vec(ln_b))


def _conv_step_kernel(u_ref, buf_ref, w_ref, bdw_ref, lng_ref, lnb_ref, y_ref, new_ref, acc_ref, *, width):
    u = u_ref[...]
    acc = u * w_ref[width - 1:width, :] + bdw_ref[...]
    for tap in range(width - 1):
        acc = acc + buf_ref[tap] * w_ref[tap:tap + 1, :]
    acc_ref[...] = acc
    for tap in range(width - 2):
        new_ref[tap] = buf_ref[tap + 1]
    new_ref[width - 2] = u
    _ln_silu_into(y_ref, acc_ref, lng_ref, lnb_ref)


def _conv_step_call(u, buf, w_dw, b_dw, ln_g, ln_b, *, j, tb):
    nb, ch = u.shape
    width = w_dw.shape[1]
    vec = lambda a: a.reshape(a.shape[0], 1, ch)
    vspec = pl.BlockSpec((None, 1, ch), lambda i: (j, 0, 0))
    return pl.pallas_call(
        functools.partial(_conv_step_kernel, width=width),
        grid=(nb // tb,),
        in_specs=[
            pl.BlockSpec((tb, ch), lambda i: (i, 0)),
            pl.BlockSpec((None, width - 1, tb, ch), lambda i: (j, 0, i, 0)),
            pl.BlockSpec((None, width, ch), lambda i: (j, 0, 0)),
            vspec, vspec, vspec,
        ],
        out_specs=[
            pl.BlockSpec((tb, ch), lambda i: (i, 0)),
            pl.BlockSpec((width - 1, tb, ch), lambda i: (0, i, 0)),
        ],
        out_shape=[jax.ShapeDtypeStruct((nb, ch), BF16), jax.ShapeDtypeStruct((width - 1, nb, ch), F32)],
        scratch_shapes=[pltpu.VMEM((tb, ch), F32)],
        compiler_params=_cparams("arbitrary"),
        name="conv_step",
    )(u, buf, w_dw, vec(b_dw), vec(ln_g), vec(ln_b))


def kernel(x_prompt, x_sample, state_hgrn, state_conv, c_prompt, c_sample, hgrn_lb, hgrn_w_in, hgrn_b_f, hgrn_g_norm, hgrn_w_out, conv_w_pw1, conv_b_pw1, conv_w_dw, conv_b_dw, conv_ln_g, conv_ln_b, conv_w_pw2, conv_b_pw2, ada_w, ada_b, norm_g, ffn_w_in, ffn_w_out, final_g):
    bp, t, d = x_prompt.shape
    bs = x_sample.shape[0]
    depth = ada_w.shape[0]
    heads = d // LANES
    time_major = lambda s: jnp.transpose(s, (0, 2, 1, 3))

    mod_s, mod_p = _ada_call(c_sample, c_prompt, ada_w, ada_b)
    mod_s = mod_s.reshape(depth, 1, bs, N_SUB * 3 * d)
    mod_p = mod_p.reshape(depth, bp, 1, N_SUB * 3 * d)
    norm_g4 = norm_g.reshape(depth, N_SUB, 1, d)
    conv_buf = time_major(state_conv)

    x, xs = x_prompt.reshape(bp * t, d), x_sample.reshape(bs, d)
    hgrn_p, hgrn_s, conv_p, conv_s = [], [], [], []
    for layer in range(depth):
        x, xs = _ffn_call(x, xs, mod_p, mod_s, norm_g4, ffn_w_in, ffn_w_out, final_g,
                          layer=layer, sub=0, j=0, final=False)
        j = layer // 2
        if layer % 2 == 0:
            outs = _hgrn_in_call(x, xs, mod_p, mod_s, norm_g4, hgrn_w_in, hgrn_b_f, hgrn_lb,
                                 layer=layer, j=j, lb_row=layer)
            y, s = _hgrn_scan_call(*outs[:5], hgrn_g_norm[j], batch=bp, heads=heads)
            ys, ss = _hgrn_step_call(*outs[5:], hgrn_g_norm[j], state_hgrn[j], heads=heads)
            hgrn_p.append(s)
            hgrn_s.append(ss)
            x, xs = _proj_call(y, ys, hgrn_w_out, None, x, xs, mod_p, mod_s, layer=layer, j=j)
        else:
            u, us = _pw1_call(x, xs, mod_p, mod_s, norm_g4, conv_w_pw1, conv_b_pw1, layer=layer, j=j)
            y, s = _conv_seq_call(u, conv_w_dw, conv_b_dw, conv_ln_g, conv_ln_b, j=j, batch=bp)
            ys, ss = _conv_step_call(us, conv_buf, conv_w_dw, conv_b_dw, conv_ln_g, conv_ln_b, j=j, tb=4 * SUBLANES)
            conv_p.append(jnp.transpose(s, (1, 0, 2)))
            conv_s.append(ss)
            x, xs = _proj_call(y, ys, conv_w_pw2, conv_b_pw2, x, xs, mod_p, mod_s, layer=layer, j=j)
        x, xs = _ffn_call(x, xs, mod_p, mod_s, norm_g4, ffn_w_in, ffn_w_out, final_g,
                          layer=layer, sub=2, j=1, final=(layer == depth - 1))
    return (x.reshape(bp, t, d), xs.reshape(bs, 1, d), jnp.stack(hgrn_p), jnp.stack(hgrn_s),
            time_major(jnp.stack(conv_p)), time_major(jnp.stack(conv_s)))
```

```python
import functools

import jax
import jax.numpy as jnp
from jax import lax
from jax.experimental import pallas as pl
from jax.experimental.pallas import tpu as pltpu

F32 = jnp.float32
BF16 = jnp.bfloat16
EPS = 1e-6
N_SUB = 3
LANES = 128
SUBLANES = 8
VMEM_LIMIT_BYTES = 60 * 1024 * 1024

ROW_TILE = 1024
FFN_TILE = 256
IN_TILE = 256
PW1_TILE = 512
OUT_TILE = 2048
OUT_ROW_TILE = 512
SCAN_ROWS = 256
SCAN_SUB = 64
SCAN_SPAN_MAX = 120.0
SCAN_SLOW_ROWS = 16
CONV_PAD = 32
CONV_TILE = 256
NORM_ROWS = 16
NORM_UNROLL = 4


def _cparams(*sem):
    return pltpu.CompilerParams(dimension_semantics=sem, vmem_limit_bytes=VMEM_LIMIT_BYTES)


def _dot(a, b):
    return jnp.dot(a, b, preferred_element_type=F32)


def _dot_nt(a, b):
    return lax.dot_general(a, b, (((1,), (1,)), ((), ())), preferred_element_type=F32)


def _dot_tn(a, b):
    return lax.dot_general(a, b, (((0,), (0,)), ((), ())), preferred_element_type=F32)


def _sigmoid(x):
    return 1.0 / (1.0 + jnp.exp(-x))


def _silu(x):
    return x * _sigmoid(x)


def _rmsnorm(x, g):
    return x * lax.rsqrt(jnp.mean(x * x, axis=-1, keepdims=True) + EPS) * g


def _row_loop(n_rows, fn, in_place=False):
    rows = min(NORM_ROWS * NORM_UNROLL if in_place else NORM_ROWS, n_rows)
    steps = n_rows // rows
    if steps == 1:
        fn(pl.ds(0, rows))
        return

    def body(i, carry):
        fn(pl.ds(pl.multiple_of(i * rows, rows), rows))
        return carry

    lax.fori_loop(0, steps, body, 0, unroll=NORM_UNROLL if not in_place and steps % NORM_UNROLL == 0 else 1)


def _mod_rows(ref, rows):
    return ref[...] if ref.shape[0] == 1 else ref[rows, :]


def _modulate_into(h_ref, x_ref, g_ref, sh_ref, sc_ref):
    per_tile = sc_ref.shape[0] == 1
    gain_tile = g_ref[...] * (1.0 + sc_ref[...]) if per_tile else None

    def fn(rows):
        x = x_ref[rows, :]
        gain = gain_tile if per_tile else g_ref[...] * (1.0 + sc_ref[rows, :])
        y = x * lax.rsqrt(jnp.mean(x * x, axis=-1, keepdims=True) + EPS) * gain + _mod_rows(sh_ref, rows)
        h_ref[rows, :] = y.astype(BF16)

    _row_loop(x_ref.shape[0], fn)


def _once(shape, index_map):
    return pl.BlockSpec(shape, index_map, pipeline_mode=pl.Buffered(1))


def _mod_specs(mod, layer, secs, width, col_of, row_tiles):
    groups, r = mod.shape[1], mod.shape[2]
    tiles_per_group = max(row_tiles // groups, 1)
    ncol = mod.shape[3] // (N_SUB * 3) // width

    def spec(sec):
        def index_map(*idx):
            i, n = col_of(*idx)
            return (layer, i // tiles_per_group, 0, sec * ncol + n)
        if groups == 1 and ncol == 1:
            return _once((None, None, r, width), index_map)
        return pl.BlockSpec((None, None, r, width), index_map)

    return [spec(s) for s in secs]


def _ada_kernel(cs_ref, cp_ref, w_ref, b_ref, os_ref, op_ref):
    w = w_ref[...].astype(BF16)
    os_ref[...] = _dot(_silu(cs_ref[...]).astype(BF16), w) + b_ref[...]
    op_ref[...] = _dot(_silu(cp_ref[...]).astype(BF16), w) + b_ref[...]


def _ada_call(c_sample, c_prompt, ada_w, ada_b, tn=1024):
    depth, d, n = ada_w.shape
    rs, rp = c_sample.shape[0], c_prompt.shape[0]
    return pl.pallas_call(
        _ada_kernel,
        grid=(depth, n // tn),
        in_specs=[
            pl.BlockSpec((rs, d), lambda l, j: (0, 0)),
            pl.BlockSpec((rp, d), lambda l, j: (0, 0)),
            pl.BlockSpec((None, d, tn), lambda l, j: (l, 0, j)),
            pl.BlockSpec((None, 1, tn), lambda l, j: (l, 0, j)),
        ],
        out_specs=[
            pl.BlockSpec((None, rs, tn), lambda l, j: (l, 0, j)),
            pl.BlockSpec((None, rp, tn), lambda l, j: (l, 0, j)),
        ],
        out_shape=[jax.ShapeDtypeStruct((depth, rs, n), F32), jax.ShapeDtypeStruct((depth, rp, n), F32)],
        compiler_params=_cparams("arbitrary", "arbitrary"),
        name="adaln_params",
    )(c_sample, c_prompt, ada_w, ada_b.reshape(depth, 1, n))


def _ffn_rows(x_ref, sh_ref, sc_ref, gt_ref, g_ref, fg_ref, o_ref, h_ref, wa_ref, wu_ref, wo_ref):
    f = pl.program_id(1)

    @pl.when(f == 0)
    def _():
        _modulate_into(h_ref, x_ref, g_ref, sh_ref, sc_ref)
        o_ref[...] = jnp.zeros_like(o_ref)

    h = h_ref[...]
    act = (_silu(_dot(h, wa_ref[...].astype(BF16))) * _dot(h, wu_ref[...].astype(BF16))).astype(BF16)
    o_ref[...] += _dot(act, wo_ref[...].astype(BF16))

    @pl.when(f == pl.num_programs(1) - 1)
    def _():
        def fn(rows):
            y = x_ref[rows, :] + 0.5 * _mod_rows(gt_ref, rows) * o_ref[rows, :]
            if fg_ref is not None:
                y = _rmsnorm(y, fg_ref[...])
            o_ref[rows, :] = y

        _row_loop(x_ref.shape[0], fn, in_place=True)


def _ffn_kernel(x_ref, sh_ref, sc_ref, gt_ref, xs_ref, shs_ref, scs_ref, gts_ref, g_ref, wa_ref, wu_ref, wo_ref,
                *rest, final):
    fg_ref = rest[0] if final else None
    o_ref, os_ref, h_ref, hs_ref = rest[-4:]
    _ffn_rows(x_ref, sh_ref, sc_ref, gt_ref, g_ref, fg_ref, o_ref, h_ref, wa_ref, wu_ref, wo_ref)

    @pl.when(pl.program_id(0) == 0)
    def _():
        _ffn_rows(xs_ref, shs_ref, scs_ref, gts_ref, g_ref, fg_ref, os_ref, hs_ref, wa_ref, wu_ref, wo_ref)


def _ffn_call(x, xs, mod, mod_s, norm_g, w_in, w_out, final_g, *, layer, sub, j, final):
    m, d = x.shape
    ms = xs.shape[0]
    tm, tf = ROW_TILE, FFN_TILE
    nf = w_out.shape[2] // tf
    secs = [3 * sub, 3 * sub + 1, 3 * sub + 2]
    in_specs = (
        [pl.BlockSpec((tm, d), lambda i, f: (i, 0))]
        + _mod_specs(mod, layer, secs, d, lambda i, f: (i, 0), m // tm)
        + [_once((ms, d), lambda i, f: (0, 0))]
        + _mod_specs(mod_s, layer, secs, d, lambda i, f: (0, 0), 1)
        + [
            pl.BlockSpec((None, None, 1, d), lambda i, f: (layer, sub, 0, 0)),
            pl.BlockSpec((None, None, d, tf), lambda i, f: (layer, j, 0, f)),
            pl.BlockSpec((None, None, d, tf), lambda i, f: (layer, j, 0, nf + f)),
            pl.BlockSpec((None, None, tf, d), lambda i, f: (layer, j, f, 0)),
        ]
    )
    args = [x, mod, mod, mod, xs, mod_s, mod_s, mod_s, norm_g, w_in, w_in, w_out]
    if final:
        in_specs.append(pl.BlockSpec((1, d), lambda i, f: (0, 0)))
        args.append(final_g.reshape(1, d))
    return pl.pallas_call(
        functools.partial(_ffn_kernel, final=final),
        grid=(m // tm, nf),
        in_specs=in_specs,
        out_specs=[pl.BlockSpec((tm, d), lambda i, f: (i, 0)), _once((ms, d), lambda i, f: (0, 0))],
        out_shape=[jax.ShapeDtypeStruct((m, d), F32), jax.ShapeDtypeStruct((ms, d), F32)],
        scratch_shapes=[pltpu.VMEM((tm, d), BF16), pltpu.VMEM((ms, d), BF16)],
        compiler_params=_cparams("arbitrary", "arbitrary"),
        name="ffn",
    )(*args)


def _hgrn_in_rows(x_ref, sh_ref, sc_ref, g_ref, bf_ref, h_ref, outs, ws, lb):
    q_ref, k_ref, v_ref, lf_ref, sg_ref = outs
    wq_ref, wf_ref, wi_ref, wg_ref = ws

    @pl.when(pl.program_id(1) == 0)
    def _():
        _modulate_into(h_ref, x_ref, g_ref, sh_ref, sc_ref)

    h = h_ref[...]
    q_ref[...] = _dot(h, wq_ref[...].astype(BF16))
    z = _dot(h, wf_ref[...].astype(BF16)) + bf_ref[...]
    lf_ref[...] = jnp.log(lb + (1.0 - lb) * _sigmoid(z))
    k_ref[...] = (1.0 - lb) * _sigmoid(-z)
    v_ref[...] = _dot(h, wi_ref[...].astype(BF16)).astype(BF16)
    sg_ref[...] = _silu(_dot(h, wg_ref[...].astype(BF16)))


def _hgrn_in_kernel(x_ref, sh_ref, sc_ref, xs_ref, shs_ref, scs_ref, g_ref, wq_ref, wf_ref, wi_ref, wg_ref,
                    bf_ref, lb_ref, *rest, lb_row):
    outs, outs_s, (h_ref, hs_ref) = rest[0:5], rest[5:10], rest[10:12]
    raw = [lb_ref[i:i + 1, :] for i in range(lb_ref.shape[0])]
    top = functools.reduce(jnp.maximum, raw)
    e = [jnp.exp(r - top) for r in raw]
    lb = sum(e[:lb_row + 1]) / sum(e)
    ws = (wq_ref, wf_ref, wi_ref, wg_ref)
    _hgrn_in_rows(x_ref, sh_ref, sc_ref, g_ref, bf_ref, h_ref, outs, ws, lb)

    @pl.when(pl.program_id(0) == 0)
    def _():
        _hgrn_in_rows(xs_ref, shs_ref, scs_ref, g_ref, bf_ref, hs_ref, outs_s, ws, lb)


def _side_col(nn):
    return lambda i, n: (0, jnp.where(i == 0, n, nn - 1))


def _hgrn_in_call(x, xs, mod, mod_s, norm_g, w_in, b_f, lb_raw, *, layer, j, lb_row):
    m, d = x.shape
    ms = xs.shape[0]
    tm, tn = ROW_TILE, IN_TILE
    nn = d // tn
    sub = 1
    secs = [3 * sub, 3 * sub + 1]

    def wspec(sec):
        return pl.BlockSpec((None, d, tn), lambda i, n: (j, 0, sec * nn + n))

    def out_shapes(rows):
        return [jax.ShapeDtypeStruct((rows, d), dt) for dt in (F32, F32, BF16, F32, F32)]

    return pl.pallas_call(
        functools.partial(_hgrn_in_kernel, lb_row=lb_row),
        grid=(m // tm, nn),
        in_specs=(
            [pl.BlockSpec((tm, d), lambda i, n: (i, 0))]
            + _mod_specs(mod, layer, secs, d, lambda i, n: (i, 0), m // tm)
            + [_once((ms, d), lambda i, n: (0, 0))]
            + _mod_specs(mod_s, layer, secs, d, lambda i, n: (0, 0), 1)
            + [
                pl.BlockSpec((None, None, 1, d), lambda i, n: (layer, sub, 0, 0)),
                wspec(0), wspec(1), wspec(2), wspec(3),
                pl.BlockSpec((None, 1, tn), lambda i, n: (j, 0, n)),
                pl.BlockSpec((lb_raw.shape[0], tn), lambda i, n: (0, n)),
            ]
        ),
        out_specs=[pl.BlockSpec((tm, tn), lambda i, n: (i, n))] * 5 + [pl.BlockSpec((ms, tn), _side_col(nn))] * 5,
        out_shape=out_shapes(m) + out_shapes(ms),
        scratch_shapes=[pltpu.VMEM((tm, d), BF16), pltpu.VMEM((ms, d), BF16)],
        compiler_params=_cparams("arbitrary", "arbitrary"),
        name="hgrn_in",
    )(x, mod, mod, xs, mod_s, mod_s, norm_g, w_in, w_in, w_in, w_in, b_f.reshape(b_f.shape[0], 1, d), lb_raw)


def _hgrn_scan_kernel(q_ref, k_ref, v_ref, lf_ref, sg_ref, gn_ref, y_ref, sfin_ref, st_ref, b_ref, o_ref, *,
                      heads, span_max):
    c = pl.program_id(1)
    rows, d = q_ref.shape
    dk = d // heads
    sub = min(SCAN_SUB, rows)
    n_sub = rows // sub
    n_levels = n_sub.bit_length() - 1
    sub_shift = sub.bit_length() - 1

    @pl.when(c == 0)
    def _():
        st_ref[...] = jnp.zeros_like(st_ref)

    row = lax.broadcasted_iota(jnp.int32, (rows, rows), 0)
    col = lax.broadcasted_iota(jnp.int32, (rows, rows), 1)
    tri = jnp.where(col <= row, 1.0, 0.0).astype(BF16)
    lf = lf_ref[...]
    p1 = lf.astype(BF16)
    r1 = lf - p1.astype(F32)
    p2 = r1.astype(BF16)
    p3 = (r1 - p2.astype(F32)).astype(BF16)
    b_ref[...] = _dot(tri, p1) + _dot(tri, p2) + _dot(tri, p3)

    def first_row(j):
        return slice(j * sub, j * sub + 1)

    def last_row(j):
        return slice((j + 1) * sub - 1, (j + 1) * sub)

    spans = [b_ref[first_row(j), :] - b_ref[last_row(j), :] for j in range(n_sub)]
    safe = jnp.max(functools.reduce(jnp.maximum, spans)) <= span_max

    def gated_out(o, sl):
        return (_rmsnorm(o, gn_ref[:, sl]) * sg_ref[:, sl]).astype(BF16)

    @pl.when(safe)
    def _():
        srl = lax.shift_right_logical
        level = jnp.where((srl(row, sub_shift) == srl(col, sub_shift)) & (col <= row), 0, -1)
        later_half = []
        row_d = lax.broadcasted_iota(jnp.int32, (rows, dk), 0)
        for l in range(1, n_levels + 1):
            s = sub_shift + l - 1
            hit = (srl(row, s + 1) == srl(col, s + 1)) & ((srl(row, s) & 1) == 1) & ((srl(col, s) & 1) == 0)
            level = jnp.where(hit, l, level)
            later_half.append((srl(row_d, s) & 1) == 1)

        def rows_of(pieces, n):
            return jnp.concatenate([jnp.broadcast_to(p, (n, dk)) for p in pieces], axis=0)

        for h in range(heads):
            sl = slice(h * dk, (h + 1) * dk)
            b = b_ref[:, sl]
            b_last = b_ref[rows - 1:rows, sl]
            q = q_ref[:, sl]
            k = k_ref[:, sl]
            v = v_ref[:, sl]
            rel = b - rows_of([0.5 * (b_ref[first_row(j), sl] + b_ref[last_row(j), sl]) for j in range(n_sub)], sub)
            att = jnp.where(level == 0,
                            _dot_nt((q * jnp.exp(rel)).astype(BF16), (k * jnp.exp(-rel)).astype(BF16)), 0.0)
            for l in range(1, n_levels + 1):
                m = sub << (l - 1)
                rel = b - rows_of([b_ref[p * 2 * m + m:p * 2 * m + m + 1, sl] for p in range(rows // (2 * m))], 2 * m)
                e = jnp.exp(jnp.minimum(jnp.where(later_half[l - 1], rel, -rel), 0.0))
                att = jnp.where(level == l, _dot_nt((q * e).astype(BF16), (k * e).astype(BF16)), att)
            q_st = (q * jnp.exp(b)).astype(BF16)
            k_st = (k * jnp.exp(b_last - b)).astype(BF16)
            st = st_ref[h]
            o = _dot(att.astype(BF16), v) + _dot_nt(q_st, st.astype(BF16))
            st_ref[h] = st * jnp.exp(b_last) + _dot_tn(v, k_st)
            y_ref[:, sl] = gated_out(o, sl)

    @pl.when(jnp.logical_not(safe))
    def _():
        grp_rows = min(SCAN_SLOW_ROWS, rows)
        grp_shift = grp_rows.bit_length() - 1
        pick = lax.broadcasted_iota(jnp.int32, (grp_rows, dk), 0)
        o_ref[...] = jnp.zeros_like(o_ref)

        def token(t, carry):
            r0 = pl.multiple_of(lax.shift_left(lax.shift_right_logical(t, grp_shift), grp_shift), grp_rows)
            sel = pick == t - r0
            grp = pl.ds(r0, grp_rows)
            for h in range(heads):
                sl = slice(h * dk, (h + 1) * dk)
                f = jnp.sum(jnp.where(sel, jnp.exp(lf_ref[grp, sl]), 0.0), axis=0, keepdims=True)
                q1 = jnp.where(sel, q_ref[grp, sl], 0.0).astype(BF16)
                k1 = jnp.where(sel, k_ref[grp, sl], 0.0).astype(BF16)
                v1 = jnp.where(sel, v_ref[grp, sl].astype(F32), 0.0).astype(BF16)
                st = st_ref[h] * f + _dot_tn(v1, k1)
                st_ref[h] = st
                o_ref[grp, sl] = jnp.where(sel, _dot_nt(q1, st.astype(BF16)), o_ref[grp, sl])
            return carry

        lax.fori_loop(0, rows, token, 0)
        for h in range(heads):
            sl = slice(h * dk, (h + 1) * dk)
            y_ref[:, sl] = gated_out(o_ref[:, sl], sl)

    @pl.when(c == pl.num_programs(1) - 1)
    def _():
        for h in range(heads):
            sfin_ref[h] = st_ref[h].T


def _hgrn_scan_call(q, k, v, lf, sg, g_norm, *, batch, heads, span_max=SCAN_SPAN_MAX):
    m, d = q.shape
    t = m // batch
    rows = min(SCAN_ROWS, t)
    nc = t // rows
    dk = d // heads
    spec = pl.BlockSpec((rows, d), lambda b, c: (b * nc + c, 0))
    return pl.pallas_call(
        functools.partial(_hgrn_scan_kernel, heads=heads, span_max=span_max),
        grid=(batch, nc),
        in_specs=[spec, spec, spec, spec, spec, pl.BlockSpec((1, d), lambda b, c: (0, 0))],
        out_specs=[spec, pl.BlockSpec((None, heads, dk, dk), lambda b, c: (b, 0, 0, 0))],
        out_shape=[jax.ShapeDtypeStruct((m, d), BF16), jax.ShapeDtypeStruct((batch, heads, dk, dk), F32)],
        scratch_shapes=[pltpu.VMEM((heads, dk, dk), F32), pltpu.VMEM((rows, d), F32), pltpu.VMEM((rows, d), F32)],
        compiler_params=_cparams("arbitrary", "arbitrary"),
        name="hgrn_scan",
    )(q, k, v, lf, sg, g_norm.reshape(1, d))


def _hgrn_step_kernel(q_ref, k_ref, v_ref, lf_ref, sg_ref, gn_ref, s0_ref, y_ref, s1_ref, o_ref):
    nb = q_ref.shape[0]
    q_t = q_ref[...].T
    k_t = k_ref[...].T
    f_t = jnp.exp(lf_ref[...]).T
    v = v_ref[...].astype(F32)
    for b in range(nb):
        s1 = s0_ref[b] * f_t[:, b:b + 1] + k_t[:, b:b + 1] * v[b:b + 1, :]
        s1_ref[b] = s1
        o_ref[b:b + 1, :] = jnp.sum(s1 * q_t[:, b:b + 1], axis=0, keepdims=True)
    y = _rmsnorm(o_ref[...], gn_ref[...]) * sg_ref[...]
    y_ref[...] = y.astype(BF16)


def _hgrn_step_call(q, k, v, lf, sg, g_norm, s0, *, heads):
    nb, d = q.shape
    dk = d // heads
    spec = pl.BlockSpec((nb, dk), lambda h: (0, h))
    sspec = pl.BlockSpec((nb, None, dk, dk), lambda h: (0, h, 0, 0))
    return pl.pallas_call(
        _hgrn_step_kernel,
        grid=(heads,),
        in_specs=[spec, spec, spec, spec, spec, pl.BlockSpec((1, dk), lambda h: (0, h)), sspec],
        out_specs=[spec, sspec],
        out_shape=[jax.ShapeDtypeStruct((nb, d), BF16), jax.ShapeDtypeStruct(s0.shape, F32)],
        scratch_shapes=[pltpu.VMEM((nb, dk), F32)],
        compiler_params=_cparams("arbitrary"),
        name="hgrn_step",
    )(q, k, v, lf, sg, g_norm.reshape(1, d), s0)


def _proj_kernel(y_ref, x_ref, gt_ref, ys_ref, xs_ref, gts_ref, w_ref, *rest, has_bias):
    b_ref = rest[0] if has_bias else None
    o_ref, os_ref = rest[-2:]
    def rows(y_ref, x_ref, gt_ref, o_ref):
        out = _dot(y_ref[...], w_ref[...].astype(BF16))
        if has_bias:
            out = out + b_ref[...]
        o_ref[...] = x_ref[...] + gt_ref[...] * out

    rows(y_ref, x_ref, gt_ref, o_ref)

    @pl.when(pl.program_id(1) == 0)
    def _():
        rows(ys_ref, xs_ref, gts_ref, os_ref)


def _proj_call(y, ys, w, bias, x, xs, mod, mod_s, *, layer, j):
    m, kdim = y.shape
    ms = ys.shape[0]
    d = x.shape[1]
    tm, tn = OUT_ROW_TILE, min(OUT_TILE, d)
    gate = [3 * 1 + 2]
    has_bias = bias is not None
    wspec = _once if tn == d else pl.BlockSpec
    in_specs = (
        [pl.BlockSpec((tm, kdim), lambda n, i: (i, 0)), pl.BlockSpec((tm, tn), lambda n, i: (i, n))]
        + _mod_specs(mod, layer, gate, tn, lambda n, i: (i, n), m // tm)
        + [_once((ms, kdim), lambda n, i: (0, 0)), pl.BlockSpec((ms, tn), lambda n, i: (0, n))]
        + _mod_specs(mod_s, layer, gate, tn, lambda n, i: (0, n), 1)
        + [wspec((None, kdim, tn), lambda n, i: (j, 0, n))]
    )
    args = [y, x, mod, ys, xs, mod_s, w]
    if has_bias:
        in_specs.append(pl.BlockSpec((None, 1, tn), lambda n, i: (j, 0, n)))
        args.append(bias.reshape(bias.shape[0], 1, d))
    return pl.pallas_call(
        functools.partial(_proj_kernel, has_bias=has_bias),
        grid=(d // tn, m // tm),
        in_specs=in_specs,
        out_specs=[pl.BlockSpec((tm, tn), lambda n, i: (i, n)), pl.BlockSpec((ms, tn), lambda n, i: (0, n))],
        out_shape=[jax.ShapeDtypeStruct((m, d), F32), jax.ShapeDtypeStruct((ms, d), F32)],
        compiler_params=_cparams("arbitrary", "arbitrary"),
        name="proj_residual",
    )(*args)


def _pw1_kernel(x_ref, sh_ref, sc_ref, xs_ref, shs_ref, scs_ref, g_ref, wa_ref, wg_ref, ba_ref, bg_ref,
                u_ref, us_ref, h_ref, hs_ref):
    def rows(x_ref, sh_ref, sc_ref, u_ref, h_ref):
        @pl.when(pl.program_id(1) == 0)
        def _():
            _modulate_into(h_ref, x_ref, g_ref, sh_ref, sc_ref)

        h = h_ref[...]
        a = _dot(h, wa_ref[...].astype(BF16)) + ba_ref[...]
        u_ref[...] = a * _sigmoid(_dot(h, wg_ref[...].astype(BF16)) + bg_ref[...])

    rows(x_ref, sh_ref, sc_ref, u_ref, h_ref)

    @pl.when(pl.program_id(0) == 0)
    def _():
        rows(xs_ref, shs_ref, scs_ref, us_ref, hs_ref)


def _pw1_call(x, xs, mod, mod_s, norm_g, w, bias, *, layer, j):
    m, d = x.shape
    ms = xs.shape[0]
    tm, tn = ROW_TILE, PW1_TILE
    ch = w.shape[2] // 2
    nn = ch // tn
    sub = 1
    secs = [3 * sub, 3 * sub + 1]

    def wspec(sec):
        return pl.BlockSpec((None, d, tn), lambda i, n: (j, 0, sec * nn + n))

    def bspec(sec):
        return pl.BlockSpec((None, 1, tn), lambda i, n: (j, 0, sec * nn + n))

    bias3 = bias.reshape(bias.shape[0], 1, 2 * ch)
    return pl.pallas_call(
        _pw1_kernel,
        grid=(m // tm, nn),
        in_specs=(
            [pl.BlockSpec((tm, d), lambda i, n: (i, 0))]
            + _mod_specs(mod, layer, secs, d, lambda i, n: (i, 0), m // tm)
            + [_once((ms, d), lambda i, n: (0, 0))]
            + _mod_specs(mod_s, layer, secs, d, lambda i, n: (0, 0), 1)
            + [
                pl.BlockSpec((None, None, 1, d), lambda i, n: (layer, sub, 0, 0)),
                wspec(0), wspec(1), bspec(0), bspec(1),
            ]
        ),
        out_specs=[pl.BlockSpec((tm, tn), lambda i, n: (i, n)), pl.BlockSpec((ms, tn), _side_col(nn))],
        out_shape=[jax.ShapeDtypeStruct((m, ch), F32), jax.ShapeDtypeStruct((ms, ch), F32)],
        scratch_shapes=[pltpu.VMEM((tm, d), BF16), pltpu.VMEM((ms, d), BF16)],
        compiler_params=_cparams("arbitrary", "arbitrary"),
        name="conv_pw1",
    )(x, mod, mod, xs, mod_s, mod_s, norm_g, w, w, bias3, bias3)


def _ln_silu_into(y_ref, acc_ref, lng_ref, lnb_ref):
    def fn(rows):
        a = acc_ref[rows, :]
        mu = jnp.mean(a, axis=-1, keepdims=True)
        xc = a - mu
        yn = xc * lax.rsqrt(jnp.mean(xc * xc, axis=-1, keepdims=True) + EPS) * lng_ref[...] + lnb_ref[...]
        y_ref[rows, :] = _silu(yn).astype(BF16)

    _row_loop(acc_ref.shape[0], fn)


def _conv_seq_kernel(u_ref, w_ref, bdw_ref, lng_ref, lnb_ref, y_ref, tail_ref, win_ref, acc_ref, *, width):
    t = pl.program_id(1)
    tt, ch = u_ref.shape
    n_lt = ch // LANES
    off = CONV_PAD - (width - 1)
    blk = SUBLANES * SUBLANES

    @pl.when(t == 0)
    def _():
        win_ref[:, 0:CONV_PAD, :] = jnp.zeros((n_lt, CONV_PAD, LANES), F32)

    for l in range(n_lt):
        win_ref[l, CONV_PAD:CONV_PAD + tt, :] = u_ref[:, l * LANES:(l + 1) * LANES]

    def body(i, carry):
        l = i // (tt // blk)
        r0 = pl.multiple_of((i % (tt // blk)) * blk, blk)
        strided = [win_ref[l, pl.ds(r0 + r, SUBLANES, stride=SUBLANES), :]
                   for r in range(off, off + width - 1 + SUBLANES)]
        accs = [[jnp.broadcast_to(bdw_ref[l], (SUBLANES, LANES)), None] for _ in range(SUBLANES)]
        for tap in range(width):
            w_tap = jnp.broadcast_to(w_ref[l, tap:tap + 1, :], (SUBLANES, LANES))
            for t0 in range(SUBLANES):
                term = strided[t0 + tap] * w_tap
                prev = accs[t0][tap % 2]
                accs[t0][tap % 2] = term if prev is None else prev + term
        for t0 in range(SUBLANES):
            acc_ref[l, pl.ds(r0 + t0, SUBLANES, stride=SUBLANES), :] = accs[t0][0] + accs[t0][1]
        return carry

    lax.fori_loop(0, n_lt * (tt // blk), body, 0, unroll=2)

    for l in range(n_lt):
        tail_ref[:, l * LANES:(l + 1) * LANES] = win_ref[l, CONV_PAD + tt - (width - 1):CONV_PAD + tt, :]
    win_ref[:, 0:CONV_PAD, :] = win_ref[:, tt:tt + CONV_PAD, :]

    def fn(rows):
        a = jnp.concatenate([acc_ref[l, rows, :] for l in range(n_lt)], axis=1)
        mu = jnp.mean(a, axis=-1, keepdims=True)
        xc = a - mu
        yn = xc * lax.rsqrt(jnp.mean(xc * xc, axis=-1, keepdims=True) + EPS) * lng_ref[...] + lnb_ref[...]
        y_ref[rows, :] = _silu(yn).astype(BF16)

    _row_loop(tt, fn)


def _conv_seq_call(u, w_dw, b_dw, ln_g, ln_b, *, j, batch):
    m, ch = u.shape
    t = m // batch
    tt = min(CONV_TILE, t)
    nt = t // tt
    width = w_dw.shape[1]
    n_lt = ch // LANES
    vec = lambda a: a.reshape(a.shape[0], 1, ch)
    vspec = pl.BlockSpec((None, 1, ch), lambda b, i: (j, 0, 0))
    w_lt = jnp.transpose(w_dw[j].reshape(width, n_lt, LANES), (1, 0, 2))
    b_lt = b_dw[j].reshape(n_lt, 1, LANES)
    return pl.pallas_call(
        functools.partial(_conv_seq_kernel, width=width),
        grid=(batch, nt),
        in_specs=[
            pl.BlockSpec((tt, ch), lambda b, i: (b * nt + i, 0)),
            pl.BlockSpec((n_lt, width, LANES), lambda b, i: (0, 0, 0)),
            pl.BlockSpec((n_lt, 1, LANES), lambda b, i: (0, 0, 0)),
            vspec, vspec,
        ],
        out_specs=[
            pl.BlockSpec((tt, ch), lambda b, i: (b * nt + i, 0)),
            pl.BlockSpec((None, width - 1, ch), lambda b, i: (b, 0, 0)),
        ],
        out_shape=[jax.ShapeDtypeStruct((m, ch), BF16), jax.ShapeDtypeStruct((batch, width - 1, ch), F32)],
        scratch_shapes=[pltpu.VMEM((n_lt, CONV_PAD + tt, LANES), F32), pltpu.VMEM((n_lt, tt, LANES), F32)],
        compiler_params=_cparams("arbitrary", "arbitrary"),
        name="conv_seq",
    )(u, w_lt, b_lt, vec(ln_g), vec(ln_b))


def _conv_step_kernel(u_ref, buf_ref, w_ref, bdw_ref, lng_ref, lnb_ref, y_ref, new_ref, acc_ref, *, width):
    u = u_ref[...]
    acc = u * w_ref[width - 1:width, :] + bdw_ref[...]
    for tap in range(width - 1):
        acc = acc + buf_ref[tap] * w_ref[tap:tap + 1, :]
    acc_ref[...] = acc
    for tap in range(width - 2):
        new_ref[tap] = buf_ref[tap + 1]
    new_ref[width - 2] = u
    _ln_silu_into(y_ref, acc_ref, lng_ref, lnb_ref)


def _conv_step_call(u, buf, w_dw, b_dw, ln_g, ln_b, *, j, tb):
    nb, ch = u.shape
    width = w_dw.shape[1]
    vec = lambda a: a.reshape(a.shape[0], 1, ch)
    vspec = pl.BlockSpec((None, 1, ch), lambda i: (j, 0, 0))
    return pl.pallas_call(
        functools.partial(_conv_step_kernel, width=width),
        grid=(nb // tb,),
        in_specs=[
            pl.BlockSpec((tb, ch), lambda i: (i, 0)),
            pl.BlockSpec((None, width - 1, tb, ch), lambda i: (j, 0, i, 0)),
            pl.BlockSpec((None, width, ch), lambda i: (j, 0, 0)),
            vspec, vspec, vspec,
        ],
        out_specs=[
            pl.BlockSpec((tb, ch), lambda i: (i, 0)),
            pl.BlockSpec((width - 1, tb, ch), lambda i: (0, i, 0)),
        ],
        out_shape=[jax.ShapeDtypeStruct((nb, ch), BF16), jax.ShapeDtypeStruct((width - 1, nb, ch), F32)],
        scratch_shapes=[pltpu.VMEM((tb, ch), F32)],
        compiler_params=_cparams("arbitrary"),
        name="conv_step",
    )(u, buf, w_dw, vec(b_dw), vec(ln_g), vec(ln_b))


def kernel(x_prompt, x_sample, state_hgrn, state_conv, c_prompt, c_sample, hgrn_lb, hgrn_w_in, hgrn_b_f, hgrn_g_norm, hgrn_w_out, conv_w_pw1, conv_b_pw1, conv_w_dw, conv_b_dw, conv_ln_g, conv_ln_b, conv_w_pw2, conv_b_pw2, ada_w, ada_b, norm_g, ffn_w_in, ffn_w_out, final_g):
    bp, t, d = x_prompt.shape
    bs = x_sample.shape[0]
    depth = ada_w.shape[0]
    heads = d // LANES
    time_major = lambda s: jnp.transpose(s, (0, 2, 1, 3))

    mod_s, mod_p = _ada_call(c_sample, c_prompt, ada_w, ada_b)
    mod_s = mod_s.reshape(depth, 1, bs, N_SUB * 3 * d)
    mod_p = mod_p.reshape(depth, bp, 1, N_SUB * 3 * d)
    norm_g4 = norm_g.reshape(depth, N_SUB, 1, d)
    conv_buf = time_major(state_conv)

    x, xs = x_prompt.reshape(bp * t, d), x_sample.reshape(bs, d)
    hgrn_p, hgrn_s, conv_p, conv_s = [], [], [], []
    for layer in range(depth):
        x, xs = _ffn_call(x, xs, mod_p, mod_s, norm_g4, ffn_w_in, ffn_w_out, final_g,
                          layer=layer, sub=0, j=0, final=False)
        j = layer // 2
        if layer % 2 == 0:
            outs = _hgrn_in_call(x, xs, mod_p, mod_s, norm_g4, hgrn_w_in, hgrn_b_f, hgrn_lb,
                                 layer=layer, j=j, lb_row=layer)
            y, s = _hgrn_scan_call(*outs[:5], hgrn_g_norm[j], batch=bp, heads=heads)
            ys, ss = _hgrn_step_call(*outs[5:], hgrn_g_norm[j], state_hgrn[j], heads=heads)
            hgrn_p.append(s)
            hgrn_s.append(ss)
            x, xs = _proj_call(y, ys, hgrn_w_out, None, x, xs, mod_p, mod_s, layer=layer, j=j)
        else:
            u, us = _pw1_call(x, xs, mod_p, mod_s, norm_g4, conv_w_pw1, conv_b_pw1, layer=layer, j=j)
            y, s = _conv_seq_call(u, conv_w_dw, conv_b_dw, conv_ln_g, conv_ln_b, j=j, batch=bp)
            ys, ss = _conv_step_call(us, conv_buf, conv_w_dw, conv_b_dw, conv_ln_g, conv_ln_b, j=j, tb=4 * SUBLANES)
            conv_p.append(jnp.transpose(s, (1, 0, 2)))
            conv_s.append(ss)
            x, xs = _proj_call(y, ys, conv_w_pw2, conv_b_pw2, x, xs, mod_p, mod_s, layer=layer, j=j)
        x, xs = _ffn_call(x, xs, mod_p, mod_s, norm_g4, ffn_w_in, ffn_w_out, final_g,
                          layer=layer, sub=2, j=1, final=(layer == depth - 1))
    return (x.reshape(bp, t, d), xs.reshape(bs, 1, d), jnp.stack(hgrn_p), jnp.stack(hgrn_s),
            time_major(jnp.stack(conv_p)), time_major(jnp.stack(conv_s)))
```

```python
import functools

import jax
import jax.numpy as jnp
from jax import lax
from jax.experimental import pallas as pl
from jax.experimental.pallas import tpu as pltpu

F32 = jnp.float32
BF16 = jnp.bfloat16
EPS = 1e-6
N_SUB = 3
LANES = 128
SUBLANES = 8
VMEM_LIMIT_BYTES = 60 * 1024 * 1024

ROW_TILE = 1024
FFN_TILE = 256
IN_TILE = 256
PW1_TILE = 512
OUT_TILE = 2048
OUT_ROW_TILE = 512
SCAN_ROWS = 256
SCAN_SUB = 64
SCAN_SPAN_MAX = 120.0
SCAN_SLOW_ROWS = 16
CONV_PAD = 32
CONV_TILE = 256
CONV_SLAB = 256
NORM_ROWS = 16
NORM_UNROLL = 4


def _cparams(*sem):
    return pltpu.CompilerParams(dimension_semantics=sem, vmem_limit_bytes=VMEM_LIMIT_BYTES)


def _dot(a, b):
    return jnp.dot(a, b, preferred_element_type=F32)


def _dot_nt(a, b):
    return lax.dot_general(a, b, (((1,), (1,)), ((), ())), preferred_element_type=F32)


def _dot_tn(a, b):
    return lax.dot_general(a, b, (((0,), (0,)), ((), ())), preferred_element_type=F32)


def _sigmoid(x):
    return 1.0 / (1.0 + jnp.exp(-x))


def _silu(x):
    return x * _sigmoid(x)


def _rmsnorm(x, g):
    return x * lax.rsqrt(jnp.mean(x * x, axis=-1, keepdims=True) + EPS) * g


def _row_loop(n_rows, fn, in_place=False):
    rows = min(NORM_ROWS * NORM_UNROLL if in_place else NORM_ROWS, n_rows)
    steps = n_rows // rows
    if steps == 1:
        fn(pl.ds(0, rows))
        return

    def body(i, carry):
        fn(pl.ds(pl.multiple_of(i * rows, rows), rows))
        return carry

    lax.fori_loop(0, steps, body, 0, unroll=NORM_UNROLL if not in_place and steps % NORM_UNROLL == 0 else 1)


def _mod_rows(ref, rows):
    return ref[...] if ref.shape[0] == 1 else ref[rows, :]


def _modulate_into(h_ref, x_ref, g_ref, sh_ref, sc_ref):
    per_tile = sc_ref.shape[0] == 1
    gain_tile = g_ref[...] * (1.0 + sc_ref[...]) if per_tile else None

    def fn(rows):
        x = x_ref[rows, :]
        gain = gain_tile if per_tile else g_ref[...] * (1.0 + sc_ref[rows, :])
        y = x * lax.rsqrt(jnp.mean(x * x, axis=-1, keepdims=True) + EPS) * gain + _mod_rows(sh_ref, rows)
        h_ref[rows, :] = y.astype(BF16)

    _row_loop(x_ref.shape[0], fn)


def _once(shape, index_map):
    return pl.BlockSpec(shape, index_map, pipeline_mode=pl.Buffered(1))


def _mod_specs(mod, layer, secs, width, col_of, row_tiles):
    groups, r = mod.shape[1], mod.shape[2]
    tiles_per_group = max(row_tiles // groups, 1)
    ncol = mod.shape[3] // (N_SUB * 3) // width

    def spec(sec):
        def index_map(*idx):
            i, n = col_of(*idx)
            return (layer, i // tiles_per_group, 0, sec * ncol + n)
        if groups == 1 and ncol == 1:
            return _once((None, None, r, width), index_map)
        return pl.BlockSpec((None, None, r, width), index_map)

    return [spec(s) for s in secs]


def _ada_kernel(cs_ref, cp_ref, w_ref, b_ref, os_ref, op_ref):
    w = w_ref[...].astype(BF16)
    os_ref[...] = _dot(_silu(cs_ref[...]).astype(BF16), w) + b_ref[...]
    op_ref[...] = _dot(_silu(cp_ref[...]).astype(BF16), w) + b_ref[...]


def _ada_call(c_sample, c_prompt, ada_w, ada_b, tn=1024):
    depth, d, n = ada_w.shape
    rs, rp = c_sample.shape[0], c_prompt.shape[0]
    return pl.pallas_call(
        _ada_kernel,
        grid=(depth, n // tn),
        in_specs=[
            pl.BlockSpec((rs, d), lambda l, j: (0, 0)),
            pl.BlockSpec((rp, d), lambda l, j: (0, 0)),
            pl.BlockSpec((None, d, tn), lambda l, j: (l, 0, j)),
            pl.BlockSpec((None, 1, tn), lambda l, j: (l, 0, j)),
        ],
        out_specs=[
            pl.BlockSpec((None, rs, tn), lambda l, j: (l, 0, j)),
            pl.BlockSpec((None, rp, tn), lambda l, j: (l, 0, j)),
        ],
        out_shape=[jax.ShapeDtypeStruct((depth, rs, n), F32), jax.ShapeDtypeStruct((depth, rp, n), F32)],
        compiler_params=_cparams("arbitrary", "arbitrary"),
        name="adaln_params",
    )(c_sample, c_prompt, ada_w, ada_b.reshape(depth, 1, n))


def _ffn_kernel(x_ref, sh_ref, sc_ref, gt_ref, xs_ref, shs_ref, scs_ref, gts_ref, g_ref, wa_ref, wu_ref, wo_ref,
                *rest, final):
    fg_ref = rest[0] if final else None
    o_ref, os_ref, h_ref = rest[-3:]
    tm, ms = x_ref.shape[0], xs_ref.shape[0]
    first_tile = pl.program_id(0) == 0
    f = pl.program_id(1)
    last = pl.num_programs(1) - 1

    def start(x_ref, sh_ref, sc_ref, o_ref, row0):
        _modulate_into(h_ref.at[pl.ds(row0, x_ref.shape[0])], x_ref, g_ref, sh_ref, sc_ref)
        o_ref[...] = jnp.zeros_like(o_ref)

    def finish(x_ref, gt_ref, o_ref):
        def fn(rows):
            y = x_ref[rows, :] + 0.5 * _mod_rows(gt_ref, rows) * o_ref[rows, :]
            if fg_ref is not None:
                y = _rmsnorm(y, fg_ref[...])
            o_ref[rows, :] = y

        _row_loop(x_ref.shape[0], fn, in_place=True)

    def hidden(rows):
        h = h_ref[0:rows, :]
        return (_silu(_dot(h, wa_ref[...].astype(BF16))) * _dot(h, wu_ref[...].astype(BF16))).astype(BF16)

    @pl.when(f == 0)
    def _():
        start(x_ref, sh_ref, sc_ref, o_ref, 0)

    @pl.when((f == 0) & first_tile)
    def _():
        start(xs_ref, shs_ref, scs_ref, os_ref, tm)

    @pl.when(jnp.logical_not(first_tile))
    def _():
        o_ref[...] += _dot(hidden(tm), wo_ref[...].astype(BF16))

    @pl.when(first_tile)
    def _():
        act = hidden(tm + ms)
        wo = wo_ref[...].astype(BF16)
        o_ref[...] += _dot(act[0:tm], wo)
        os_ref[...] += _dot(act[tm:tm + ms], wo)

    @pl.when(f == last)
    def _():
        finish(x_ref, gt_ref, o_ref)

    @pl.when((f == last) & first_tile)
    def _():
        finish(xs_ref, gts_ref, os_ref)


def _ffn_call(x, xs, mod, mod_s, norm_g, w_in, w_out, final_g, *, layer, sub, j, final):
    m, d = x.shape
    ms = xs.shape[0]
    tm, tf = ROW_TILE, FFN_TILE
    nf = w_out.shape[2] // tf
    secs = [3 * sub, 3 * sub + 1, 3 * sub + 2]
    in_specs = (
        [pl.BlockSpec((tm, d), lambda i, f: (i, 0))]
        + _mod_specs(mod, layer, secs, d, lambda i, f: (i, 0), m // tm)
        + [_once((ms, d), lambda i, f: (0, 0))]
        + _mod_specs(mod_s, layer, secs, d, lambda i, f: (0, 0), 1)
        + [
            pl.BlockSpec((None, None, 1, d), lambda i, f: (layer, sub, 0, 0)),
            pl.BlockSpec((None, None, d, tf), lambda i, f: (layer, j, 0, f)),
            pl.BlockSpec((None, None, d, tf), lambda i, f: (layer, j, 0, nf + f)),
            pl.BlockSpec((None, None, tf, d), lambda i, f: (layer, j, f, 0)),
        ]
    )
    args = [x, mod, mod, mod, xs, mod_s, mod_s, mod_s, norm_g, w_in, w_in, w_out]
    if final:
        in_specs.append(pl.BlockSpec((1, d), lambda i, f: (0, 0)))
        args.append(final_g.reshape(1, d))
    return pl.pallas_call(
        functools.partial(_ffn_kernel, final=final),
        grid=(m // tm, nf),
        in_specs=in_specs,
        out_specs=[pl.BlockSpec((tm, d), lambda i, f: (i, 0)), _once((ms, d), lambda i, f: (0, 0))],
        out_shape=[jax.ShapeDtypeStruct((m, d), F32), jax.ShapeDtypeStruct((ms, d), F32)],
        scratch_shapes=[pltpu.VMEM((tm + ms, d), BF16)],
        compiler_params=_cparams("arbitrary", "arbitrary"),
        name="ffn",
    )(*args)


def _hgrn_in_rows(x_ref, sh_ref, sc_ref, g_ref, bf_ref, h_ref, outs, ws, lb):
    q_ref, k_ref, v_ref, lf_ref, sg_ref = outs
    wq_ref, wf_ref, wi_ref, wg_ref = ws

    @pl.when(pl.program_id(1) == 0)
    def _():
        _modulate_into(h_ref, x_ref, g_ref, sh_ref, sc_ref)

    h = h_ref[...]
    q_ref[...] = _dot(h, wq_ref[...].astype(BF16))
    z = _dot(h, wf_ref[...].astype(BF16)) + bf_ref[...]
    lf_ref[...] = jnp.log(lb + (1.0 - lb) * _sigmoid(z))
    k_ref[...] = (1.0 - lb) * _sigmoid(-z)
    v_ref[...] = _dot(h, wi_ref[...].astype(BF16)).astype(BF16)
    sg_ref[...] = _silu(_dot(h, wg_ref[...].astype(BF16)))


def _hgrn_in_kernel(x_ref, sh_ref, sc_ref, xs_ref, shs_ref, scs_ref, g_ref, wq_ref, wf_ref, wi_ref, wg_ref,
                    bf_ref, lb_ref, *rest, lb_row):
    outs, outs_s, (h_ref, hs_ref) = rest[0:5], rest[5:10], rest[10:12]
    raw = [lb_ref[i:i + 1, :] for i in range(lb_ref.shape[0])]
    top = functools.reduce(jnp.maximum, raw)
    e = [jnp.exp(r - top) for r in raw]
    lb = sum(e[:lb_row + 1]) / sum(e)
    ws = (wq_ref, wf_ref, wi_ref, wg_ref)
    _hgrn_in_rows(x_ref, sh_ref, sc_ref, g_ref, bf_ref, h_ref, outs, ws, lb)

    @pl.when(pl.program_id(0) == 0)
    def _():
        _hgrn_in_rows(xs_ref, shs_ref, scs_ref, g_ref, bf_ref, hs_ref, outs_s, ws, lb)


def _side_col(nn):
    return lambda i, n: (0, jnp.where(i == 0, n, nn - 1))


def _hgrn_in_call(x, xs, mod, mod_s, norm_g, w_in, b_f, lb_raw, *, layer, j, lb_row):
    m, d = x.shape
    ms = xs.shape[0]
    tm, tn = ROW_TILE, IN_TILE
    nn = d // tn
    sub = 1
    secs = [3 * sub, 3 * sub + 1]

    def wspec(sec):
        return pl.BlockSpec((None, d, tn), lambda i, n: (j, 0, sec * nn + n))

    def out_shapes(rows):
        return [jax.ShapeDtypeStruct((rows, d), dt) for dt in (F32, F32, BF16, F32, F32)]

    return pl.pallas_call(
        functools.partial(_hgrn_in_kernel, lb_row=lb_row),
        grid=(m // tm, nn),
        in_specs=(
            [pl.BlockSpec((tm, d), lambda i, n: (i, 0))]
            + _mod_specs(mod, layer, secs, d, lambda i, n: (i, 0), m // tm)
            + [_once((ms, d), lambda i, n: (0, 0))]
            + _mod_specs(mod_s, layer, secs, d, lambda i, n: (0, 0), 1)
            + [
                pl.BlockSpec((None, None, 1, d), lambda i, n: (layer, sub, 0, 0)),
                wspec(0), wspec(1), wspec(2), wspec(3),
                pl.BlockSpec((None, 1, tn), lambda i, n: (j, 0, n)),
                pl.BlockSpec((lb_raw.shape[0], tn), lambda i, n: (0, n)),
            ]
        ),
        out_specs=[pl.BlockSpec((tm, tn), lambda i, n: (i, n))] * 5 + [pl.BlockSpec((ms, tn), _side_col(nn))] * 5,
        out_shape=out_shapes(m) + out_shapes(ms),
        scratch_shapes=[pltpu.VMEM((tm, d), BF16), pltpu.VMEM((ms, d), BF16)],
        compiler_params=_cparams("arbitrary", "arbitrary"),
        name="hgrn_in",
    )(x, mod, mod, xs, mod_s, mod_s, norm_g, w_in, w_in, w_in, w_in, b_f.reshape(b_f.shape[0], 1, d), lb_raw)


def _hgrn_scan_kernel(q_ref, k_ref, v_ref, lf_ref, sg_ref, gn_ref, y_ref, sfin_ref, st_ref, b_ref, o_ref, *,
                      heads, span_max):
    c = pl.program_id(1)
    rows, d = q_ref.shape
    dk = d // heads
    sub = min(SCAN_SUB, rows)
    n_sub = rows // sub
    n_levels = n_sub.bit_length() - 1
    sub_shift = sub.bit_length() - 1

    @pl.when(c == 0)
    def _():
        st_ref[...] = jnp.zeros_like(st_ref)

    row = lax.broadcasted_iota(jnp.int32, (rows, rows), 0)
    col = lax.broadcasted_iota(jnp.int32, (rows, rows), 1)
    tri = jnp.where(col <= row, 1.0, 0.0).astype(BF16)
    lf = lf_ref[...]
    p1 = lf.astype(BF16)
    r1 = lf - p1.astype(F32)
    p2 = r1.astype(BF16)
    p3 = (r1 - p2.astype(F32)).astype(BF16)
    b_ref[...] = _dot(tri, p1) + _dot(tri, p2) + _dot(tri, p3)

    def first_row(j):
        return slice(j * sub, j * sub + 1)

    def last_row(j):
        return slice((j + 1) * sub - 1, (j + 1) * sub)

    spans = [b_ref[first_row(j), :] - b_ref[last_row(j), :] for j in range(n_sub)]
    safe = jnp.max(functools.reduce(jnp.maximum, spans)) <= span_max

    def gated_out(o, sl):
        return (_rmsnorm(o, gn_ref[:, sl]) * sg_ref[:, sl]).astype(BF16)

    @pl.when(safe)
    def _():
        srl = lax.shift_right_logical
        level = jnp.where((srl(row, sub_shift) == srl(col, sub_shift)) & (col <= row), 0, -1)
        later_half = []
        row_d = lax.broadcasted_iota(jnp.int32, (rows, dk), 0)
        for l in range(1, n_levels + 1):
            s = sub_shift + l - 1
            hit = (srl(row, s + 1) == srl(col, s + 1)) & ((srl(row, s) & 1) == 1) & ((srl(col, s) & 1) == 0)
            level = jnp.where(hit, l, level)
            later_half.append((srl(row_d, s) & 1) == 1)

        def rows_of(pieces, n):
            return jnp.concatenate([jnp.broadcast_to(p, (n, dk)) for p in pieces], axis=0)

        for h in range(heads):
            sl = slice(h * dk, (h + 1) * dk)
            b = b_ref[:, sl]
            b_last = b_ref[rows - 1:rows, sl]
            q = q_ref[:, sl]
            k = k_ref[:, sl]
            v = v_ref[:, sl]
            rel = b - rows_of([0.5 * (b_ref[first_row(j), sl] + b_ref[last_row(j), sl]) for j in range(n_sub)], sub)
            att = jnp.where(level == 0,
                            _dot_nt((q * jnp.exp(rel)).astype(BF16), (k * jnp.exp(-rel)).astype(BF16)), 0.0)
            for l in range(1, n_levels + 1):
                m = sub << (l - 1)
                rel = b - rows_of([b_ref[p * 2 * m + m:p * 2 * m + m + 1, sl] for p in range(rows // (2 * m))], 2 * m)
                e = jnp.exp(jnp.minimum(jnp.where(later_half[l - 1], rel, -rel), 0.0))
                att = jnp.where(level == l, _dot_nt((q * e).astype(BF16), (k * e).astype(BF16)), att)
            q_st = (q * jnp.exp(b)).astype(BF16)
            k_st = (k * jnp.exp(b_last - b)).astype(BF16)
            st = st_ref[h]
            o = _dot(att.astype(BF16), v) + _dot_nt(q_st, st.astype(BF16))
            st_ref[h] = st * jnp.exp(b_last) + _dot_tn(v, k_st)
            y_ref[:, sl] = gated_out(o, sl)

    @pl.when(jnp.logical_not(safe))
    def _():
        grp_rows = min(SCAN_SLOW_ROWS, rows)
        grp_shift = grp_rows.bit_length() - 1
        pick = lax.broadcasted_iota(jnp.int32, (grp_rows, dk), 0)
        o_ref[...] = jnp.zeros_like(o_ref)

        def token(t, carry):
            r0 = pl.multiple_of(lax.shift_left(lax.shift_right_logical(t, grp_shift), grp_shift), grp_rows)
            sel = pick == t - r0
            grp = pl.ds(r0, grp_rows)
            for h in range(heads):
                sl = slice(h * dk, (h + 1) * dk)
                f = jnp.sum(jnp.where(sel, jnp.exp(lf_ref[grp, sl]), 0.0), axis=0, keepdims=True)
                q1 = jnp.where(sel, q_ref[grp, sl], 0.0).astype(BF16)
                k1 = jnp.where(sel, k_ref[grp, sl], 0.0).astype(BF16)
                v1 = jnp.where(sel, v_ref[grp, sl].astype(F32), 0.0).astype(BF16)
                st = st_ref[h] * f + _dot_tn(v1, k1)
                st_ref[h] = st
                o_ref[grp, sl] = jnp.where(sel, _dot_nt(q1, st.astype(BF16)), o_ref[grp, sl])
            return carry

        lax.fori_loop(0, rows, token, 0)
        for h in range(heads):
            sl = slice(h * dk, (h + 1) * dk)
            y_ref[:, sl] = gated_out(o_ref[:, sl], sl)

    @pl.when(c == pl.num_programs(1) - 1)
    def _():
        for h in range(heads):
            sfin_ref[h] = st_ref[h].T


def _hgrn_scan_call(q, k, v, lf, sg, g_norm, *, batch, heads, span_max=SCAN_SPAN_MAX):
    m, d = q.shape
    t = m // batch
    rows = min(SCAN_ROWS, t)
    nc = t // rows
    dk = d // heads
    spec = pl.BlockSpec((rows, d), lambda b, c: (b * nc + c, 0))
    return pl.pallas_call(
        functools.partial(_hgrn_scan_kernel, heads=heads, span_max=span_max),
        grid=(batch, nc),
        in_specs=[spec, spec, spec, spec, spec, pl.BlockSpec((1, d), lambda b, c: (0, 0))],
        out_specs=[spec, pl.BlockSpec((None, heads, dk, dk), lambda b, c: (b, 0, 0, 0))],
        out_shape=[jax.ShapeDtypeStruct((m, d), BF16), jax.ShapeDtypeStruct((batch, heads, dk, dk), F32)],
        scratch_shapes=[pltpu.VMEM((heads, dk, dk), F32), pltpu.VMEM((rows, d), F32), pltpu.VMEM((rows, d), F32)],
        compiler_params=_cparams("arbitrary", "arbitrary"),
        name="hgrn_scan",
    )(q, k, v, lf, sg, g_norm.reshape(1, d))


def _hgrn_step_kernel(q_ref, k_ref, v_ref, lf_ref, sg_ref, gn_ref, s0_ref, y_ref, s1_ref, o_ref):
    nb = q_ref.shape[0]
    q_t = q_ref[...].T
    k_t = k_ref[...].T
    f_t = jnp.exp(lf_ref[...]).T
    v = v_ref[...].astype(F32)
    for b in range(nb):
        s1 = s0_ref[b] * f_t[:, b:b + 1] + k_t[:, b:b + 1] * v[b:b + 1, :]
        s1_ref[b] = s1
        o_ref[b:b + 1, :] = jnp.sum(s1 * q_t[:, b:b + 1], axis=0, keepdims=True)
    y = _rmsnorm(o_ref[...], gn_ref[...]) * sg_ref[...]
    y_ref[...] = y.astype(BF16)


def _hgrn_step_call(q, k, v, lf, sg, g_norm, s0, *, heads):
    nb, d = q.shape
    dk = d // heads
    spec = pl.BlockSpec((nb, dk), lambda h: (0, h))
    sspec = pl.BlockSpec((nb, None, dk, dk), lambda h: (0, h, 0, 0))
    return pl.pallas_call(
        _hgrn_step_kernel,
        grid=(heads,),
        in_specs=[spec, spec, spec, spec, spec, pl.BlockSpec((1, dk), lambda h: (0, h)), sspec],
        out_specs=[spec, sspec],
        out_shape=[jax.ShapeDtypeStruct((nb, d), BF16), jax.ShapeDtypeStruct(s0.shape, F32)],
        scratch_shapes=[pltpu.VMEM((nb, dk), F32)],
        compiler_params=_cparams("arbitrary"),
        name="hgrn_step",
    )(q, k, v, lf, sg, g_norm.reshape(1, d), s0)


def _proj_kernel(y_ref, x_ref, gt_ref, ys_ref, xs_ref, gts_ref, w_ref, *rest, has_bias):
    b_ref = rest[0] if has_bias else None
    o_ref, os_ref = rest[-2:]
    def rows(y_ref, x_ref, gt_ref, o_ref):
        out = _dot(y_ref[...], w_ref[...].astype(BF16))
        if has_bias:
            out = out + b_ref[...]
        o_ref[...] = x_ref[...] + gt_ref[...] * out

    rows(y_ref, x_ref, gt_ref, o_ref)

    @pl.when(pl.program_id(1) == 0)
    def _():
        rows(ys_ref, xs_ref, gts_ref, os_ref)


def _proj_call(y, ys, w, bias, x, xs, mod, mod_s, *, layer, j):
    m, kdim = y.shape
    ms = ys.shape[0]
    d = x.shape[1]
    tm, tn = OUT_ROW_TILE, min(OUT_TILE, d)
    gate = [3 * 1 + 2]
    has_bias = bias is not None
    wspec = _once if tn == d else pl.BlockSpec
    in_specs = (
        [pl.BlockSpec((tm, kdim), lambda n, i: (i, 0)), pl.BlockSpec((tm, tn), lambda n, i: (i, n))]
        + _mod_specs(mod, layer, gate, tn, lambda n, i: (i, n), m // tm)
        + [_once((ms, kdim), lambda n, i: (0, 0)), pl.BlockSpec((ms, tn), lambda n, i: (0, n))]
        + _mod_specs(mod_s, layer, gate, tn, lambda n, i: (0, n), 1)
        + [wspec((None, kdim, tn), lambda n, i: (j, 0, n))]
    )
    args = [y, x, mod, ys, xs, mod_s, w]
    if has_bias:
        in_specs.append(pl.BlockSpec((None, 1, tn), lambda n, i: (j, 0, n)))
        args.append(bias.reshape(bias.shape[0], 1, d))
    return pl.pallas_call(
        functools.partial(_proj_kernel, has_bias=has_bias),
        grid=(d // tn, m // tm),
        in_specs=in_specs,
        out_specs=[pl.BlockSpec((tm, tn), lambda n, i: (i, n)), pl.BlockSpec((ms, tn), lambda n, i: (0, n))],
        out_shape=[jax.ShapeDtypeStruct((m, d), F32), jax.ShapeDtypeStruct((ms, d), F32)],
        compiler_params=_cparams("arbitrary", "arbitrary"),
        name="proj_residual",
    )(*args)


def _proj_rows_kernel(y_ref, x_ref, gt_ref, w_ref, b_ref, o_ref):
    o_ref[...] = x_ref[...] + gt_ref[...] * (_dot(y_ref[...], w_ref[...].astype(BF16)) + b_ref[...])


def _proj_rows_call(y, w, bias, x, mod, *, layer, j):
    ms, kdim = y.shape
    d = x.shape[1]
    tn = IN_TILE
    return pl.pallas_call(
        _proj_rows_kernel,
        grid=(d // tn,),
        in_specs=(
            [_once((ms, kdim), lambda n: (0, 0)), pl.BlockSpec((ms, tn), lambda n: (0, n))]
            + _mod_specs(mod, layer, [3 * 1 + 2], tn, lambda n: (0, n), 1)
            + [pl.BlockSpec((None, kdim, tn), lambda n: (j, 0, n)), pl.BlockSpec((None, 1, tn), lambda n: (j, 0, n))]
        ),
        out_specs=pl.BlockSpec((ms, tn), lambda n: (0, n)),
        out_shape=jax.ShapeDtypeStruct((ms, d), F32),
        compiler_params=_cparams("arbitrary"),
        name="proj_rows",
    )(y, x, mod, w, bias.reshape(bias.shape[0], 1, d))


def _pw1_kernel(x_ref, sh_ref, sc_ref, xs_ref, shs_ref, scs_ref, g_ref, wa_ref, wg_ref, ba_ref, bg_ref,
                u_ref, us_ref, h_ref, hs_ref):
    def rows(x_ref, sh_ref, sc_ref, u_ref, h_ref):
        @pl.when(pl.program_id(1) == 0)
        def _():
            _modulate_into(h_ref, x_ref, g_ref, sh_ref, sc_ref)

        h = h_ref[...]
        a = _dot(h, wa_ref[...].astype(BF16)) + ba_ref[...]
        u_ref[...] = a * _sigmoid(_dot(h, wg_ref[...].astype(BF16)) + bg_ref[...])

    rows(x_ref, sh_ref, sc_ref, u_ref, h_ref)

    @pl.when(pl.program_id(0) == 0)
    def _():
        rows(xs_ref, shs_ref, scs_ref, us_ref, hs_ref)


def _pw1_call(x, xs, mod, mod_s, norm_g, w, bias, *, layer, j):
    m, d = x.shape
    ms = xs.shape[0]
    tm, tn = ROW_TILE, PW1_TILE
    ch = w.shape[2] // 2
    nn = ch // tn
    sub = 1
    secs = [3 * sub, 3 * sub + 1]

    def wspec(sec):
        return pl.BlockSpec((None, d, tn), lambda i, n: (j, 0, sec * nn + n))

    def bspec(sec):
        return pl.BlockSpec((None, 1, tn), lambda i, n: (j, 0, sec * nn + n))

    bias3 = bias.reshape(bias.shape[0], 1, 2 * ch)
    return pl.pallas_call(
        _pw1_kernel,
        grid=(m // tm, nn),
        in_specs=(
            [pl.BlockSpec((tm, d), lambda i, n: (i, 0))]
            + _mod_specs(mod, layer, secs, d, lambda i, n: (i, 0), m // tm)
            + [_once((ms, d), lambda i, n: (0, 0))]
            + _mod_specs(mod_s, layer, secs, d, lambda i, n: (0, 0), 1)
            + [
                pl.BlockSpec((None, None, 1, d), lambda i, n: (layer, sub, 0, 0)),
                wspec(0), wspec(1), bspec(0), bspec(1),
            ]
        ),
        out_specs=[pl.BlockSpec((tm, tn), lambda i, n: (i, n)), pl.BlockSpec((ms, tn), _side_col(nn))],
        out_shape=[jax.ShapeDtypeStruct((m, ch), F32), jax.ShapeDtypeStruct((ms, ch), F32)],
        scratch_shapes=[pltpu.VMEM((tm, d), BF16), pltpu.VMEM((ms, d), BF16)],
        compiler_params=_cparams("arbitrary", "arbitrary"),
        name="conv_pw1",
    )(x, mod, mod, xs, mod_s, mod_s, norm_g, w, w, bias3, bias3)


def _ln_silu_into(y_ref, acc_ref, lng_ref, lnb_ref):
    def fn(rows):
        a = acc_ref[rows, :]
        mu = jnp.mean(a, axis=-1, keepdims=True)
        xc = a - mu
        yn = xc * lax.rsqrt(jnp.mean(xc * xc, axis=-1, keepdims=True) + EPS) * lng_ref[...] + lnb_ref[...]
        y_ref[rows, :] = _silu(yn).astype(BF16)

    _row_loop(acc_ref.shape[0], fn)


def _conv_seq_kernel(u_ref, w_ref, bdw_ref, lng_ref, lnb_ref, x_ref, gt_ref, w2_ref, b2_ref, o_ref, tail_ref,
                     win_ref, acc_ref, y_ref, w2b_ref, *, width):
    t = pl.program_id(1)
    nt = pl.num_programs(1) - 1
    tt, ch = u_ref.shape
    n_lt = ch // LANES
    off = CONV_PAD - (width - 1)
    blk = SUBLANES * SUBLANES
    n_slab, _, slab = w2b_ref.shape
    lt_per_slab = n_lt // n_slab

    @pl.when((pl.program_id(0) == 0) & (t == 0))
    def _():
        for s in range(n_slab):
            w2b_ref[s] = w2_ref[:, s * slab:(s + 1) * slab].astype(BF16)

    @pl.when(t == 0)
    def _():
        win_ref[:, 0:CONV_PAD, :] = jnp.zeros((n_lt, CONV_PAD, LANES), F32)

    @pl.when(t < nt)
    def _():
        for l in range(n_lt):
            win_ref[l, CONV_PAD:CONV_PAD + tt, :] = u_ref[:, l * LANES:(l + 1) * LANES]

    def project(s):
        cols = slice(s * slab, (s + 1) * slab)
        out = _dot(y_ref[(t + 1) % 2], w2b_ref[s]) + b2_ref[:, cols]
        o_ref[:, cols] = x_ref[:, cols] + gt_ref[:, cols] * out

    def conv_block(l, r0):
        strided = [win_ref[l, pl.ds(r0 + r, SUBLANES, stride=SUBLANES), :]
                   for r in range(off, off + width - 1 + SUBLANES)]
        accs = [[jnp.broadcast_to(bdw_ref[l], (SUBLANES, LANES)), None] for _ in range(SUBLANES)]
        for tap in range(width):
            w_tap = jnp.broadcast_to(w_ref[l, tap:tap + 1, :], (SUBLANES, LANES))
            for t0 in range(SUBLANES):
                term = strided[t0 + tap] * w_tap
                prev = accs[t0][tap % 2]
                accs[t0][tap % 2] = term if prev is None else prev + term
        for t0 in range(SUBLANES):
            acc_ref[l, pl.ds(r0 + t0, SUBLANES, stride=SUBLANES), :] = accs[t0][0] + accs[t0][1]

    def conv_lane_tiles(lts):
        for l in lts:
            for r0 in range(0, tt, blk):
                conv_block(l, r0)

    @pl.when(t == 0)
    def _():
        def body(i, carry):
            conv_block(i // (tt // blk), pl.multiple_of((i % (tt // blk)) * blk, blk))
            return carry

        lax.fori_loop(0, n_lt * (tt // blk), body, 0, unroll=2)

    @pl.when((t > 0) & (t < nt))
    def _():
        for s in range(n_slab):
            project(s)
            conv_lane_tiles(range(s * lt_per_slab, (s + 1) * lt_per_slab))

    @pl.when(t == nt)
    def _():
        for s in range(n_slab):
            project(s)

    @pl.when(t < nt)
    def _():
        for l in range(n_lt):
            tail_ref[:, l * LANES:(l + 1) * LANES] = win_ref[l, CONV_PAD + tt - (width - 1):CONV_PAD + tt, :]
        win_ref[:, 0:CONV_PAD, :] = win_ref[:, tt:tt + CONV_PAD, :]
        y_cur = y_ref.at[t % 2]

        def fn(rows):
            a = jnp.concatenate([acc_ref[l, rows, :] for l in range(n_lt)], axis=1)
            mu = jnp.mean(a, axis=-1, keepdims=True)
            xc = a - mu
            yn = xc * lax.rsqrt(jnp.mean(xc * xc, axis=-1, keepdims=True) + EPS) * lng_ref[...] + lnb_ref[...]
            y_cur[rows, :] = _silu(yn).astype(BF16)

        _row_loop(tt, fn)


def _conv_seq_call(u, w_dw, b_dw, ln_g, ln_b, x, mod, w2, b2, *, layer, j, batch):
    m, ch = u.shape
    d = x.shape[1]
    t = m // batch
    tt = min(CONV_TILE, t)
    nt = t // tt
    width = w_dw.shape[1]
    n_lt = ch // LANES
    n_slab = d // CONV_SLAB
    vec = lambda a: a.reshape(a.shape[0], 1, a.shape[-1])
    vspec = pl.BlockSpec((None, 1, ch), lambda b, i: (j, 0, 0))
    w_lt = jnp.transpose(w_dw[j].reshape(width, n_lt, LANES), (1, 0, 2))
    b_lt = b_dw[j].reshape(n_lt, 1, LANES)
    conv_tile = lambda b, i: (b * nt + jnp.minimum(i, nt - 1), 0)
    proj_tile = lambda b, i: (b * nt + jnp.maximum(i - 1, 0), 0)
    return pl.pallas_call(
        functools.partial(_conv_seq_kernel, width=width),
        grid=(batch, nt + 1),
        in_specs=[
            pl.BlockSpec((tt, ch), conv_tile),
            pl.BlockSpec((n_lt, width, LANES), lambda b, i: (0, 0, 0)),
            pl.BlockSpec((n_lt, 1, LANES), lambda b, i: (0, 0, 0)),
            vspec, vspec,
            pl.BlockSpec((tt, d), proj_tile),
            pl.BlockSpec((None, None, 1, d), lambda b, i: (layer, b, 0, 3 * 1 + 2)),
            _once((None, ch, d), lambda b, i: (j, 0, 0)),
            pl.BlockSpec((None, 1, d), lambda b, i: (j, 0, 0)),
        ],
        out_specs=[
            pl.BlockSpec((tt, d), proj_tile),
            pl.BlockSpec((None, width - 1, ch), lambda b, i: (b, 0, 0)),
        ],
        out_shape=[jax.ShapeDtypeStruct((m, d), F32), jax.ShapeDtypeStruct((batch, width - 1, ch), F32)],
        scratch_shapes=[
            pltpu.VMEM((n_lt, CONV_PAD + tt, LANES), F32),
            pltpu.VMEM((n_lt, tt, LANES), F32),
            pltpu.VMEM((2, tt, ch), BF16),
            pltpu.VMEM((n_slab, ch, CONV_SLAB), BF16),
        ],
        compiler_params=_cparams("arbitrary", "arbitrary"),
        name="conv_seq",
    )(u, w_lt, b_lt, vec(ln_g), vec(ln_b), x, mod, w2, vec(b2))


def _conv_step_kernel(u_ref, buf_ref, w_ref, bdw_ref, lng_ref, lnb_ref, y_ref, new_ref, acc_ref, *, width):
    u = u_ref[...]
    acc = u * w_ref[width - 1:width, :] + bdw_ref[...]
    for tap in range(width - 1):
        acc = acc + buf_ref[tap] * w_ref[tap:tap + 1, :]
    acc_ref[...] = acc
    for tap in range(width - 2):
        new_ref[tap] = buf_ref[tap + 1]
    new_ref[width - 2] = u
    _ln_silu_into(y_ref, acc_ref, lng_ref, lnb_ref)


def _conv_step_call(u, buf, w_dw, b_dw, ln_g, ln_b, *, j, tb):
    nb, ch = u.shape
    width = w_dw.shape[1]
    vec = lambda a: a.reshape(a.shape[0], 1, ch)
    vspec = pl.BlockSpec((None, 1, ch), lambda i: (j, 0, 0))
    return pl.pallas_call(
        functools.partial(_conv_step_kernel, width=width),
        grid=(nb // tb,),
        in_specs=[
            pl.BlockSpec((tb, ch), lambda i: (i, 0)),
            pl.BlockSpec((None, width - 1, tb, ch), lambda i: (j, 0, i, 0)),
            pl.BlockSpec((None, width, ch), lambda i: (j, 0, 0)),
            vspec, vspec, vspec,
        ],
        out_specs=[
            pl.BlockSpec((tb, ch), lambda i: (i, 0)),
            pl.BlockSpec((width - 1, tb, ch), lambda i: (0, i, 0)),
        ],
        out_shape=[jax.ShapeDtypeStruct((nb, ch), BF16), jax.ShapeDtypeStruct((width - 1, nb, ch), F32)],
        scratch_shapes=[pltpu.VMEM((tb, ch), F32)],
        compiler_params=_cparams("arbitrary"),
        name="conv_step",
    )(u, buf, w_dw, vec(b_dw), vec(ln_g), vec(ln_b))


def kernel(x_prompt, x_sample, state_hgrn, state_conv, c_prompt, c_sample, hgrn_lb, hgrn_w_in, hgrn_b_f, hgrn_g_norm, hgrn_w_out, conv_w_pw1, conv_b_pw1, conv_w_dw, conv_b_dw, conv_ln_g, conv_ln_b, conv_w_pw2, conv_b_pw2, ada_w, ada_b, norm_g, ffn_w_in, ffn_w_out, final_g):
    bp, t, d = x_prompt.shape
    bs = x_sample.shape[0]
    depth = ada_w.shape[0]
    heads = d // LANES
    time_major = lambda s: jnp.transpose(s, (0, 2, 1, 3))

    mod_s, mod_p = _ada_call(c_sample, c_prompt, ada_w, ada_b)
    mod_s = mod_s.reshape(depth, 1, bs, N_SUB * 3 * d)
    mod_p = mod_p.reshape(depth, bp, 1, N_SUB * 3 * d)
    norm_g4 = norm_g.reshape(depth, N_SUB, 1, d)
    conv_buf = time_major(state_conv)

    x, xs = x_prompt.reshape(bp * t, d), x_sample.reshape(bs, d)
    hgrn_p, hgrn_s, conv_p, conv_s = [], [], [], []
    for layer in range(depth):
        x, xs = _ffn_call(x, xs, mod_p, mod_s, norm_g4, ffn_w_in, ffn_w_out, final_g,
                          layer=layer, sub=0, j=0, final=False)
        j = layer // 2
        if layer % 2 == 0:
            outs = _hgrn_in_call(x, xs, mod_p, mod_s, norm_g4, hgrn_w_in, hgrn_b_f, hgrn_lb,
                                 layer=layer, j=j, lb_row=layer)
            y, s = _hgrn_scan_call(*outs[:5], hgrn_g_norm[j], batch=bp, heads=heads)
            ys, ss = _hgrn_step_call(*outs[5:], hgrn_g_norm[j], state_hgrn[j], heads=heads)
            hgrn_p.append(s)
            hgrn_s.append(ss)
            x, xs = _proj_call(y, ys, hgrn_w_out, None, x, xs, mod_p, mod_s, layer=layer, j=j)
        else:
            u, us = _pw1_call(x, xs, mod_p, mod_s, norm_g4, conv_w_pw1, conv_b_pw1, layer=layer, j=j)
            x, s = _conv_seq_call(u, conv_w_dw, conv_b_dw, conv_ln_g, conv_ln_b, x, mod_p, conv_w_pw2, conv_b_pw2,
                                  layer=layer, j=j, batch=bp)
            ys, ss = _conv_step_call(us, conv_buf, conv_w_dw, conv_b_dw, conv_ln_g, conv_ln_b, j=j, tb=4 * SUBLANES)
            conv_p.append(jnp.transpose(s, (1, 0, 2)))
            conv_s.append(ss)
            xs = _proj_rows_call(ys, conv_w_pw2, conv_b_pw2, xs, mod_s, layer=layer, j=j)
        x, xs = _ffn_call(x, xs, mod_p, mod_s, norm_g4, ffn_w_in, ffn_w_out, final_g,
                          layer=layer, sub=2, j=1, final=(layer == depth - 1))
    return (x.reshape(bp, t, d), xs.reshape(bs, 1, d), jnp.stack(hgrn_p), jnp.stack(hgrn_s),
            time_major(jnp.stack(conv_p)), time_major(jnp.stack(conv_s)))
```

```python
import functools

import jax
import jax.numpy as jnp
from jax import lax
from jax.experimental import pallas as pl
from jax.experimental.pallas import tpu as pltpu

F32 = jnp.float32
BF16 = jnp.bfloat16
EPS = 1e-6
N_SUB = 3
LANES = 128
SUBLANES = 8
VMEM_LIMIT_BYTES = 60 * 1024 * 1024

ROW_TILE = 1024
FFN_TILE = 256
IN_TILE = 256
PW1_TILE = 512
OUT_TILE = 2048
OUT_ROW_TILE = 512
SCAN_ROWS = 256
SCAN_SUB = 64
SCAN_SPAN_MAX = 120.0
SCAN_SLOW_ROWS = 16
CONV_PAD = 32
CONV_TILE = 256
CONV_SLAB = 256
NORM_ROWS = 16
NORM_UNROLL = 8


def _cparams(*sem):
    return pltpu.CompilerParams(dimension_semantics=sem, vmem_limit_bytes=VMEM_LIMIT_BYTES)


def _dot(a, b):
    return jnp.dot(a, b, preferred_element_type=F32)


def _dot_nt(a, b):
    return lax.dot_general(a, b, (((1,), (1,)), ((), ())), preferred_element_type=F32)


def _dot_tn(a, b):
    return lax.dot_general(a, b, (((0,), (0,)), ((), ())), preferred_element_type=F32)


def _sigmoid(x):
    return 1.0 / (1.0 + jnp.exp(-x))


def _silu(x):
    return x * _sigmoid(x)


def _rmsnorm(x, g):
    return x * lax.rsqrt(jnp.mean(x * x, axis=-1, keepdims=True) + EPS) * g


def _row_loop(n_rows, fn, in_place=False):
    rows = min(NORM_ROWS * NORM_UNROLL if in_place else NORM_ROWS, n_rows)
    steps = n_rows // rows
    if steps == 1:
        fn(pl.ds(0, rows))
        return

    def body(i, carry):
        fn(pl.ds(pl.multiple_of(i * rows, rows), rows))
        return carry

    lax.fori_loop(0, steps, body, 0, unroll=NORM_UNROLL if not in_place and steps % NORM_UNROLL == 0 else 1)


def _mod_rows(ref, rows):
    return ref[...] if ref.shape[0] == 1 else ref[rows, :]


def _modulate_into(h_ref, x_ref, g_ref, sh_ref, sc_ref):
    per_tile = sc_ref.shape[0] == 1
    gain_tile = g_ref[...] * (1.0 + sc_ref[...]) if per_tile else None

    def fn(rows):
        x = x_ref[rows, :]
        gain = gain_tile if per_tile else g_ref[...] * (1.0 + sc_ref[rows, :])
        y = x * lax.rsqrt(jnp.mean(x * x, axis=-1, keepdims=True) + EPS) * gain + _mod_rows(sh_ref, rows)
        h_ref[rows, :] = y.astype(BF16)

    _row_loop(x_ref.shape[0], fn)


def _once(shape, index_map):
    return pl.BlockSpec(shape, index_map, pipeline_mode=pl.Buffered(1))


def _mod_specs(mod, layer, secs, width, col_of, row_tiles):
    groups, r = mod.shape[1], mod.shape[2]
    tiles_per_group = max(row_tiles // groups, 1)
    ncol = mod.shape[3] // (N_SUB * 3) // width

    def spec(sec):
        def index_map(*idx):
            i, n = col_of(*idx)
            return (layer, i // tiles_per_group, 0, sec * ncol + n)
        if groups == 1 and ncol == 1:
            return _once((None, None, r, width), index_map)
        return pl.BlockSpec((None, None, r, width), index_map)

    return [spec(s) for s in secs]


def _ada_kernel(cs_ref, cp_ref, w_ref, b_ref, os_ref, op_ref):
    w = w_ref[...].astype(BF16)
    os_ref[...] = _dot(_silu(cs_ref[...]).astype(BF16), w) + b_ref[...]
    op_ref[...] = _dot(_silu(cp_ref[...]).astype(BF16), w) + b_ref[...]


def _ada_call(c_sample, c_prompt, ada_w, ada_b, tn=1024):
    depth, d, n = ada_w.shape
    rs, rp = c_sample.shape[0], c_prompt.shape[0]
    return pl.pallas_call(
        _ada_kernel,
        grid=(depth, n // tn),
        in_specs=[
            pl.BlockSpec((rs, d), lambda l, j: (0, 0)),
            pl.BlockSpec((rp, d), lambda l, j: (0, 0)),
            pl.BlockSpec((None, d, tn), lambda l, j: (l, 0, j)),
            pl.BlockSpec((None, 1, tn), lambda l, j: (l, 0, j)),
        ],
        out_specs=[
            pl.BlockSpec((None, rs, tn), lambda l, j: (l, 0, j)),
            pl.BlockSpec((None, rp, tn), lambda l, j: (l, 0, j)),
        ],
        out_shape=[jax.ShapeDtypeStruct((depth, rs, n), F32), jax.ShapeDtypeStruct((depth, rp, n), F32)],
        compiler_params=_cparams("arbitrary", "arbitrary"),
        name="adaln_params",
    )(c_sample, c_prompt, ada_w, ada_b.reshape(depth, 1, n))


def _ffn_kernel(x_ref, sh_ref, sc_ref, gt_ref, xs_ref, shs_ref, scs_ref, gts_ref, g_ref, wa_ref, wu_ref, wo_ref,
                *rest, final):
    fg_ref = rest[0] if final else None
    o_ref, os_ref, h_ref = rest[-3:]
    tm, ms = x_ref.shape[0], xs_ref.shape[0]
    first_tile = pl.program_id(0) == 0
    f = pl.program_id(1)
    last = pl.num_programs(1) - 1

    def start(x_ref, sh_ref, sc_ref, o_ref, row0):
        _modulate_into(h_ref.at[pl.ds(row0, x_ref.shape[0])], x_ref, g_ref, sh_ref, sc_ref)
        o_ref[...] = jnp.zeros_like(o_ref)

    def finish(x_ref, gt_ref, o_ref):
        def fn(rows):
            y = x_ref[rows, :] + 0.5 * _mod_rows(gt_ref, rows) * o_ref[rows, :]
            if fg_ref is not None:
                y = _rmsnorm(y, fg_ref[...])
            o_ref[rows, :] = y

        _row_loop(x_ref.shape[0], fn, in_place=True)

    def hidden(rows):
        h = h_ref[0:rows, :]
        return (_silu(_dot(h, wa_ref[...].astype(BF16))) * _dot(h, wu_ref[...].astype(BF16))).astype(BF16)

    @pl.when(f == 0)
    def _():
        start(x_ref, sh_ref, sc_ref, o_ref, 0)

    @pl.when((f == 0) & first_tile)
    def _():
        start(xs_ref, shs_ref, scs_ref, os_ref, tm)

    @pl.when(jnp.logical_not(first_tile))
    def _():
        o_ref[...] += _dot(hidden(tm), wo_ref[...].astype(BF16))

    @pl.when(first_tile)
    def _():
        act = hidden(tm + ms)
        wo = wo_ref[...].astype(BF16)
        o_ref[...] += _dot(act[0:tm], wo)
        os_ref[...] += _dot(act[tm:tm + ms], wo)

    @pl.when(f == last)
    def _():
        finish(x_ref, gt_ref, o_ref)

    @pl.when((f == last) & first_tile)
    def _():
        finish(xs_ref, gts_ref, os_ref)


def _ffn_call(x, xs, mod, mod_s, norm_g, w_in, w_out, final_g, *, layer, sub, j, final):
    m, d = x.shape
    ms = xs.shape[0]
    tm, tf = ROW_TILE, FFN_TILE
    nf = w_out.shape[2] // tf
    secs = [3 * sub, 3 * sub + 1, 3 * sub + 2]
    in_specs = (
        [pl.BlockSpec((tm, d), lambda i, f: (i, 0))]
        + _mod_specs(mod, layer, secs, d, lambda i, f: (i, 0), m // tm)
        + [_once((ms, d), lambda i, f: (0, 0))]
        + _mod_specs(mod_s, layer, secs, d, lambda i, f: (0, 0), 1)
        + [
            pl.BlockSpec((None, None, 1, d), lambda i, f: (layer, sub, 0, 0)),
            pl.BlockSpec((None, None, d, tf), lambda i, f: (layer, j, 0, f)),
            pl.BlockSpec((None, None, d, tf), lambda i, f: (layer, j, 0, nf + f)),
            pl.BlockSpec((None, None, tf, d), lambda i, f: (layer, j, f, 0)),
        ]
    )
    args = [x, mod, mod, mod, xs, mod_s, mod_s, mod_s, norm_g, w_in, w_in, w_out]
    if final:
        in_specs.append(pl.BlockSpec((1, d), lambda i, f: (0, 0)))
        args.append(final_g.reshape(1, d))
    return pl.pallas_call(
        functools.partial(_ffn_kernel, final=final),
        grid=(m // tm, nf),
        in_specs=in_specs,
        out_specs=[pl.BlockSpec((tm, d), lambda i, f: (i, 0)), _once((ms, d), lambda i, f: (0, 0))],
        out_shape=[jax.ShapeDtypeStruct((m, d), F32), jax.ShapeDtypeStruct((ms, d), F32)],
        scratch_shapes=[pltpu.VMEM((tm + ms, d), BF16)],
        compiler_params=_cparams("arbitrary", "arbitrary"),
        name="ffn",
    )(*args)


def _hgrn_in_rows(x_ref, sh_ref, sc_ref, g_ref, bf_ref, h_ref, outs, ws, lb):
    q_ref, k_ref, v_ref, lf_ref, sg_ref = outs
    wq_ref, wf_ref, wi_ref, wg_ref = ws

    @pl.when(pl.program_id(1) == 0)
    def _():
        _modulate_into(h_ref, x_ref, g_ref, sh_ref, sc_ref)

    h = h_ref[...]
    q_ref[...] = _dot(h, wq_ref[...].astype(BF16))
    z = _dot(h, wf_ref[...].astype(BF16)) + bf_ref[...]
    lf_ref[...] = jnp.log(lb + (1.0 - lb) * _sigmoid(z))
    k_ref[...] = (1.0 - lb) * _sigmoid(-z)
    v_ref[...] = _dot(h, wi_ref[...].astype(BF16)).astype(BF16)
    sg_ref[...] = _silu(_dot(h, wg_ref[...].astype(BF16)))


def _hgrn_in_kernel(x_ref, sh_ref, sc_ref, xs_ref, shs_ref, scs_ref, g_ref, wq_ref, wf_ref, wi_ref, wg_ref,
                    bf_ref, lb_ref, *rest, lb_row):
    outs, outs_s, (h_ref, hs_ref) = rest[0:5], rest[5:10], rest[10:12]
    raw = [lb_ref[i:i + 1, :] for i in range(lb_ref.shape[0])]
    top = functools.reduce(jnp.maximum, raw)
    e = [jnp.exp(r - top) for r in raw]
    lb = sum(e[:lb_row + 1]) / sum(e)
    ws = (wq_ref, wf_ref, wi_ref, wg_ref)
    _hgrn_in_rows(x_ref, sh_ref, sc_ref, g_ref, bf_ref, h_ref, outs, ws, lb)

    @pl.when(pl.program_id(0) == 0)
    def _():
        _hgrn_in_rows(xs_ref, shs_ref, scs_ref, g_ref, bf_ref, hs_ref, outs_s, ws, lb)


def _side_col(nn):
    return lambda i, n: (0, jnp.where(i == 0, n, nn - 1))


def _hgrn_in_call(x, xs, mod, mod_s, norm_g, w_in, b_f, lb_raw, *, layer, j, lb_row):
    m, d = x.shape
    ms = xs.shape[0]
    tm, tn = ROW_TILE, IN_TILE
    nn = d // tn
    sub = 1
    secs = [3 * sub, 3 * sub + 1]

    def wspec(sec):
        return pl.BlockSpec((None, d, tn), lambda i, n: (j, 0, sec * nn + n))

    def out_shapes(rows):
        return [jax.ShapeDtypeStruct((rows, d), dt) for dt in (F32, F32, BF16, F32, F32)]

    return pl.pallas_call(
        functools.partial(_hgrn_in_kernel, lb_row=lb_row),
        grid=(m // tm, nn),
        in_specs=(
            [pl.BlockSpec((tm, d), lambda i, n: (i, 0))]
            + _mod_specs(mod, layer, secs, d, lambda i, n: (i, 0), m // tm)
            + [_once((ms, d), lambda i, n: (0, 0))]
            + _mod_specs(mod_s, layer, secs, d, lambda i, n: (0, 0), 1)
            + [
                pl.BlockSpec((None, None, 1, d), lambda i, n: (layer, sub, 0, 0)),
                wspec(0), wspec(1), wspec(2), wspec(3),
                pl.BlockSpec((None, 1, tn), lambda i, n: (j, 0, n)),
                pl.BlockSpec((lb_raw.shape[0], tn), lambda i, n: (0, n)),
            ]
        ),
        out_specs=[pl.BlockSpec((tm, tn), lambda i, n: (i, n))] * 5 + [pl.BlockSpec((ms, tn), _side_col(nn))] * 5,
        out_shape=out_shapes(m) + out_shapes(ms),
        scratch_shapes=[pltpu.VMEM((tm, d), BF16), pltpu.VMEM((ms, d), BF16)],
        compiler_params=_cparams("arbitrary", "arbitrary"),
        name="hgrn_in",
    )(x, mod, mod, xs, mod_s, mod_s, norm_g, w_in, w_in, w_in, w_in, b_f.reshape(b_f.shape[0], 1, d), lb_raw)


def _hgrn_scan_kernel(q_ref, k_ref, v_ref, lf_ref, sg_ref, gn_ref, y_ref, sfin_ref, st_ref, b_ref, o_ref, *,
                      heads, span_max):
    c = pl.program_id(1)
    rows, d = q_ref.shape
    dk = d // heads
    sub = min(SCAN_SUB, rows)
    n_sub = rows // sub
    n_levels = n_sub.bit_length() - 1
    sub_shift = sub.bit_length() - 1

    @pl.when(c == 0)
    def _():
        st_ref[...] = jnp.zeros_like(st_ref)

    same_head = ((lax.broadcasted_iota(jnp.int32, (2 * dk, 2 * dk), 0) < dk)
                 == (lax.broadcasted_iota(jnp.int32, (2 * dk, 2 * dk), 1) < dk))
    row = lax.broadcasted_iota(jnp.int32, (rows, rows), 0)
    col = lax.broadcasted_iota(jnp.int32, (rows, rows), 1)
    tri = jnp.where(col <= row, 1.0, 0.0).astype(BF16)
    lf = lf_ref[...]
    p1 = lf.astype(BF16)
    r1 = lf - p1.astype(F32)
    p2 = r1.astype(BF16)
    p3 = (r1 - p2.astype(F32)).astype(BF16)
    b_ref[...] = _dot(tri, p1) + _dot(tri, p2) + _dot(tri, p3)

    def first_row(j):
        return slice(j * sub, j * sub + 1)

    def last_row(j):
        return slice((j + 1) * sub - 1, (j + 1) * sub)

    spans = [b_ref[first_row(j), :] - b_ref[last_row(j), :] for j in range(n_sub)]
    safe = jnp.max(functools.reduce(jnp.maximum, spans)) <= span_max

    def gated_out(o, sl):
        return (_rmsnorm(o, gn_ref[:, sl]) * sg_ref[:, sl]).astype(BF16)

    @pl.when(safe)
    def _():
        srl = lax.shift_right_logical
        level = jnp.where((srl(row, sub_shift) == srl(col, sub_shift)) & (col <= row), 0, -1)
        later_half = []
        row_d = lax.broadcasted_iota(jnp.int32, (rows, dk), 0)
        for l in range(1, n_levels + 1):
            s = sub_shift + l - 1
            hit = (srl(row, s + 1) == srl(col, s + 1)) & ((srl(row, s) & 1) == 1) & ((srl(col, s) & 1) == 0)
            level = jnp.where(hit, l, level)
            later_half.append((srl(row_d, s) & 1) == 1)

        def rows_of(pieces, n):
            return jnp.concatenate([jnp.broadcast_to(p, (n, dk)) for p in pieces], axis=0)

        def scores(sl):
            b = b_ref[:, sl]
            q = q_ref[:, sl]
            k = k_ref[:, sl]
            rel = b - rows_of([0.5 * (b_ref[first_row(j), sl] + b_ref[last_row(j), sl]) for j in range(n_sub)], sub)
            att = jnp.where(level == 0,
                            _dot_nt((q * jnp.exp(rel)).astype(BF16), (k * jnp.exp(-rel)).astype(BF16)), 0.0)
            for l in range(1, n_levels + 1):
                m = sub << (l - 1)
                rel = b - rows_of([b_ref[p * 2 * m + m:p * 2 * m + m + 1, sl] for p in range(rows // (2 * m))], 2 * m)
                e = jnp.exp(jnp.minimum(jnp.where(later_half[l - 1], rel, -rel), 0.0))
                att = jnp.where(level == l, _dot_nt((q * e).astype(BF16), (k * e).astype(BF16)), att)
            return att.astype(BF16)

        first_head = lax.broadcasted_iota(jnp.int32, (rows, 2 * dk), 1) < dk
        for pair in range(heads // 2):
            sl0 = slice(2 * pair * dk, (2 * pair + 1) * dk)
            sl1 = slice((2 * pair + 1) * dk, (2 * pair + 2) * dk)
            both = slice(2 * pair * dk, (2 * pair + 2) * dk)
            att = jnp.concatenate([scores(sl0), scores(sl1)], axis=1)
            v = v_ref[:, both]
            zero = jnp.zeros_like(v)
            v_split = jnp.concatenate([jnp.where(first_head, v, zero), jnp.where(first_head, zero, v)], axis=0)
            b = b_ref[:, both]
            b_last = b_ref[rows - 1:rows, both]
            q_st = (q_ref[:, both] * jnp.exp(b)).astype(BF16)
            k_st = (k_ref[:, both] * jnp.exp(b_last - b)).astype(BF16)
            st = st_ref[pair]
            o = _dot(att, v_split) + _dot_nt(q_st, st.astype(BF16))
            st_ref[pair] = st * jnp.exp(b_last) + jnp.where(same_head, _dot_tn(v, k_st), 0.0)
            y_ref[:, sl0] = gated_out(o[:, 0:dk], sl0)
            y_ref[:, sl1] = gated_out(o[:, dk:2 * dk], sl1)

    @pl.when(jnp.logical_not(safe))
    def _():
        grp_rows = min(SCAN_SLOW_ROWS, rows)
        grp_shift = grp_rows.bit_length() - 1
        pick = lax.broadcasted_iota(jnp.int32, (grp_rows, 2 * dk), 0)
        o_ref[...] = jnp.zeros_like(o_ref)

        def token(t, carry):
            r0 = pl.multiple_of(lax.shift_left(lax.shift_right_logical(t, grp_shift), grp_shift), grp_rows)
            sel = pick == t - r0
            grp = pl.ds(r0, grp_rows)
            for pair in range(heads // 2):
                sl = slice(2 * pair * dk, (2 * pair + 2) * dk)
                f = jnp.sum(jnp.where(sel, jnp.exp(lf_ref[grp, sl]), 0.0), axis=0, keepdims=True)
                q1 = jnp.where(sel, q_ref[grp, sl], 0.0).astype(BF16)
                k1 = jnp.where(sel, k_ref[grp, sl], 0.0).astype(BF16)
                v1 = jnp.where(sel, v_ref[grp, sl].astype(F32), 0.0).astype(BF16)
                st = st_ref[pair] * f + jnp.where(same_head, _dot_tn(v1, k1), 0.0)
                st_ref[pair] = st
                o_ref[grp, sl] = jnp.where(sel, _dot_nt(q1, st.astype(BF16)), o_ref[grp, sl])
            return carry

        lax.fori_loop(0, rows, token, 0)
        for h in range(heads):
            sl = slice(h * dk, (h + 1) * dk)
            y_ref[:, sl] = gated_out(o_ref[:, sl], sl)

    @pl.when(c == pl.num_programs(1) - 1)
    def _():
        for h in range(heads):
            lo = (h % 2) * dk
            sfin_ref[h] = st_ref[h // 2, lo:lo + dk, lo:lo + dk].T


def _hgrn_scan_call(q, k, v, lf, sg, g_norm, *, batch, heads, span_max=SCAN_SPAN_MAX):
    m, d = q.shape
    t = m // batch
    rows = min(SCAN_ROWS, t)
    nc = t // rows
    dk = d // heads
    spec = pl.BlockSpec((rows, d), lambda b, c: (b * nc + c, 0))
    return pl.pallas_call(
        functools.partial(_hgrn_scan_kernel, heads=heads, span_max=span_max),
        grid=(batch, nc),
        in_specs=[spec, spec, spec, spec, spec, pl.BlockSpec((1, d), lambda b, c: (0, 0))],
        out_specs=[spec, pl.BlockSpec((None, heads, dk, dk), lambda b, c: (b, 0, 0, 0))],
        out_shape=[jax.ShapeDtypeStruct((m, d), BF16), jax.ShapeDtypeStruct((batch, heads, dk, dk), F32)],
        scratch_shapes=[pltpu.VMEM((heads // 2, 2 * dk, 2 * dk), F32), pltpu.VMEM((rows, d), F32),
                        pltpu.VMEM((rows, d), F32)],
        compiler_params=_cparams("arbitrary", "arbitrary"),
        name="hgrn_scan",
    )(q, k, v, lf, sg, g_norm.reshape(1, d))


def _hgrn_step_kernel(q_ref, k_ref, v_ref, lf_ref, sg_ref, gn_ref, s0_ref, y_ref, s1_ref, o_ref):
    nb = q_ref.shape[0]
    q_t = q_ref[...].T
    k_t = k_ref[...].T
    f_t = jnp.exp(lf_ref[...]).T
    v = v_ref[...].astype(F32)
    for b in range(nb):
        s1 = s0_ref[b] * f_t[:, b:b + 1] + k_t[:, b:b + 1] * v[b:b + 1, :]
        s1_ref[b] = s1
        o_ref[b:b + 1, :] = jnp.sum(s1 * q_t[:, b:b + 1], axis=0, keepdims=True)
    y = _rmsnorm(o_ref[...], gn_ref[...]) * sg_ref[...]
    y_ref[...] = y.astype(BF16)


def _hgrn_step_call(q, k, v, lf, sg, g_norm, s0, *, heads):
    nb, d = q.shape
    dk = d // heads
    spec = pl.BlockSpec((nb, dk), lambda h: (0, h))
    sspec = pl.BlockSpec((nb, None, dk, dk), lambda h: (0, h, 0, 0))
    return pl.pallas_call(
        _hgrn_step_kernel,
        grid=(heads,),
        in_specs=[spec, spec, spec, spec, spec, pl.BlockSpec((1, dk), lambda h: (0, h)), sspec],
        out_specs=[spec, sspec],
        out_shape=[jax.ShapeDtypeStruct((nb, d), BF16), jax.ShapeDtypeStruct(s0.shape, F32)],
        scratch_shapes=[pltpu.VMEM((nb, dk), F32)],
        compiler_params=_cparams("arbitrary"),
        name="hgrn_step",
    )(q, k, v, lf, sg, g_norm.reshape(1, d), s0)


def _proj_kernel(y_ref, x_ref, gt_ref, ys_ref, xs_ref, gts_ref, w_ref, *rest, has_bias):
    b_ref = rest[0] if has_bias else None
    o_ref, os_ref = rest[-2:]
    def rows(y_ref, x_ref, gt_ref, o_ref):
        out = _dot(y_ref[...], w_ref[...].astype(BF16))
        if has_bias:
            out = out + b_ref[...]
        o_ref[...] = x_ref[...] + gt_ref[...] * out

    rows(y_ref, x_ref, gt_ref, o_ref)

    @pl.when(pl.program_id(1) == 0)
    def _():
        rows(ys_ref, xs_ref, gts_ref, os_ref)


def _proj_call(y, ys, w, bias, x, xs, mod, mod_s, *, layer, j):
    m, kdim = y.shape
    ms = ys.shape[0]
    d = x.shape[1]
    tm, tn = OUT_ROW_TILE, min(OUT_TILE, d)
    gate = [3 * 1 + 2]
    has_bias = bias is not None
    wspec = _once if tn == d else pl.BlockSpec
    in_specs = (
        [pl.BlockSpec((tm, kdim), lambda n, i: (i, 0)), pl.BlockSpec((tm, tn), lambda n, i: (i, n))]
        + _mod_specs(mod, layer, gate, tn, lambda n, i: (i, n), m // tm)
        + [_once((ms, kdim), lambda n, i: (0, 0)), pl.BlockSpec((ms, tn), lambda n, i: (0, n))]
        + _mod_specs(mod_s, layer, gate, tn, lambda n, i: (0, n), 1)
        + [wspec((None, kdim, tn), lambda n, i: (j, 0, n))]
    )
    args = [y, x, mod, ys, xs, mod_s, w]
    if has_bias:
        in_specs.append(pl.BlockSpec((None, 1, tn), lambda n, i: (j, 0, n)))
        args.append(bias.reshape(bias.shape[0], 1, d))
    return pl.pallas_call(
        functools.partial(_proj_kernel, has_bias=has_bias),
        grid=(d // tn, m // tm),
        in_specs=in_specs,
        out_specs=[pl.BlockSpec((tm, tn), lambda n, i: (i, n)), pl.BlockSpec((ms, tn), lambda n, i: (0, n))],
        out_shape=[jax.ShapeDtypeStruct((m, d), F32), jax.ShapeDtypeStruct((ms, d), F32)],
        compiler_params=_cparams("arbitrary", "arbitrary"),
        name="proj_residual",
    )(*args)


def _proj_rows_kernel(y_ref, x_ref, gt_ref, w_ref, b_ref, o_ref):
    o_ref[...] = x_ref[...] + gt_ref[...] * (_dot(y_ref[...], w_ref[...].astype(BF16)) + b_ref[...])


def _proj_rows_call(y, w, bias, x, mod, *, layer, j):
    ms, kdim = y.shape
    d = x.shape[1]
    tn = IN_TILE
    return pl.pallas_call(
        _proj_rows_kernel,
        grid=(d // tn,),
        in_specs=(
            [_once((ms, kdim), lambda n: (0, 0)), pl.BlockSpec((ms, tn), lambda n: (0, n))]
            + _mod_specs(mod, layer, [3 * 1 + 2], tn, lambda n: (0, n), 1)
            + [pl.BlockSpec((None, kdim, tn), lambda n: (j, 0, n)), pl.BlockSpec((None, 1, tn), lambda n: (j, 0, n))]
        ),
        out_specs=pl.BlockSpec((ms, tn), lambda n: (0, n)),
        out_shape=jax.ShapeDtypeStruct((ms, d), F32),
        compiler_params=_cparams("arbitrary"),
        name="proj_rows",
    )(y, x, mod, w, bias.reshape(bias.shape[0], 1, d))


def _pw1_kernel(x_ref, sh_ref, sc_ref, xs_ref, shs_ref, scs_ref, g_ref, wa_ref, wg_ref, ba_ref, bg_ref,
                u_ref, us_ref, h_ref, hs_ref):
    def rows(x_ref, sh_ref, sc_ref, u_ref, h_ref):
        @pl.when(pl.program_id(1) == 0)
        def _():
            _modulate_into(h_ref, x_ref, g_ref, sh_ref, sc_ref)

        h = h_ref[...]
        a = _dot(h, wa_ref[...].astype(BF16)) + ba_ref[...]
        u_ref[...] = a * _sigmoid(_dot(h, wg_ref[...].astype(BF16)) + bg_ref[...])

    rows(x_ref, sh_ref, sc_ref, u_ref, h_ref)

    @pl.when(pl.program_id(0) == 0)
    def _():
        rows(xs_ref, shs_ref, scs_ref, us_ref, hs_ref)


def _pw1_call(x, xs, mod, mod_s, norm_g, w, bias, *, layer, j):
    m, d = x.shape
    ms = xs.shape[0]
    tm, tn = ROW_TILE, PW1_TILE
    ch = w.shape[2] // 2
    nn = ch // tn
    sub = 1
    secs = [3 * sub, 3 * sub + 1]

    def wspec(sec):
        return pl.BlockSpec((None, d, tn), lambda i, n: (j, 0, sec * nn + n))

    def bspec(sec):
        return pl.BlockSpec((None, 1, tn), lambda i, n: (j, 0, sec * nn + n))

    bias3 = bias.reshape(bias.shape[0], 1, 2 * ch)
    return pl.pallas_call(
        _pw1_kernel,
        grid=(m // tm, nn),
        in_specs=(
            [pl.BlockSpec((tm, d), lambda i, n: (i, 0))]
            + _mod_specs(mod, layer, secs, d, lambda i, n: (i, 0), m // tm)
            + [_once((ms, d), lambda i, n: (0, 0))]
            + _mod_specs(mod_s, layer, secs, d, lambda i, n: (0, 0), 1)
            + [
                pl.BlockSpec((None, None, 1, d), lambda i, n: (layer, sub, 0, 0)),
                wspec(0), wspec(1), bspec(0), bspec(1),
            ]
        ),
        out_specs=[pl.BlockSpec((tm, tn), lambda i, n: (i, n)), pl.BlockSpec((ms, tn), _side_col(nn))],
        out_shape=[jax.ShapeDtypeStruct((m, ch), F32), jax.ShapeDtypeStruct((ms, ch), F32)],
        scratch_shapes=[pltpu.VMEM((tm, d), BF16), pltpu.VMEM((ms, d), BF16)],
        compiler_params=_cparams("arbitrary", "arbitrary"),
        name="conv_pw1",
    )(x, mod, mod, xs, mod_s, mod_s, norm_g, w, w, bias3, bias3)


def _ln_silu_into(y_ref, acc_ref, lng_ref, lnb_ref):
    def fn(rows):
        a = acc_ref[rows, :]
        mu = jnp.mean(a, axis=-1, keepdims=True)
        xc = a - mu
        yn = xc * lax.rsqrt(jnp.mean(xc * xc, axis=-1, keepdims=True) + EPS) * lng_ref[...] + lnb_ref[...]
        y_ref[rows, :] = _silu(yn).astype(BF16)

    _row_loop(acc_ref.shape[0], fn)


def _conv_seq_kernel(u_ref, w_ref, bdw_ref, lng_ref, lnb_ref, x_ref, gt_ref, w2_ref, b2_ref, o_ref, tail_ref,
                     win_ref, acc_ref, y_ref, w2b_ref, *, width):
    t = pl.program_id(1)
    nt = pl.num_programs(1) - 1
    tt, ch = u_ref.shape
    n_lt = ch // LANES
    off = CONV_PAD - (width - 1)
    blk = SUBLANES * SUBLANES
    n_slab, _, slab = w2b_ref.shape
    lt_per_slab = n_lt // n_slab

    @pl.when((pl.program_id(0) == 0) & (t == 0))
    def _():
        for s in range(n_slab):
            w2b_ref[s] = w2_ref[:, s * slab:(s + 1) * slab].astype(BF16)

    @pl.when(t == 0)
    def _():
        win_ref[:, 0:CONV_PAD, :] = jnp.zeros((n_lt, CONV_PAD, LANES), F32)

    @pl.when(t < nt)
    def _():
        for l in range(n_lt):
            win_ref[l, CONV_PAD:CONV_PAD + tt, :] = u_ref[:, l * LANES:(l + 1) * LANES]

    def project(s):
        cols = slice(s * slab, (s + 1) * slab)
        out = _dot(y_ref[(t + 1) % 2], w2b_ref[s]) + b2_ref[:, cols]
        o_ref[:, cols] = x_ref[:, cols] + gt_ref[:, cols] * out

    def conv_block(l, r0):
        strided = [win_ref[l, pl.ds(r0 + r, SUBLANES, stride=SUBLANES), :]
                   for r in range(off, off + width - 1 + SUBLANES)]
        accs = [[jnp.broadcast_to(bdw_ref[l], (SUBLANES, LANES)), None] for _ in range(SUBLANES)]
        for tap in range(width):
            w_tap = jnp.broadcast_to(w_ref[l, tap:tap + 1, :], (SUBLANES, LANES))
            for t0 in range(SUBLANES):
                term = strided[t0 + tap] * w_tap
                prev = accs[t0][tap % 2]
                accs[t0][tap % 2] = term if prev is None else prev + term
        for t0 in range(SUBLANES):
            acc_ref[l, pl.ds(r0 + t0, SUBLANES, stride=SUBLANES), :] = accs[t0][0] + accs[t0][1]

    def conv_lane_tiles(lts):
        for l in lts:
            for r0 in range(0, tt, blk):
                conv_block(l, r0)

    @pl.when(t == 0)
    def _():
        def body(i, carry):
            conv_block(i // (tt // blk), pl.multiple_of((i % (tt // blk)) * blk, blk))
            return carry

        lax.fori_loop(0, n_lt * (tt // blk), body, 0, unroll=2)

    @pl.when((t > 0) & (t < nt))
    def _():
        for s in range(n_slab):
            project(s)
            conv_lane_tiles(range(s * lt_per_slab, (s + 1) * lt_per_slab))

    @pl.when(t == nt)
    def _():
        for s in range(n_slab):
            project(s)

    @pl.when(t < nt)
    def _():
        for l in range(n_lt):
            tail_ref[:, l * LANES:(l + 1) * LANES] = win_ref[l, CONV_PAD + tt - (width - 1):CONV_PAD + tt, :]
        win_ref[:, 0:CONV_PAD, :] = win_ref[:, tt:tt + CONV_PAD, :]
        y_cur = y_ref.at[t % 2]

        def fn(rows):
            a = jnp.concatenate([acc_ref[l, rows, :] for l in range(n_lt)], axis=1)
            mu = jnp.mean(a, axis=-1, keepdims=True)
            xc = a - mu
            yn = xc * lax.rsqrt(jnp.mean(xc * xc, axis=-1, keepdims=True) + EPS) * lng_ref[...] + lnb_ref[...]
            y_cur[rows, :] = _silu(yn).astype(BF16)

        _row_loop(tt, fn)


def _conv_seq_call(u, w_dw, b_dw, ln_g, ln_b, x, mod, w2, b2, *, layer, j, batch):
    m, ch = u.shape
    d = x.shape[1]
    t = m // batch
    tt = min(CONV_TILE, t)
    nt = t // tt
    width = w_dw.shape[1]
    n_lt = ch // LANES
    n_slab = d // CONV_SLAB
    vec = lambda a: a.reshape(a.shape[0], 1, a.shape[-1])
    vspec = pl.BlockSpec((None, 1, ch), lambda b, i: (j, 0, 0))
    w_lt = jnp.transpose(w_dw[j].reshape(width, n_lt, LANES), (1, 0, 2))
    b_lt = b_dw[j].reshape(n_lt, 1, LANES)
    conv_tile = lambda b, i: (b * nt + jnp.minimum(i, nt - 1), 0)
    proj_tile = lambda b, i: (b * nt + jnp.maximum(i - 1, 0), 0)
    return pl.pallas_call(
        functools.partial(_conv_seq_kernel, width=width),
        grid=(batch, nt + 1),
        in_specs=[
            pl.BlockSpec((tt, ch), conv_tile),
            pl.BlockSpec((n_lt, width, LANES), lambda b, i: (0, 0, 0)),
            pl.BlockSpec((n_lt, 1, LANES), lambda b, i: (0, 0, 0)),
            vspec, vspec,
            pl.BlockSpec((tt, d), proj_tile),
            pl.BlockSpec((None, None, 1, d), lambda b, i: (layer, b, 0, 3 * 1 + 2)),
            _once((None, ch, d), lambda b, i: (j, 0, 0)),
            pl.BlockSpec((None, 1, d), lambda b, i: (j, 0, 0)),
        ],
        out_specs=[
            pl.BlockSpec((tt, d), proj_tile),
            pl.BlockSpec((None, width - 1, ch), lambda b, i: (b, 0, 0)),
        ],
        out_shape=[jax.ShapeDtypeStruct((m, d), F32), jax.ShapeDtypeStruct((batch, width - 1, ch), F32)],
        scratch_shapes=[
            pltpu.VMEM((n_lt, CONV_PAD + tt, LANES), F32),
            pltpu.VMEM((n_lt, tt, LANES), F32),
            pltpu.VMEM((2, tt, ch), BF16),
            pltpu.VMEM((n_slab, ch, CONV_SLAB), BF16),
        ],
        compiler_params=_cparams("arbitrary", "arbitrary"),
        name="conv_seq",
    )(u, w_lt, b_lt, vec(ln_g), vec(ln_b), x, mod, w2, vec(b2))


def _conv_step_kernel(u_ref, buf_ref, w_ref, bdw_ref, lng_ref, lnb_ref, y_ref, new_ref, acc_ref, *, width):
    u = u_ref[...]
    acc = u * w_ref[width - 1:width, :] + bdw_ref[...]
    for tap in range(width - 1):
        acc = acc + buf_ref[tap] * w_ref[tap:tap + 1, :]
    acc_ref[...] = acc
    for tap in range(width - 2):
        new_ref[tap] = buf_ref[tap + 1]
    new_ref[width - 2] = u
    _ln_silu_into(y_ref, acc_ref, lng_ref, lnb_ref)


def _conv_step_call(u, buf, w_dw, b_dw, ln_g, ln_b, *, j, tb):
    nb, ch = u.shape
    width = w_dw.shape[1]
    vec = lambda a: a.reshape(a.shape[0], 1, ch)
    vspec = pl.BlockSpec((None, 1, ch), lambda i: (j, 0, 0))
    return pl.pallas_call(
        functools.partial(_conv_step_kernel, width=width),
        grid=(nb // tb,),
        in_specs=[
            pl.BlockSpec((tb, ch), lambda i: (i, 0)),
            pl.BlockSpec((None, width - 1, tb, ch), lambda i: (j, 0, i, 0)),
            pl.BlockSpec((None, width, ch), lambda i: (j, 0, 0)),
            vspec, vspec, vspec,
        ],
        out_specs=[
            pl.BlockSpec((tb, ch), lambda i: (i, 0)),
            pl.BlockSpec((width - 1, tb, ch), lambda i: (0, i, 0)),
        ],
        out_shape=[jax.ShapeDtypeStruct((nb, ch), BF16), jax.ShapeDtypeStruct((width - 1, nb, ch), F32)],
        scratch_shapes=[pltpu.VMEM((tb, ch), F32)],
        compiler_params=_cparams("arbitrary"),
        name="conv_step",
    )(u, buf, w_dw, vec(b_dw), vec(ln_g), vec(ln_b))


def kernel(x_prompt, x_sample, state_hgrn, state_conv, c_prompt, c_sample, hgrn_lb, hgrn_w_in, hgrn_b_f, hgrn_g_norm, hgrn_w_out, conv_w_pw1, conv_b_pw1, conv_w_dw, conv_b_dw, conv_ln_g, conv_ln_b, conv_w_pw2, conv_b_pw2, ada_w, ada_b, norm_g, ffn_w_in, ffn_w_out, final_g):
    bp, t, d = x_prompt.shape
    bs = x_sample.shape[0]
    depth = ada_w.shape[0]
    heads = d // LANES
    time_major = lambda s: jnp.transpose(s, (0, 2, 1, 3))

    mod_s, mod_p = _ada_call(c_sample, c_prompt, ada_w, ada_b)
    mod_s = mod_s.reshape(depth, 1, bs, N_SUB * 3 * d)
    mod_p = mod_p.reshape(depth, bp, 1, N_SUB * 3 * d)
    norm_g4 = norm_g.reshape(depth, N_SUB, 1, d)
    conv_buf = time_major(state_conv)

    x, xs = x_prompt.reshape(bp * t, d), x_sample.reshape(bs, d)
    hgrn_p, hgrn_s, conv_p, conv_s = [], [], [], []
    for layer in range(depth):
        x, xs = _ffn_call(x, xs, mod_p, mod_s, norm_g4, ffn_w_in, ffn_w_out, final_g,
                          layer=layer, sub=0, j=0, final=False)
        j = layer // 2
        if layer % 2 == 0:
            outs = _hgrn_in_call(x, xs, mod_p, mod_s, norm_g4, hgrn_w_in, hgrn_b_f, hgrn_lb,
                                 layer=layer, j=j, lb_row=layer)
            y, s = _hgrn_scan_call(*outs[:5], hgrn_g_norm[j], batch=bp, heads=heads)
            ys, ss = _hgrn_step_call(*outs[5:], hgrn_g_norm[j], state_hgrn[j], heads=heads)
            hgrn_p.append(s)
            hgrn_s.append(ss)
            x, xs = _proj_call(y, ys, hgrn_w_out, None, x, xs, mod_p, mod_s, layer=layer, j=j)
        else:
            u, us = _pw1_call(x, xs, mod_p, mod_s, norm_g4, conv_w_pw1, conv_b_pw1, layer=layer, j=j)
            x, s = _conv_seq_call(u, conv_w_dw, conv_b_dw, conv_ln_g, conv_ln_b, x, mod_p, conv_w_pw2, conv_b_pw2,
                                  layer=layer, j=j, batch=bp)
            ys, ss = _conv_step_call(us, conv_buf, conv_w_dw, conv_b_dw, conv_ln_g, conv_ln_b, j=j, tb=4 * SUBLANES)
            conv_p.append(jnp.transpose(s, (1, 0, 2)))
            conv_s.append(ss)
            xs = _proj_rows_call(ys, conv_w_pw2, conv_b_pw2, xs, mod_s, layer=layer, j=j)
        x, xs = _ffn_call(x, xs, mod_p, mod_s, norm_g4, ffn_w_in, ffn_w_out, final_g,
                          layer=layer, sub=2, j=1, final=(layer == depth - 1))
    return (x.reshape(bp, t, d), xs.reshape(bs, 1, d), jnp.stack(hgrn_p), jnp.stack(hgrn_s),
            time_major(jnp.stack(conv_p)), time_major(jnp.stack(conv_s)))
```

```python
import functools

import jax
import jax.numpy as jnp
from jax import lax
from jax.experimental import pallas as pl
from jax.experimental.pallas import tpu as pltpu

F32 = jnp.float32
BF16 = jnp.bfloat16
EPS = 1e-6
N_SUB = 3
LANES = 128
SUBLANES = 8
VMEM_LIMIT_BYTES = 60 * 1024 * 1024

ADA_TILE = 2048
ROW_TILE = 1024
FFN_TILE = 256
IN_TILE = 256
PW1_TILE = 512
OUT_TILE = 2048
OUT_ROW_TILE = 512
SCAN_ROWS = 256
SCAN_SUB = 64
SCAN_SPAN_MAX = 120.0
SCAN_SLOW_ROWS = 16
CONV_PAD = 32
CONV_TILE = 256
CONV_SLAB = 256
NORM_ROWS = 16
NORM_UNROLL = 8


def _cparams(*sem):
    return pltpu.CompilerParams(dimension_semantics=sem, vmem_limit_bytes=VMEM_LIMIT_BYTES)


def _dot(a, b):
    return jnp.dot(a, b, preferred_element_type=F32)


def _dot_nt(a, b):
    return lax.dot_general(a, b, (((1,), (1,)), ((), ())), preferred_element_type=F32)


def _dot_tn(a, b):
    return lax.dot_general(a, b, (((0,), (0,)), ((), ())), preferred_element_type=F32)


def _sigmoid(x):
    return 1.0 / (1.0 + jnp.exp(-x))


def _silu(x):
    return x * _sigmoid(x)


def _rmsnorm(x, g):
    return x * lax.rsqrt(jnp.mean(x * x, axis=-1, keepdims=True) + EPS) * g


def _row_loop(n_rows, fn, in_place=False):
    rows = min(NORM_ROWS * NORM_UNROLL if in_place else NORM_ROWS, n_rows)
    steps = n_rows // rows
    if steps == 1:
        fn(pl.ds(0, rows))
        return

    def body(i, carry):
        fn(pl.ds(pl.multiple_of(i * rows, rows), rows))
        return carry

    lax.fori_loop(0, steps, body, 0, unroll=NORM_UNROLL if not in_place and steps % NORM_UNROLL == 0 else 1)


def _mod_rows(ref, rows):
    return ref[...] if ref.shape[0] == 1 else ref[rows, :]


def _modulate_into(h_ref, x_ref, g_ref, sh_ref, sc_ref, zero_ref=None):
    per_tile = sc_ref.shape[0] == 1
    gain_tile = g_ref[...] * (1.0 + sc_ref[...]) if per_tile else None

    def fn(rows):
        x = x_ref[rows, :]
        gain = gain_tile if per_tile else g_ref[...] * (1.0 + sc_ref[rows, :])
        y = x * lax.rsqrt(jnp.mean(x * x, axis=-1, keepdims=True) + EPS) * gain + _mod_rows(sh_ref, rows)
        h_ref[rows, :] = y.astype(BF16)
        if zero_ref is not None:
            zero_ref[rows, :] = jnp.zeros_like(x)

    _row_loop(x_ref.shape[0], fn)


def _once(shape, index_map):
    return pl.BlockSpec(shape, index_map, pipeline_mode=pl.Buffered(1))


def _mod_specs(mod, layer, secs, width, col_of, row_tiles):
    groups, r = mod.shape[1], mod.shape[2]
    tiles_per_group = max(row_tiles // groups, 1)
    ncol = mod.shape[3] // (N_SUB * 3) // width

    def spec(sec):
        def index_map(*idx):
            i, n = col_of(*idx)
            return (layer, i // tiles_per_group, 0, sec * ncol + n)
        if groups == 1 and ncol == 1:
            return _once((None, None, r, width), index_map)
        return pl.BlockSpec((None, None, r, width), index_map)

    return [spec(s) for s in secs]


def _ada_kernel(cs_ref, cp_ref, w_ref, b_ref, os_ref, op_ref):
    w = w_ref[...].astype(BF16)
    os_ref[...] = _dot(_silu(cs_ref[...]).astype(BF16), w) + b_ref[...]
    op_ref[...] = _dot(_silu(cp_ref[...]).astype(BF16), w) + b_ref[...]


def _ada_call(c_sample, c_prompt, ada_w, ada_b, tn=ADA_TILE):
    depth, d, n = ada_w.shape
    rs, rp = c_sample.shape[0], c_prompt.shape[0]
    return pl.pallas_call(
        _ada_kernel,
        grid=(depth, n // tn),
        in_specs=[
            pl.BlockSpec((rs, d), lambda l, j: (0, 0)),
            pl.BlockSpec((rp, d), lambda l, j: (0, 0)),
            pl.BlockSpec((None, d, tn), lambda l, j: (l, 0, j)),
            pl.BlockSpec((None, 1, tn), lambda l, j: (l, 0, j)),
        ],
        out_specs=[
            pl.BlockSpec((None, rs, tn), lambda l, j: (l, 0, j)),
            pl.BlockSpec((None, rp, tn), lambda l, j: (l, 0, j)),
        ],
        out_shape=[jax.ShapeDtypeStruct((depth, rs, n), F32), jax.ShapeDtypeStruct((depth, rp, n), F32)],
        compiler_params=_cparams("arbitrary", "arbitrary"),
        name="adaln_params",
    )(c_sample, c_prompt, ada_w, ada_b.reshape(depth, 1, n))


def _ffn_kernel(x_ref, sh_ref, sc_ref, gt_ref, xs_ref, shs_ref, scs_ref, gts_ref, g_ref, wa_ref, wu_ref, wo_ref,
                *rest, final):
    fg_ref = rest[0] if final else None
    o_ref, os_ref, h_ref = rest[-3:]
    tm, ms = x_ref.shape[0], xs_ref.shape[0]
    first_tile = pl.program_id(0) == 0
    f = pl.program_id(1)
    last = pl.num_programs(1) - 1

    def start(x_ref, sh_ref, sc_ref, o_ref, row0):
        _modulate_into(h_ref.at[pl.ds(row0, x_ref.shape[0])], x_ref, g_ref, sh_ref, sc_ref, zero_ref=o_ref)

    def finish(x_ref, gt_ref, o_ref):
        def fn(rows):
            y = x_ref[rows, :] + 0.5 * _mod_rows(gt_ref, rows) * o_ref[rows, :]
            if fg_ref is not None:
                y = _rmsnorm(y, fg_ref[...])
            o_ref[rows, :] = y

        _row_loop(x_ref.shape[0], fn, in_place=True)

    def hidden(rows):
        h = h_ref[0:rows, :]
        return (_silu(_dot(h, wa_ref[...].astype(BF16))) * _dot(h, wu_ref[...].astype(BF16))).astype(BF16)

    @pl.when(f == 0)
    def _():
        start(x_ref, sh_ref, sc_ref, o_ref, 0)

    @pl.when((f == 0) & first_tile)
    def _():
        start(xs_ref, shs_ref, scs_ref, os_ref, tm)

    @pl.when(jnp.logical_not(first_tile))
    def _():
        o_ref[...] += _dot(hidden(tm), wo_ref[...].astype(BF16))

    @pl.when(first_tile)
    def _():
        act = hidden(tm + ms)
        wo = wo_ref[...].astype(BF16)
        o_ref[...] += _dot(act[0:tm], wo)
        os_ref[...] += _dot(act[tm:tm + ms], wo)

    @pl.when(f == last)
    def _():
        finish(x_ref, gt_ref, o_ref)

    @pl.when((f == last) & first_tile)
    def _():
        finish(xs_ref, gts_ref, os_ref)


def _ffn_call(x, xs, mod, mod_s, norm_g, w_in, w_out, final_g, *, layer, sub, j, final):
    m, d = x.shape
    ms = xs.shape[0]
    tm, tf = ROW_TILE, FFN_TILE
    nf = w_out.shape[2] // tf
    secs = [3 * sub, 3 * sub + 1, 3 * sub + 2]
    in_specs = (
        [pl.BlockSpec((tm, d), lambda i, f: (i, 0))]
        + _mod_specs(mod, layer, secs, d, lambda i, f: (i, 0), m // tm)
        + [_once((ms, d), lambda i, f: (0, 0))]
        + _mod_specs(mod_s, layer, secs, d, lambda i, f: (0, 0), 1)
        + [
            pl.BlockSpec((None, None, 1, d), lambda i, f: (layer, sub, 0, 0)),
            pl.BlockSpec((None, None, d, tf), lambda i, f: (layer, j, 0, f)),
            pl.BlockSpec((None, None, d, tf), lambda i, f: (layer, j, 0, nf + f)),
            pl.BlockSpec((None, None, tf, d), lambda i, f: (layer, j, f, 0)),
        ]
    )
    args = [x, mod, mod, mod, xs, mod_s, mod_s, mod_s, norm_g, w_in, w_in, w_out]
    if final:
        in_specs.append(pl.BlockSpec((1, d), lambda i, f: (0, 0)))
        args.append(final_g.reshape(1, d))
    return pl.pallas_call(
        functools.partial(_ffn_kernel, final=final),
        grid=(m // tm, nf),
        in_specs=in_specs,
        out_specs=[pl.BlockSpec((tm, d), lambda i, f: (i, 0)), _once((ms, d), lambda i, f: (0, 0))],
        out_shape=[jax.ShapeDtypeStruct((m, d), F32), jax.ShapeDtypeStruct((ms, d), F32)],
        scratch_shapes=[pltpu.VMEM((tm + ms, d), BF16)],
        compiler_params=_cparams("arbitrary", "arbitrary"),
        name="ffn",
    )(*args)


def _hgrn_in_rows(x_ref, sh_ref, sc_ref, g_ref, bf_ref, h_ref, outs, ws, lb):
    q_ref, k_ref, v_ref, lf_ref, sg_ref = outs
    wq_ref, wf_ref, wi_ref, wg_ref = ws

    @pl.when(pl.program_id(1) == 0)
    def _():
        _modulate_into(h_ref, x_ref, g_ref, sh_ref, sc_ref)

    h = h_ref[...]
    q_ref[...] = _dot(h, wq_ref[...].astype(BF16))
    z = _dot(h, wf_ref[...].astype(BF16)) + bf_ref[...]
    lf_ref[...] = jnp.log(lb + (1.0 - lb) * _sigmoid(z))
    k_ref[...] = (1.0 - lb) * _sigmoid(-z)
    v_ref[...] = _dot(h, wi_ref[...].astype(BF16)).astype(BF16)
    sg_ref[...] = _silu(_dot(h, wg_ref[...].astype(BF16)))


def _hgrn_in_kernel(x_ref, sh_ref, sc_ref, xs_ref, shs_ref, scs_ref, g_ref, wq_ref, wf_ref, wi_ref, wg_ref,
                    bf_ref, lb_ref, *rest, lb_row):
    outs, outs_s, (h_ref, hs_ref) = rest[0:5], rest[5:10], rest[10:12]
    raw = [lb_ref[i:i + 1, :] for i in range(lb_ref.shape[0])]
    top = functools.reduce(jnp.maximum, raw)
    e = [jnp.exp(r - top) for r in raw]
    lb = sum(e[:lb_row + 1]) / sum(e)
    ws = (wq_ref, wf_ref, wi_ref, wg_ref)
    _hgrn_in_rows(x_ref, sh_ref, sc_ref, g_ref, bf_ref, h_ref, outs, ws, lb)

    @pl.when(pl.program_id(0) == 0)
    def _():
        _hgrn_in_rows(xs_ref, shs_ref, scs_ref, g_ref, bf_ref, hs_ref, outs_s, ws, lb)


def _side_col(nn):
    return lambda i, n: (0, jnp.where(i == 0, n, nn - 1))


def _hgrn_in_call(x, xs, mod, mod_s, norm_g, w_in, b_f, lb_raw, *, layer, j, lb_row):
    m, d = x.shape
    ms = xs.shape[0]
    tm, tn = ROW_TILE, IN_TILE
    nn = d // tn
    sub = 1
    secs = [3 * sub, 3 * sub + 1]

    def wspec(sec):
        return pl.BlockSpec((None, d, tn), lambda i, n: (j, 0, sec * nn + n))

    def out_shapes(rows):
        return [jax.ShapeDtypeStruct((rows, d), dt) for dt in (F32, F32, BF16, F32, F32)]

    return pl.pallas_call(
        functools.partial(_hgrn_in_kernel, lb_row=lb_row),
        grid=(m // tm, nn),
        in_specs=(
            [pl.BlockSpec((tm, d), lambda i, n: (i, 0))]
            + _mod_specs(mod, layer, secs, d, lambda i, n: (i, 0), m // tm)
            + [_once((ms, d), lambda i, n: (0, 0))]
            + _mod_specs(mod_s, layer, secs, d, lambda i, n: (0, 0), 1)
            + [
                pl.BlockSpec((None, None, 1, d), lambda i, n: (layer, sub, 0, 0)),
                wspec(0), wspec(1), wspec(2), wspec(3),
                pl.BlockSpec((None, 1, tn), lambda i, n: (j, 0, n)),
                pl.BlockSpec((lb_raw.shape[0], tn), lambda i, n: (0, n)),
            ]
        ),
        out_specs=[pl.BlockSpec((tm, tn), lambda i, n: (i, n))] * 5 + [pl.BlockSpec((ms, tn), _side_col(nn))] * 5,
        out_shape=out_shapes(m) + out_shapes(ms),
        scratch_shapes=[pltpu.VMEM((tm, d), BF16), pltpu.VMEM((ms, d), BF16)],
        compiler_params=_cparams("arbitrary", "arbitrary"),
        name="hgrn_in",
    )(x, mod, mod, xs, mod_s, mod_s, norm_g, w_in, w_in, w_in, w_in, b_f.reshape(b_f.shape[0], 1, d), lb_raw)


def _hgrn_scan_kernel(q_ref, k_ref, v_ref, lf_ref, sg_ref, gn_ref, y_ref, sfin_ref, st_ref, b_ref, o_ref, *,
                      heads, span_max):
    c = pl.program_id(1)
    rows, d = q_ref.shape
    dk = d // heads
    sub = min(SCAN_SUB, rows)
    n_sub = rows // sub
    n_levels = n_sub.bit_length() - 1
    sub_shift = sub.bit_length() - 1

    @pl.when(c == 0)
    def _():
        st_ref[...] = jnp.zeros_like(st_ref)

    same_head = ((lax.broadcasted_iota(jnp.int32, (2 * dk, 2 * dk), 0) < dk)
                 == (lax.broadcasted_iota(jnp.int32, (2 * dk, 2 * dk), 1) < dk))
    row = lax.broadcasted_iota(jnp.int32, (rows, rows), 0)
    col = lax.broadcasted_iota(jnp.int32, (rows, rows), 1)
    tri = jnp.where(col <= row, 1.0, 0.0).astype(BF16)
    lf = lf_ref[...]
    p1 = lf.astype(BF16)
    r1 = lf - p1.astype(F32)
    p2 = r1.astype(BF16)
    p3 = (r1 - p2.astype(F32)).astype(BF16)
    b_ref[...] = _dot(tri, p1) + _dot(tri, p2) + _dot(tri, p3)

    def first_row(j):
        return slice(j * sub, j * sub + 1)

    def last_row(j):
        return slice((j + 1) * sub - 1, (j + 1) * sub)

    spans = [b_ref[first_row(j), :] - b_ref[last_row(j), :] for j in range(n_sub)]
    safe = jnp.max(functools.reduce(jnp.maximum, spans)) <= span_max

    def gated_out(o, sl):
        return (_rmsnorm(o, gn_ref[:, sl]) * sg_ref[:, sl]).astype(BF16)

    @pl.when(safe)
    def _():
        srl = lax.shift_right_logical
        level = jnp.where((srl(row, sub_shift) == srl(col, sub_shift)) & (col <= row), 0, -1)
        later_half = []
        row_d = lax.broadcasted_iota(jnp.int32, (rows, dk), 0)
        for l in range(1, n_levels + 1):
            s = sub_shift + l - 1
            hit = (srl(row, s + 1) == srl(col, s + 1)) & ((srl(row, s) & 1) == 1) & ((srl(col, s) & 1) == 0)
            level = jnp.where(hit, l, level)
            later_half.append((srl(row_d, s) & 1) == 1)

        def rows_of(pieces, n):
            return jnp.concatenate([jnp.broadcast_to(p, (n, dk)) for p in pieces], axis=0)

        def scores(sl):
            b = b_ref[:, sl]
            q = q_ref[:, sl]
            k = k_ref[:, sl]
            rel = b - rows_of([0.5 * (b_ref[first_row(j), sl] + b_ref[last_row(j), sl]) for j in range(n_sub)], sub)
            att = jnp.where(level == 0,
                            _dot_nt((q * jnp.exp(rel)).astype(BF16), (k * jnp.exp(-rel)).astype(BF16)), 0.0)
            for l in range(1, n_levels + 1):
                m = sub << (l - 1)
                rel = b - rows_of([b_ref[p * 2 * m + m:p * 2 * m + m + 1, sl] for p in range(rows // (2 * m))], 2 * m)
                e = jnp.exp(jnp.minimum(jnp.where(later_half[l - 1], rel, -rel), 0.0))
                att = jnp.where(level == l, _dot_nt((q * e).astype(BF16), (k * e).astype(BF16)), att)
            return att.astype(BF16)

        first_head = lax.broadcasted_iota(jnp.int32, (rows, 2 * dk), 1) < dk
        for pair in range(heads // 2):
            sl0 = slice(2 * pair * dk, (2 * pair + 1) * dk)
            sl1 = slice((2 * pair + 1) * dk, (2 * pair + 2) * dk)
            both = slice(2 * pair * dk, (2 * pair + 2) * dk)
            att = jnp.concatenate([scores(sl0), scores(sl1)], axis=1)
            v = v_ref[:, both]
            zero = jnp.zeros_like(v)
            v_split = jnp.concatenate([jnp.where(first_head, v, zero), jnp.where(first_head, zero, v)], axis=0)
            b = b_ref[:, both]
            b_last = b_ref[rows - 1:rows, both]
            q_st = (q_ref[:, both] * jnp.exp(b)).astype(BF16)
            k_st = (k_ref[:, both] * jnp.exp(b_last - b)).astype(BF16)
            st = st_ref[pair]
            o = _dot(att, v_split) + _dot_nt(q_st, st.astype(BF16))
            st_ref[pair] = st * jnp.exp(b_last) + jnp.where(same_head, _dot_tn(v, k_st), 0.0)
            y_ref[:, sl0] = gated_out(o[:, 0:dk], sl0)
            y_ref[:, sl1] = gated_out(o[:, dk:2 * dk], sl1)

    @pl.when(jnp.logical_not(safe))
    def _():
        grp_rows = min(SCAN_SLOW_ROWS, rows)
        grp_shift = grp_rows.bit_length() - 1
        pick = lax.broadcasted_iota(jnp.int32, (grp_rows, 2 * dk), 0)
        o_ref[...] = jnp.zeros_like(o_ref)

        def token(t, carry):
            r0 = pl.multiple_of(lax.shift_left(lax.shift_right_logical(t, grp_shift), grp_shift), grp_rows)
            sel = pick == t - r0
            grp = pl.ds(r0, grp_rows)
            for pair in range(heads // 2):
                sl = slice(2 * pair * dk, (2 * pair + 2) * dk)
                f = jnp.sum(jnp.where(sel, jnp.exp(lf_ref[grp, sl]), 0.0), axis=0, keepdims=True)
                q1 = jnp.where(sel, q_ref[grp, sl], 0.0).astype(BF16)
                k1 = jnp.where(sel, k_ref[grp, sl], 0.0).astype(BF16)
                v1 = jnp.where(sel, v_ref[grp, sl].astype(F32), 0.0).astype(BF16)
                st = st_ref[pair] * f + jnp.where(same_head, _dot_tn(v1, k1), 0.0)
                st_ref[pair] = st
                o_ref[grp, sl] = jnp.where(sel, _dot_nt(q1, st.astype(BF16)), o_ref[grp, sl])
            return carry

        lax.fori_loop(0, rows, token, 0)
        for h in range(heads):
            sl = slice(h * dk, (h + 1) * dk)
            y_ref[:, sl] = gated_out(o_ref[:, sl], sl)

    @pl.when(c == pl.num_programs(1) - 1)
    def _():
        for h in range(heads):
            lo = (h % 2) * dk
            sfin_ref[h] = st_ref[h // 2, lo:lo + dk, lo:lo + dk].T


def _hgrn_scan_call(q, k, v, lf, sg, g_norm, *, batch, heads, span_max=SCAN_SPAN_MAX):
    m, d = q.shape
    t = m // batch
    rows = min(SCAN_ROWS, t)
    nc = t // rows
    dk = d // heads
    spec = pl.BlockSpec((rows, d), lambda b, c: (b * nc + c, 0))
    return pl.pallas_call(
        functools.partial(_hgrn_scan_kernel, heads=heads, span_max=span_max),
        grid=(batch, nc),
        in_specs=[spec, spec, spec, spec, spec, pl.BlockSpec((1, d), lambda b, c: (0, 0))],
        out_specs=[spec, pl.BlockSpec((None, heads, dk, dk), lambda b, c: (b, 0, 0, 0))],
        out_shape=[jax.ShapeDtypeStruct((m, d), BF16), jax.ShapeDtypeStruct((batch, heads, dk, dk), F32)],
        scratch_shapes=[pltpu.VMEM((heads // 2, 2 * dk, 2 * dk), F32), pltpu.VMEM((rows, d), F32),
                        pltpu.VMEM((rows, d), F32)],
        compiler_params=_cparams("arbitrary", "arbitrary"),
        name="hgrn_scan",
    )(q, k, v, lf, sg, g_norm.reshape(1, d))


def _hgrn_step_kernel(q_ref, k_ref, v_ref, lf_ref, sg_ref, gn_ref, s0_ref, y_ref, s1_ref, o_ref):
    nb = q_ref.shape[0]
    q_t = q_ref[...].T
    k_t = k_ref[...].T
    f_t = jnp.exp(lf_ref[...]).T
    v = v_ref[...].astype(F32)
    for b in range(nb):
        s1 = s0_ref[b] * f_t[:, b:b + 1] + k_t[:, b:b + 1] * v[b:b + 1, :]
        s1_ref[b] = s1
        o_ref[b:b + 1, :] = jnp.sum(s1 * q_t[:, b:b + 1], axis=0, keepdims=True)
    y = _rmsnorm(o_ref[...], gn_ref[...]) * sg_ref[...]
    y_ref[...] = y.astype(BF16)


def _hgrn_step_call(q, k, v, lf, sg, g_norm, s0, *, heads):
    nb, d = q.shape
    dk = d // heads
    spec = pl.BlockSpec((nb, dk), lambda h: (0, h))
    sspec = pl.BlockSpec((nb, None, dk, dk), lambda h: (0, h, 0, 0))
    return pl.pallas_call(
        _hgrn_step_kernel,
        grid=(heads,),
        in_specs=[spec, spec, spec, spec, spec, pl.BlockSpec((1, dk), lambda h: (0, h)), sspec],
        out_specs=[spec, sspec],
        out_shape=[jax.ShapeDtypeStruct((nb, d), BF16), jax.ShapeDtypeStruct(s0.shape, F32)],
        scratch_shapes=[pltpu.VMEM((nb, dk), F32)],
        compiler_params=_cparams("arbitrary"),
        name="hgrn_step",
    )(q, k, v, lf, sg, g_norm.reshape(1, d), s0)


def _proj_kernel(y_ref, x_ref, gt_ref, ys_ref, xs_ref, gts_ref, w_ref, *rest, has_bias):
    b_ref = rest[0] if has_bias else None
    o_ref, os_ref = rest[-2:]
    def rows(y_ref, x_ref, gt_ref, o_ref):
        out = _dot(y_ref[...], w_ref[...].astype(BF16))
        if has_bias:
            out = out + b_ref[...]
        o_ref[...] = x_ref[...] + gt_ref[...] * out

    rows(y_ref, x_ref, gt_ref, o_ref)

    @pl.when(pl.program_id(1) == 0)
    def _():
        rows(ys_ref, xs_ref, gts_ref, os_ref)


def _proj_call(y, ys, w, bias, x, xs, mod, mod_s, *, layer, j):
    m, kdim = y.shape
    ms = ys.shape[0]
    d = x.shape[1]
    tm, tn = OUT_ROW_TILE, min(OUT_TILE, d)
    gate = [3 * 1 + 2]
    has_bias = bias is not None
    wspec = _once if tn == d else pl.BlockSpec
    in_specs = (
        [pl.BlockSpec((tm, kdim), lambda n, i: (i, 0)), pl.BlockSpec((tm, tn), lambda n, i: (i, n))]
        + _mod_specs(mod, layer, gate, tn, lambda n, i: (i, n), m // tm)
        + [_once((ms, kdim), lambda n, i: (0, 0)), pl.BlockSpec((ms, tn), lambda n, i: (0, n))]
        + _mod_specs(mod_s, layer, gate, tn, lambda n, i: (0, n), 1)
        + [wspec((None, kdim, tn), lambda n, i: (j, 0, n))]
    )
    args = [y, x, mod, ys, xs, mod_s, w]
    if has_bias:
        in_specs.append(pl.BlockSpec((None, 1, tn), lambda n, i: (j, 0, n)))
        args.append(bias.reshape(bias.shape[0], 1, d))
    return pl.pallas_call(
        functools.partial(_proj_kernel, has_bias=has_bias),
        grid=(d // tn, m // tm),
        in_specs=in_specs,
        out_specs=[pl.BlockSpec((tm, tn), lambda n, i: (i, n)), pl.BlockSpec((ms, tn), lambda n, i: (0, n))],
        out_shape=[jax.ShapeDtypeStruct((m, d), F32), jax.ShapeDtypeStruct((ms, d), F32)],
        compiler_params=_cparams("arbitrary", "arbitrary"),
        name="proj_residual",
    )(*args)


def _proj_rows_kernel(y_ref, x_ref, gt_ref, w_ref, b_ref, o_ref):
    o_ref[...] = x_ref[...] + gt_ref[...] * (_dot(y_ref[...], w_ref[...].astype(BF16)) + b_ref[...])


def _proj_rows_call(y, w, bias, x, mod, *, layer, j):
    ms, kdim = y.shape
    d = x.shape[1]
    tn = IN_TILE
    return pl.pallas_call(
        _proj_rows_kernel,
        grid=(d // tn,),
        in_specs=(
            [_once((ms, kdim), lambda n: (0, 0)), pl.BlockSpec((ms, tn), lambda n: (0, n))]
            + _mod_specs(mod, layer, [3 * 1 + 2], tn, lambda n: (0, n), 1)
            + [pl.BlockSpec((None, kdim, tn), lambda n: (j, 0, n)), pl.BlockSpec((None, 1, tn), lambda n: (j, 0, n))]
        ),
        out_specs=pl.BlockSpec((ms, tn), lambda n: (0, n)),
        out_shape=jax.ShapeDtypeStruct((ms, d), F32),
        compiler_params=_cparams("arbitrary"),
        name="proj_rows",
    )(y, x, mod, w, bias.reshape(bias.shape[0], 1, d))


def _pw1_kernel(x_ref, sh_ref, sc_ref, xs_ref, shs_ref, scs_ref, g_ref, wa_ref, wg_ref, ba_ref, bg_ref,
                u_ref, us_ref, h_ref, hs_ref):
    def rows(x_ref, sh_ref, sc_ref, u_ref, h_ref):
        @pl.when(pl.program_id(1) == 0)
        def _():
            _modulate_into(h_ref, x_ref, g_ref, sh_ref, sc_ref)

        h = h_ref[...]
        a = _dot(h, wa_ref[...].astype(BF16)) + ba_ref[...]
        u_ref[...] = a * _sigmoid(_dot(h, wg_ref[...].astype(BF16)) + bg_ref[...])

    rows(x_ref, sh_ref, sc_ref, u_ref, h_ref)

    @pl.when(pl.program_id(0) == 0)
    def _():
        rows(xs_ref, shs_ref, scs_ref, us_ref, hs_ref)


def _pw1_call(x, xs, mod, mod_s, norm_g, w, bias, *, layer, j):
    m, d = x.shape
    ms = xs.shape[0]
    tm, tn = ROW_TILE, PW1_TILE
    ch = w.shape[2] // 2
    nn = ch // tn
    sub = 1
    secs = [3 * sub, 3 * sub + 1]

    def wspec(sec):
        return pl.BlockSpec((None, d, tn), lambda i, n: (j, 0, sec * nn + n))

    def bspec(sec):
        return pl.BlockSpec((None, 1, tn), lambda i, n: (j, 0, sec * nn + n))

    bias3 = bias.reshape(bias.shape[0], 1, 2 * ch)
    return pl.pallas_call(
        _pw1_kernel,
        grid=(m // tm, nn),
        in_specs=(
            [pl.BlockSpec((tm, d), lambda i, n: (i, 0))]
            + _mod_specs(mod, layer, secs, d, lambda i, n: (i, 0), m // tm)
            + [_once((ms, d), lambda i, n: (0, 0))]
            + _mod_specs(mod_s, layer, secs, d, lambda i, n: (0, 0), 1)
            + [
                pl.BlockSpec((None, None, 1, d), lambda i, n: (layer, sub, 0, 0)),
                wspec(0), wspec(1), bspec(0), bspec(1),
            ]
        ),
        out_specs=[pl.BlockSpec((tm, tn), lambda i, n: (i, n)), pl.BlockSpec((ms, tn), _side_col(nn))],
        out_shape=[jax.ShapeDtypeStruct((m, ch), F32), jax.ShapeDtypeStruct((ms, ch), F32)],
        scratch_shapes=[pltpu.VMEM((tm, d), BF16), pltpu.VMEM((ms, d), BF16)],
        compiler_params=_cparams("arbitrary", "arbitrary"),
        name="conv_pw1",
    )(x, mod, mod, xs, mod_s, mod_s, norm_g, w, w, bias3, bias3)


def _ln_silu_into(y_ref, acc_ref, lng_ref, lnb_ref):
    def fn(rows):
        a = acc_ref[rows, :]
        mu = jnp.mean(a, axis=-1, keepdims=True)
        xc = a - mu
        yn = xc * lax.rsqrt(jnp.mean(xc * xc, axis=-1, keepdims=True) + EPS) * lng_ref[...] + lnb_ref[...]
        y_ref[rows, :] = _silu(yn).astype(BF16)

    _row_loop(acc_ref.shape[0], fn)


def _conv_seq_kernel(u_ref, w_ref, bdw_ref, lng_ref, lnb_ref, x_ref, gt_ref, w2_ref, b2_ref, o_ref, tail_ref,
                     win_ref, acc_ref, y_ref, w2b_ref, *, width):
    t = pl.program_id(1)
    nt = pl.num_programs(1) - 1
    tt, ch = u_ref.shape
    n_lt = ch // LANES
    off = CONV_PAD - (width - 1)
    blk = SUBLANES * SUBLANES
    n_slab, _, slab = w2b_ref.shape
    lt_per_slab = n_lt // n_slab

    @pl.when((pl.program_id(0) == 0) & (t == 0))
    def _():
        for s in range(n_slab):
            w2b_ref[s] = w2_ref[:, s * slab:(s + 1) * slab].astype(BF16)

    @pl.when(t == 0)
    def _():
        win_ref[:, 0:CONV_PAD, :] = jnp.zeros((n_lt, CONV_PAD, LANES), F32)

    @pl.when(t < nt)
    def _():
        for l in range(n_lt):
            win_ref[l, CONV_PAD:CONV_PAD + tt, :] = u_ref[:, l * LANES:(l + 1) * LANES]

    def project(s):
        cols = slice(s * slab, (s + 1) * slab)
        out = _dot(y_ref[(t + 1) % 2], w2b_ref[s]) + b2_ref[:, cols]
        o_ref[:, cols] = x_ref[:, cols] + gt_ref[:, cols] * out

    def conv_block(l, r0):
        strided = [win_ref[l, pl.ds(r0 + r, SUBLANES, stride=SUBLANES), :]
                   for r in range(off, off + width - 1 + SUBLANES)]
        accs = [[jnp.broadcast_to(bdw_ref[l], (SUBLANES, LANES)), None] for _ in range(SUBLANES)]
        for tap in range(width):
            w_tap = jnp.broadcast_to(w_ref[l, tap:tap + 1, :], (SUBLANES, LANES))
            for t0 in range(SUBLANES):
                term = strided[t0 + tap] * w_tap
                prev = accs[t0][tap % 2]
                accs[t0][tap % 2] = term if prev is None else prev + term
        for t0 in range(SUBLANES):
            acc_ref[l, pl.ds(r0 + t0, SUBLANES, stride=SUBLANES), :] = accs[t0][0] + accs[t0][1]

    def conv_lane_tiles(lts):
        for l in lts:
            for r0 in range(0, tt, blk):
                conv_block(l, r0)

    @pl.when(t == 0)
    def _():
        def body(i, carry):
            conv_block(i // (tt // blk), pl.multiple_of((i % (tt // blk)) * blk, blk))
            return carry

        lax.fori_loop(0, n_lt * (tt // blk), body, 0, unroll=2)

    @pl.when((t > 0) & (t < nt))
    def _():
        for s in range(n_slab):
            project(s)
            conv_lane_tiles(range(s * lt_per_slab, (s + 1) * lt_per_slab))

    @pl.when(t == nt)
    def _():
        for s in range(n_slab):
            project(s)

    @pl.when(t < nt)
    def _():
        for l in range(n_lt):
            tail_ref[:, l * LANES:(l + 1) * LANES] = win_ref[l, CONV_PAD + tt - (width - 1):CONV_PAD + tt, :]
        win_ref[:, 0:CONV_PAD, :] = win_ref[:, tt:tt + CONV_PAD, :]
        y_cur = y_ref.at[t % 2]

        def fn(rows):
            a = jnp.concatenate([acc_ref[l, rows, :] for l in range(n_lt)], axis=1)
            mu = jnp.mean(a, axis=-1, keepdims=True)
            xc = a - mu
            yn = xc * lax.rsqrt(jnp.mean(xc * xc, axis=-1, keepdims=True) + EPS) * lng_ref[...] + lnb_ref[...]
            y_cur[rows, :] = _silu(yn).astype(BF16)

        _row_loop(tt, fn)


def _conv_seq_call(u, w_dw, b_dw, ln_g, ln_b, x, mod, w2, b2, *, layer, j, batch):
    m, ch = u.shape
    d = x.shape[1]
    t = m // batch
    tt = min(CONV_TILE, t)
    nt = t // tt
    width = w_dw.shape[1]
    n_lt = ch // LANES
    n_slab = d // CONV_SLAB
    vec = lambda a: a.reshape(a.shape[0], 1, a.shape[-1])
    vspec = pl.BlockSpec((None, 1, ch), lambda b, i: (j, 0, 0))
    w_lt = jnp.transpose(w_dw[j].reshape(width, n_lt, LANES), (1, 0, 2))
    b_lt = b_dw[j].reshape(n_lt, 1, LANES)
    conv_tile = lambda b, i: (b * nt + jnp.minimum(i, nt - 1), 0)
    proj_tile = lambda b, i: (b * nt + jnp.maximum(i - 1, 0), 0)
    return pl.pallas_call(
        functools.partial(_conv_seq_kernel, width=width),
        grid=(batch, nt + 1),
        in_specs=[
            pl.BlockSpec((tt, ch), conv_tile),
            pl.BlockSpec((n_lt, width, LANES), lambda b, i: (0, 0, 0)),
            pl.BlockSpec((n_lt, 1, LANES), lambda b, i: (0, 0, 0)),
            vspec, vspec,
            pl.BlockSpec((tt, d), proj_tile),
            pl.BlockSpec((None, None, 1, d), lambda b, i: (layer, b, 0, 3 * 1 + 2)),
            _once((None, ch, d), lambda b, i: (j, 0, 0)),
            pl.BlockSpec((None, 1, d), lambda b, i: (j, 0, 0)),
        ],
        out_specs=[
            pl.BlockSpec((tt, d), proj_tile),
            pl.BlockSpec((None, width - 1, ch), lambda b, i: (b, 0, 0)),
        ],
        out_shape=[jax.ShapeDtypeStruct((m, d), F32), jax.ShapeDtypeStruct((batch, width - 1, ch), F32)],
        scratch_shapes=[
            pltpu.VMEM((n_lt, CONV_PAD + tt, LANES), F32),
            pltpu.VMEM((n_lt, tt, LANES), F32),
            pltpu.VMEM((2, tt, ch), BF16),
            pltpu.VMEM((n_slab, ch, CONV_SLAB), BF16),
        ],
        compiler_params=_cparams("arbitrary", "arbitrary"),
        name="conv_seq",
    )(u, w_lt, b_lt, vec(ln_g), vec(ln_b), x, mod, w2, vec(b2))


def _conv_step_kernel(u_ref, buf_ref, w_ref, bdw_ref, lng_ref, lnb_ref, y_ref, new_ref, acc_ref, *, width):
    u = u_ref[...]
    acc = u * w_ref[width - 1:width, :] + bdw_ref[...]
    for tap in range(width - 1):
        acc = acc + buf_ref[tap] * w_ref[tap:tap + 1, :]
    acc_ref[...] = acc
    for tap in range(width - 2):
        new_ref[tap] = buf_ref[tap + 1]
    new_ref[width - 2] = u
    _ln_silu_into(y_ref, acc_ref, lng_ref, lnb_ref)


def _conv_step_call(u, buf, w_dw, b_dw, ln_g, ln_b, *, j, tb):
    nb, ch = u.shape
    width = w_dw.shape[1]
    vec = lambda a: a.reshape(a.shape[0], 1, ch)
    vspec = pl.BlockSpec((None, 1, ch), lambda i: (j, 0, 0))
    return pl.pallas_call(
        functools.partial(_conv_step_kernel, width=width),
        grid=(nb // tb,),
        in_specs=[
            pl.BlockSpec((tb, ch), lambda i: (i, 0)),
            pl.BlockSpec((None, width - 1, tb, ch), lambda i: (j, 0, i, 0)),
            pl.BlockSpec((None, width, ch), lambda i: (j, 0, 0)),
            vspec, vspec, vspec,
        ],
        out_specs=[
            pl.BlockSpec((tb, ch), lambda i: (i, 0)),
            pl.BlockSpec((width - 1, tb, ch), lambda i: (0, i, 0)),
        ],
        out_shape=[jax.ShapeDtypeStruct((nb, ch), BF16), jax.ShapeDtypeStruct((width - 1, nb, ch), F32)],
        scratch_shapes=[pltpu.VMEM((tb, ch), F32)],
        compiler_params=_cparams("arbitrary"),
        name="conv_step",
    )(u, buf, w_dw, vec(b_dw), vec(ln_g), vec(ln_b))


def kernel(x_prompt, x_sample, state_hgrn, state_conv, c_prompt, c_sample, hgrn_lb, hgrn_w_in, hgrn_b_f, hgrn_g_norm, hgrn_w_out, conv_w_pw1, conv_b_pw1, conv_w_dw, conv_b_dw, conv_ln_g, conv_ln_b, conv_w_pw2, conv_b_pw2, ada_w, ada_b, norm_g, ffn_w_in, ffn_w_out, final_g):
    bp, t, d = x_prompt.shape
    bs = x_sample.shape[0]
    depth = ada_w.shape[0]
    heads = d // LANES
    time_major = lambda s: jnp.transpose(s, (0, 2, 1, 3))

    mod_s, mod_p = _ada_call(c_sample, c_prompt, ada_w, ada_b)
    mod_s = mod_s.reshape(depth, 1, bs, N_SUB * 3 * d)
    mod_p = mod_p.reshape(depth, bp, 1, N_SUB * 3 * d)
    norm_g4 = norm_g.reshape(depth, N_SUB, 1, d)
    conv_buf = time_major(state_conv)

    x, xs = x_prompt.reshape(bp * t, d), x_sample.reshape(bs, d)
    hgrn_p, hgrn_s, conv_p, conv_s = [], [], [], []
    for layer in range(depth):
        x, xs = _ffn_call(x, xs, mod_p, mod_s, norm_g4, ffn_w_in, ffn_w_out, final_g,
                          layer=layer, sub=0, j=0, final=False)
        j = layer // 2
        if layer % 2 == 0:
            outs = _hgrn_in_call(x, xs, mod_p, mod_s, norm_g4, hgrn_w_in, hgrn_b_f, hgrn_lb,
                                 layer=layer, j=j, lb_row=layer)
            y, s = _hgrn_scan_call(*outs[:5], hgrn_g_norm[j], batch=bp, heads=heads)
            ys, ss = _hgrn_step_call(*outs[5:], hgrn_g_norm[j], state_hgrn[j], heads=heads)
            hgrn_p.append(s)
            hgrn_s.append(ss)
            x, xs = _proj_call(y, ys, hgrn_w_out, None, x, xs, mod_p, mod_s, layer=layer, j=j)
        else:
            u, us = _pw1_call(x, xs, mod_p, mod_s, norm_g4, conv_w_pw1, conv_b_pw1, layer=layer, j=j)
            x, s = _conv_seq_call(u, conv_w_dw, conv_b_dw, conv_ln_g, conv_ln_b, x, mod_p, conv_w_pw2, conv_b_pw2,
                                  layer=layer, j=j, batch=bp)
            ys, ss = _conv_step_call(us, conv_buf, conv_w_dw, conv_b_dw, conv_ln_g, conv_ln_b, j=j, tb=4 * SUBLANES)
            conv_p.append(jnp.transpose(s, (1, 0, 2)))
            conv_s.append(ss)
            xs = _proj_rows_call(ys, conv_w_pw2, conv_b_pw2, xs, mod_s, layer=layer, j=j)
        x, xs = _ffn_call(x, xs, mod_p, mod_s, norm_g4, ffn_w_in, ffn_w_out, final_g,
                          layer=layer, sub=2, j=1, final=(layer == depth - 1))
    return (x.reshape(bp, t, d), xs.reshape(bs, 1, d), jnp.stack(hgrn_p), jnp.stack(hgrn_s),
            time_major(jnp.stack(conv_p)), time_major(jnp.stack(conv_s)))
```

```python
import functools

import jax
import jax.numpy as jnp
from jax import lax
from jax.experimental import pallas as pl
from jax.experimental.pallas import tpu as pltpu

F32 = jnp.float32
BF16 = jnp.bfloat16
EPS = 1e-6
N_SUB = 3
LANES = 128
SUBLANES = 8
VMEM_LIMIT_BYTES = 60 * 1024 * 1024

ROW_TILE = 1024
FFN_TILE = 256
IN_TILE = 256
PW1_TILE = 512
OUT_TILE = 2048
OUT_ROW_TILE = 512
SCAN_ROWS = 256
SCAN_SUB = 64
SCAN_SPAN_MAX = 120.0
SCAN_SLOW_ROWS = 16
CONV_PAD = 32
CONV_TILE = 256
CONV_SLAB = 256
NORM_ROWS = 16
NORM_UNROLL = 8


def _cparams(*sem):
    return pltpu.CompilerParams(dimension_semantics=sem, vmem_limit_bytes=VMEM_LIMIT_BYTES)


def _dot(a, b):
    return jnp.dot(a, b, preferred_element_type=F32)


def _dot_nt(a, b):
    return lax.dot_general(a, b, (((1,), (1,)), ((), ())), preferred_element_type=F32)


def _dot_tn(a, b):
    return lax.dot_general(a, b, (((0,), (0,)), ((), ())), preferred_element_type=F32)


def _sigmoid(x):
    return 1.0 / (1.0 + jnp.exp(-x))


def _silu(x):
    return x * _sigmoid(x)


def _rmsnorm(x, g):
    return x * lax.rsqrt(jnp.mean(x * x, axis=-1, keepdims=True) + EPS) * g


def _row_loop(n_rows, fn, in_place=False):
    rows = min(NORM_ROWS * NORM_UNROLL if in_place else NORM_ROWS, n_rows)
    steps = n_rows // rows
    if steps == 1:
        fn(pl.ds(0, rows))
        return

    def body(i, carry):
        fn(pl.ds(pl.multiple_of(i * rows, rows), rows))
        return carry

    lax.fori_loop(0, steps, body, 0, unroll=NORM_UNROLL if not in_place and steps % NORM_UNROLL == 0 else 1)


def _mod_rows(ref, rows):
    return ref[...] if ref.shape[0] == 1 else ref[rows, :]


def _modulate_into(h_ref, x_ref, g_ref, sh_ref, sc_ref):
    per_tile = sc_ref.shape[0] == 1
    gain_tile = g_ref[...] * (1.0 + sc_ref[...]) if per_tile else None

    def fn(rows):
        x = x_ref[rows, :]
        gain = gain_tile if per_tile else g_ref[...] * (1.0 + sc_ref[rows, :])
        y = x * lax.rsqrt(jnp.mean(x * x, axis=-1, keepdims=True) + EPS) * gain + _mod_rows(sh_ref, rows)
        h_ref[rows, :] = y.astype(BF16)

    _row_loop(x_ref.shape[0], fn)


def _once(shape, index_map):
    return pl.BlockSpec(shape, index_map, pipeline_mode=pl.Buffered(1))


def _mod_specs(mod, layer, secs, width, col_of, row_tiles):
    groups, r = mod.shape[1], mod.shape[2]
    tiles_per_group = max(row_tiles // groups, 1)
    ncol = mod.shape[3] // (N_SUB * 3) // width

    def spec(sec):
        def index_map(*idx):
            i, n = col_of(*idx)
            return (layer, i // tiles_per_group, 0, sec * ncol + n)
        if groups == 1 and ncol == 1:
            return _once((None, None, r, width), index_map)
        return pl.BlockSpec((None, None, r, width), index_map)

    return [spec(s) for s in secs]


def _ada_kernel(cs_ref, cp_ref, w_ref, b_ref, os_ref, op_ref):
    w = w_ref[...].astype(BF16)
    os_ref[...] = _dot(_silu(cs_ref[...]).astype(BF16), w) + b_ref[...]
    op_ref[...] = _dot(_silu(cp_ref[...]).astype(BF16), w) + b_ref[...]


def _ada_call(c_sample, c_prompt, ada_w, ada_b, tn=1024):
    depth, d, n = ada_w.shape
    rs, rp = c_sample.shape[0], c_prompt.shape[0]
    return pl.pallas_call(
        _ada_kernel,
        grid=(depth, n // tn),
        in_specs=[
            pl.BlockSpec((rs, d), lambda l, j: (0, 0)),
            pl.BlockSpec((rp, d), lambda l, j: (0, 0)),
            pl.BlockSpec((None, d, tn), lambda l, j: (l, 0, j)),
            pl.BlockSpec((None, 1, tn), lambda l, j: (l, 0, j)),
        ],
        out_specs=[
            pl.BlockSpec((None, rs, tn), lambda l, j: (l, 0, j)),
            pl.BlockSpec((None, rp, tn), lambda l, j: (l, 0, j)),
        ],
        out_shape=[jax.ShapeDtypeStruct((depth, rs, n), F32), jax.ShapeDtypeStruct((depth, rp, n), F32)],
        compiler_params=_cparams("arbitrary", "arbitrary"),
        name="adaln_params",
    )(c_sample, c_prompt, ada_w, ada_b.reshape(depth, 1, n))


def _ffn_kernel(x_ref, sh_ref, sc_ref, gt_ref, xs_ref, shs_ref, scs_ref, gts_ref, g_ref, wa_ref, wu_ref, wo_ref,
                *rest, final):
    fg_ref = rest[0] if final else None
    o_ref, os_ref, h_ref = rest[-3:]
    tm, ms = x_ref.shape[0], xs_ref.shape[0]
    first_tile = pl.program_id(0) == 0
    f = pl.program_id(1)
    last = pl.num_programs(1) - 1

    def start(x_ref, sh_ref, sc_ref, o_ref, row0):
        _modulate_into(h_ref.at[pl.ds(row0, x_ref.shape[0])], x_ref, g_ref, sh_ref, sc_ref)
        o_ref[...] = jnp.zeros_like(o_ref)

    def finish(x_ref, gt_ref, o_ref):
        def fn(rows):
            y = x_ref[rows, :] + 0.5 * _mod_rows(gt_ref, rows) * o_ref[rows, :]
            if fg_ref is not None:
                y = _rmsnorm(y, fg_ref[...])
            o_ref[rows, :] = y

        _row_loop(x_ref.shape[0], fn, in_place=True)

    def hidden(rows):
        h = h_ref[0:rows, :]
        return (_silu(_dot(h, wa_ref[...].astype(BF16))) * _dot(h, wu_ref[...].astype(BF16))).astype(BF16)

    @pl.when(f == 0)
    def _():
        start(x_ref, sh_ref, sc_ref, o_ref, 0)

    @pl.when((f == 0) & first_tile)
    def _():
        start(xs_ref, shs_ref, scs_ref, os_ref, tm)

    @pl.when(jnp.logical_not(first_tile))
    def _():
        o_ref[...] += _dot(hidden(tm), wo_ref[...].astype(BF16))

    @pl.when(first_tile)
    def _():
        act = hidden(tm + ms)
        wo = wo_ref[...].astype(BF16)
        o_ref[...] += _dot(act[0:tm], wo)
        os_ref[...] += _dot(act[tm:tm + ms], wo)

    @pl.when(f == last)
    def _():
        finish(x_ref, gt_ref, o_ref)

    @pl.when((f == last) & first_tile)
    def _():
        finish(xs_ref, gts_ref, os_ref)


def _ffn_call(x, xs, mod, mod_s, norm_g, w_in, w_out, final_g, *, layer, sub, j, final):
    m, d = x.shape
    ms = xs.shape[0]
    tm, tf = ROW_TILE, FFN_TILE
    nf = w_out.shape[2] // tf
    secs = [3 * sub, 3 * sub + 1, 3 * sub + 2]
    in_specs = (
        [pl.BlockSpec((tm, d), lambda i, f: (i, 0))]
        + _mod_specs(mod, layer, secs, d, lambda i, f: (i, 0), m // tm)
        + [_once((ms, d), lambda i, f: (0, 0))]
        + _mod_specs(mod_s, layer, secs, d, lambda i, f: (0, 0), 1)
        + [
            pl.BlockSpec((None, None, 1, d), lambda i, f: (layer, sub, 0, 0)),
            pl.BlockSpec((None, None, d, tf), lambda i, f: (layer, j, 0, f)),
            pl.BlockSpec((None, None, d, tf), lambda i, f: (layer, j, 0, nf + f)),
            pl.BlockSpec((None, None, tf, d), lambda i, f: (layer, j, f, 0)),
        ]
    )
    args = [x, mod, mod, mod, xs, mod_s, mod_s, mod_s, norm_g, w_in, w_in, w_out]
    if final:
        in_specs.append(pl.BlockSpec((1, d), lambda i, f: (0, 0)))
        args.append(final_g.reshape(1, d))
    return pl.pallas_call(
        functools.partial(_ffn_kernel, final=final),
        grid=(m // tm, nf),
        in_specs=in_specs,
        out_specs=[pl.BlockSpec((tm, d), lambda i, f: (i, 0)), _once((ms, d), lambda i, f: (0, 0))],
        out_shape=[jax.ShapeDtypeStruct((m, d), F32), jax.ShapeDtypeStruct((ms, d), F32)],
        scratch_shapes=[pltpu.VMEM((tm + ms, d), BF16)],
        compiler_params=_cparams("arbitrary", "arbitrary"),
        name="ffn",
    )(*args)


def _hgrn_in_rows(x_ref, sh_ref, sc_ref, g_ref, bf_ref, h_ref, outs, ws, lb):
    q_ref, k_ref, v_ref, lf_ref, sg_ref = outs
    wq_ref, wf_ref, wi_ref, wg_ref = ws

    @pl.when(pl.program_id(1) == 0)
    def _():
        _modulate_into(h_ref, x_ref, g_ref, sh_ref, sc_ref)

    h = h_ref[...]
    q_ref[...] = _dot(h, wq_ref[...].astype(BF16))
    z = _dot(h, wf_ref[...].astype(BF16)) + bf_ref[...]
    lf_ref[...] = jnp.log(lb + (1.0 - lb) * _sigmoid(z))
    k_ref[...] = (1.0 - lb) * _sigmoid(-z)
    v_ref[...] = _dot(h, wi_ref[...].astype(BF16)).astype(BF16)
    sg_ref[...] = _silu(_dot(h, wg_ref[...].astype(BF16)))


def _hgrn_in_kernel(x_ref, sh_ref, sc_ref, xs_ref, shs_ref, scs_ref, g_ref, wq_ref, wf_ref, wi_ref, wg_ref,
                    bf_ref, lb_ref, *rest, lb_row):
    outs, outs_s, (h_ref, hs_ref) = rest[0:5], rest[5:10], rest[10:12]
    raw = [lb_ref[i:i + 1, :] for i in range(lb_ref.shape[0])]
    top = functools.reduce(jnp.maximum, raw)
    e = [jnp.exp(r - top) for r in raw]
    lb = sum(e[:lb_row + 1]) / sum(e)
    ws = (wq_ref, wf_ref, wi_ref, wg_ref)
    _hgrn_in_rows(x_ref, sh_ref, sc_ref, g_ref, bf_ref, h_ref, outs, ws, lb)

    @pl.when(pl.program_id(0) == 0)
    def _():
        _hgrn_in_rows(xs_ref, shs_ref, scs_ref, g_ref, bf_ref, hs_ref, outs_s, ws, lb)


def _side_col(nn):
    return lambda i, n: (0, jnp.where(i == 0, n, nn - 1))


def _hgrn_in_call(x, xs, mod, mod_s, norm_g, w_in, b_f, lb_raw, *, layer, j, lb_row):
    m, d = x.shape
    ms = xs.shape[0]
    tm, tn = ROW_TILE, IN_TILE
    nn = d // tn
    sub = 1
    secs = [3 * sub, 3 * sub + 1]

    def wspec(sec):
        return pl.BlockSpec((None, d, tn), lambda i, n: (j, 0, sec * nn + n))

    def out_shapes(rows):
        return [jax.ShapeDtypeStruct((rows, d), dt) for dt in (F32, F32, BF16, F32, F32)]

    return pl.pallas_call(
        functools.partial(_hgrn_in_kernel, lb_row=lb_row),
        grid=(m // tm, nn),
        in_specs=(
            [pl.BlockSpec((tm, d), lambda i, n: (i, 0))]
            + _mod_specs(mod, layer, secs, d, lambda i, n: (i, 0), m // tm)
            + [_once((ms, d), lambda i, n: (0, 0))]
            + _mod_specs(mod_s, layer, secs, d, lambda i, n: (0, 0), 1)
            + [
                pl.BlockSpec((None, None, 1, d), lambda i, n: (layer, sub, 0, 0)),
                wspec(0), wspec(1), wspec(2), wspec(3),
                pl.BlockSpec((None, 1, tn), lambda i, n: (j, 0, n)),
                pl.BlockSpec((lb_raw.shape[0], tn), lambda i, n: (0, n)),
            ]
        ),
        out_specs=[pl.BlockSpec((tm, tn), lambda i, n: (i, n))] * 5 + [pl.BlockSpec((ms, tn), _side_col(nn))] * 5,
        out_shape=out_shapes(m) + out_shapes(ms),
        scratch_shapes=[pltpu.VMEM((tm, d), BF16), pltpu.VMEM((ms, d), BF16)],
        compiler_params=_cparams("arbitrary", "arbitrary"),
        name="hgrn_in",
    )(x, mod, mod, xs, mod_s, mod_s, norm_g, w_in, w_in, w_in, w_in, b_f.reshape(b_f.shape[0], 1, d), lb_raw)


def _hgrn_scan_kernel(q_ref, k_ref, v_ref, lf_ref, sg_ref, gn_ref, y_ref, sfin_ref, st_ref, b_ref, o_ref, *,
                      heads, span_max):
    c = pl.program_id(1)
    rows, d = q_ref.shape
    dk = d // heads
    sub = min(SCAN_SUB, rows)
    n_sub = rows // sub
    n_levels = n_sub.bit_length() - 1
    sub_shift = sub.bit_length() - 1

    @pl.when(c == 0)
    def _():
        st_ref[...] = jnp.zeros_like(st_ref)

    same_head = ((lax.broadcasted_iota(jnp.int32, (2 * dk, 2 * dk), 0) < dk)
                 == (lax.broadcasted_iota(jnp.int32, (2 * dk, 2 * dk), 1) < dk))
    row = lax.broadcasted_iota(jnp.int32, (rows, rows), 0)
    col = lax.broadcasted_iota(jnp.int32, (rows, rows), 1)
    tri = jnp.where(col <= row, 1.0, 0.0).astype(BF16)
    lf = lf_ref[...]
    p1 = lf.astype(BF16)
    r1 = lf - p1.astype(F32)
    p2 = r1.astype(BF16)
    p3 = (r1 - p2.astype(F32)).astype(BF16)
    b_ref[...] = _dot(tri, p1) + _dot(tri, p2) + _dot(tri, p3)

    def first_row(j):
        return slice(j * sub, j * sub + 1)

    def last_row(j):
        return slice((j + 1) * sub - 1, (j + 1) * sub)

    spans = [b_ref[first_row(j), :] - b_ref[last_row(j), :] for j in range(n_sub)]
    safe = jnp.max(functools.reduce(jnp.maximum, spans)) <= span_max

    def gated_out(o, sl):
        return (_rmsnorm(o, gn_ref[:, sl]) * sg_ref[:, sl]).astype(BF16)

    @pl.when(safe)
    def _():
        srl = lax.shift_right_logical
        level = jnp.where((srl(row, sub_shift) == srl(col, sub_shift)) & (col <= row), 0, -1)
        later_half = []
        row_d = lax.broadcasted_iota(jnp.int32, (rows, dk), 0)
        for l in range(1, n_levels + 1):
            s = sub_shift + l - 1
            hit = (srl(row, s + 1) == srl(col, s + 1)) & ((srl(row, s) & 1) == 1) & ((srl(col, s) & 1) == 0)
            level = jnp.where(hit, l, level)
            later_half.append((srl(row_d, s) & 1) == 1)

        def rows_of(pieces, n):
            return jnp.concatenate([jnp.broadcast_to(p, (n, dk)) for p in pieces], axis=0)

        def scores(sl):
            b = b_ref[:, sl]
            q = q_ref[:, sl]
            k = k_ref[:, sl]
            rel = b - rows_of([0.5 * (b_ref[first_row(j), sl] + b_ref[last_row(j), sl]) for j in range(n_sub)], sub)
            att = jnp.where(level == 0,
                            _dot_nt((q * jnp.exp(rel)).astype(BF16), (k * jnp.exp(-rel)).astype(BF16)), 0.0)
            for l in range(1, n_levels + 1):
                m = sub << (l - 1)
                rel = b - rows_of([b_ref[p * 2 * m + m:p * 2 * m + m + 1, sl] for p in range(rows // (2 * m))], 2 * m)
                e = jnp.exp(jnp.minimum(jnp.where(later_half[l - 1], rel, -rel), 0.0))
                att = jnp.where(level == l, _dot_nt((q * e).astype(BF16), (k * e).astype(BF16)), att)
            return att.astype(BF16)

        first_head = lax.broadcasted_iota(jnp.int32, (rows, 2 * dk), 1) < dk
        for pair in range(heads // 2):
            sl0 = slice(2 * pair * dk, (2 * pair + 1) * dk)
            sl1 = slice((2 * pair + 1) * dk, (2 * pair + 2) * dk)
            both = slice(2 * pair * dk, (2 * pair + 2) * dk)
            att = jnp.concatenate([scores(sl0), scores(sl1)], axis=1)
            v = v_ref[:, both]
            zero = jnp.zeros_like(v)
            v_split = jnp.concatenate([jnp.where(first_head, v, zero), jnp.where(first_head, zero, v)], axis=0)
            b = b_ref[:, both]
            b_last = b_ref[rows - 1:rows, both]
            q_st = (q_ref[:, both] * jnp.exp(b)).astype(BF16)
            k_st = (k_ref[:, both] * jnp.exp(b_last - b)).astype(BF16)
            st = st_ref[pair]
            o = _dot(att, v_split) + _dot_nt(q_st, st.astype(BF16))
            st_ref[pair] = st * jnp.exp(b_last) + jnp.where(same_head, _dot_tn(v, k_st), 0.0)
            y_ref[:, sl0] = gated_out(o[:, 0:dk], sl0)
            y_ref[:, sl1] = gated_out(o[:, dk:2 * dk], sl1)

    @pl.when(jnp.logical_not(safe))
    def _():
        grp_rows = min(SCAN_SLOW_ROWS, rows)
        grp_shift = grp_rows.bit_length() - 1
        pick = lax.broadcasted_iota(jnp.int32, (grp_rows, 2 * dk), 0)
        o_ref[...] = jnp.zeros_like(o_ref)

        def token(t, carry):
            r0 = pl.multiple_of(lax.shift_left(lax.shift_right_logical(t, grp_shift), grp_shift), grp_rows)
            sel = pick == t - r0
            grp = pl.ds(r0, grp_rows)
            for pair in range(heads // 2):
                sl = slice(2 * pair * dk, (2 * pair + 2) * dk)
                f = jnp.sum(jnp.where(sel, jnp.exp(lf_ref[grp, sl]), 0.0), axis=0, keepdims=True)
                q1 = jnp.where(sel, q_ref[grp, sl], 0.0).astype(BF16)
                k1 = jnp.where(sel, k_ref[grp, sl], 0.0).astype(BF16)
                v1 = jnp.where(sel, v_ref[grp, sl].astype(F32), 0.0).astype(BF16)
                st = st_ref[pair] * f + jnp.where(same_head, _dot_tn(v1, k1), 0.0)
                st_ref[pair] = st
                o_ref[grp, sl] = jnp.where(sel, _dot_nt(q1, st.astype(BF16)), o_ref[grp, sl])
            return carry

        lax.fori_loop(0, rows, token, 0)
        for h in range(heads):
            sl = slice(h * dk, (h + 1) * dk)
            y_ref[:, sl] = gated_out(o_ref[:, sl], sl)

    @pl.when(c == pl.num_programs(1) - 1)
    def _():
        for h in range(heads):
            lo = (h % 2) * dk
            sfin_ref[h] = st_ref[h // 2, lo:lo + dk, lo:lo + dk].T


def _hgrn_scan_call(q, k, v, lf, sg, g_norm, *, batch, heads, span_max=SCAN_SPAN_MAX):
    m, d = q.shape
    t = m // batch
    rows = min(SCAN_ROWS, t)
    nc = t // rows
    dk = d // heads
    spec = pl.BlockSpec((rows, d), lambda b, c: (b * nc + c, 0))
    return pl.pallas_call(
        functools.partial(_hgrn_scan_kernel, heads=heads, span_max=span_max),
        grid=(batch, nc),
        in_specs=[spec, spec, spec, spec, spec, pl.BlockSpec((1, d), lambda b, c: (0, 0))],
        out_specs=[spec, pl.BlockSpec((None, heads, dk, dk), lambda b, c: (b, 0, 0, 0))],
        out_shape=[jax.ShapeDtypeStruct((m, d), BF16), jax.ShapeDtypeStruct((batch, heads, dk, dk), F32)],
        scratch_shapes=[pltpu.VMEM((heads // 2, 2 * dk, 2 * dk), F32), pltpu.VMEM((rows, d), F32),
                        pltpu.VMEM((rows, d), F32)],
        compiler_params=_cparams("arbitrary", "arbitrary"),
        name="hgrn_scan",
    )(q, k, v, lf, sg, g_norm.reshape(1, d))


def _hgrn_step_kernel(q_ref, k_ref, v_ref, lf_ref, sg_ref, gn_ref, s0_ref, y_ref, s1_ref, o_ref):
    nb = q_ref.shape[0]
    k_t = k_ref[...].T
    f_t = jnp.exp(lf_ref[...]).T
    v = v_ref[...].astype(F32)
    q = q_ref[...].astype(BF16)
    for b in range(nb):
        s1 = s0_ref[b] * f_t[:, b:b + 1] + k_t[:, b:b + 1] * v[b:b + 1, :]
        s1_ref[b] = s1
        o_ref[b:b + 1, :] = _dot(q[b:b + 1, :], s1.astype(BF16))
    y = _rmsnorm(o_ref[...], gn_ref[...]) * sg_ref[...]
    y_ref[...] = y.astype(BF16)


def _hgrn_step_call(q, k, v, lf, sg, g_norm, s0, *, heads):
    nb, d = q.shape
    dk = d // heads
    spec = pl.BlockSpec((nb, dk), lambda h: (0, h))
    sspec = pl.BlockSpec((nb, None, dk, dk), lambda h: (0, h, 0, 0))
    return pl.pallas_call(
        _hgrn_step_kernel,
        grid=(heads,),
        in_specs=[spec, spec, spec, spec, spec, pl.BlockSpec((1, dk), lambda h: (0, h)), sspec],
        out_specs=[spec, sspec],
        out_shape=[jax.ShapeDtypeStruct((nb, d), BF16), jax.ShapeDtypeStruct(s0.shape, F32)],
        scratch_shapes=[pltpu.VMEM((nb, dk), F32)],
        compiler_params=_cparams("arbitrary"),
        name="hgrn_step",
    )(q, k, v, lf, sg, g_norm.reshape(1, d), s0)


def _proj_kernel(y_ref, x_ref, gt_ref, ys_ref, xs_ref, gts_ref, w_ref, *rest, has_bias):
    b_ref = rest[0] if has_bias else None
    o_ref, os_ref = rest[-2:]
    def rows(y_ref, x_ref, gt_ref, o_ref):
        out = _dot(y_ref[...], w_ref[...].astype(BF16))
        if has_bias:
            out = out + b_ref[...]
        o_ref[...] = x_ref[...] + gt_ref[...] * out

    rows(y_ref, x_ref, gt_ref, o_ref)

    @pl.when(pl.program_id(1) == 0)
    def _():
        rows(ys_ref, xs_ref, gts_ref, os_ref)


def _proj_call(y, ys, w, bias, x, xs, mod, mod_s, *, layer, j):
    m, kdim = y.shape
    ms = ys.shape[0]
    d = x.shape[1]
    tm, tn = OUT_ROW_TILE, min(OUT_TILE, d)
    gate = [3 * 1 + 2]
    has_bias = bias is not None
    wspec = _once if tn == d else pl.BlockSpec
    in_specs = (
        [pl.BlockSpec((tm, kdim), lambda n, i: (i, 0)), pl.BlockSpec((tm, tn), lambda n, i: (i, n))]
        + _mod_specs(mod, layer, gate, tn, lambda n, i: (i, n), m // tm)
        + [_once((ms, kdim), lambda n, i: (0, 0)), pl.BlockSpec((ms, tn), lambda n, i: (0, n))]
        + _mod_specs(mod_s, layer, gate, tn, lambda n, i: (0, n), 1)
        + [wspec((None, kdim, tn), lambda n, i: (j, 0, n))]
    )
    args = [y, x, mod, ys, xs, mod_s, w]
    if has_bias:
        in_specs.append(pl.BlockSpec((None, 1, tn), lambda n, i: (j, 0, n)))
        args.append(bias.reshape(bias.shape[0], 1, d))
    return pl.pallas_call(
        functools.partial(_proj_kernel, has_bias=has_bias),
        grid=(d // tn, m // tm),
        in_specs=in_specs,
        out_specs=[pl.BlockSpec((tm, tn), lambda n, i: (i, n)), pl.BlockSpec((ms, tn), lambda n, i: (0, n))],
        out_shape=[jax.ShapeDtypeStruct((m, d), F32), jax.ShapeDtypeStruct((ms, d), F32)],
        compiler_params=_cparams("arbitrary", "arbitrary"),
        name="proj_residual",
    )(*args)


def _proj_rows_kernel(y_ref, x_ref, gt_ref, w_ref, b_ref, o_ref):
    o_ref[...] = x_ref[...] + gt_ref[...] * (_dot(y_ref[...], w_ref[...].astype(BF16)) + b_ref[...])


def _proj_rows_call(y, w, bias, x, mod, *, layer, j):
    ms, kdim = y.shape
    d = x.shape[1]
    tn = IN_TILE
    return pl.pallas_call(
        _proj_rows_kernel,
        grid=(d // tn,),
        in_specs=(
            [_once((ms, kdim), lambda n: (0, 0)), pl.BlockSpec((ms, tn), lambda n: (0, n))]
            + _mod_specs(mod, layer, [3 * 1 + 2], tn, lambda n: (0, n), 1)
            + [pl.BlockSpec((None, kdim, tn), lambda n: (j, 0, n)), pl.BlockSpec((None, 1, tn), lambda n: (j, 0, n))]
        ),
        out_specs=pl.BlockSpec((ms, tn), lambda n: (0, n)),
        out_shape=jax.ShapeDtypeStruct((ms, d), F32),
        compiler_params=_cparams("arbitrary"),
        name="proj_rows",
    )(y, x, mod, w, bias.reshape(bias.shape[0], 1, d))


def _pw1_kernel(x_ref, sh_ref, sc_ref, xs_ref, shs_ref, scs_ref, g_ref, wa_ref, wg_ref, ba_ref, bg_ref,
                u_ref, us_ref, h_ref, hs_ref):
    def rows(x_ref, sh_ref, sc_ref, u_ref, h_ref):
        @pl.when(pl.program_id(1) == 0)
        def _():
            _modulate_into(h_ref, x_ref, g_ref, sh_ref, sc_ref)

        h = h_ref[...]
        a = _dot(h, wa_ref[...].astype(BF16)) + ba_ref[...]
        u_ref[...] = a * _sigmoid(_dot(h, wg_ref[...].astype(BF16)) + bg_ref[...])

    rows(x_ref, sh_ref, sc_ref, u_ref, h_ref)

    @pl.when(pl.program_id(0) == 0)
    def _():
        rows(xs_ref, shs_ref, scs_ref, us_ref, hs_ref)


def _pw1_call(x, xs, mod, mod_s, norm_g, w, bias, *, layer, j):
    m, d = x.shape
    ms = xs.shape[0]
    tm, tn = ROW_TILE, PW1_TILE
    ch = w.shape[2] // 2
    nn = ch // tn
    sub = 1
    secs = [3 * sub, 3 * sub + 1]

    def wspec(sec):
        return pl.BlockSpec((None, d, tn), lambda i, n: (j, 0, sec * nn + n))

    def bspec(sec):
        return pl.BlockSpec((None, 1, tn), lambda i, n: (j, 0, sec * nn + n))

    bias3 = bias.reshape(bias.shape[0], 1, 2 * ch)
    return pl.pallas_call(
        _pw1_kernel,
        grid=(m // tm, nn),
        in_specs=(
            [pl.BlockSpec((tm, d), lambda i, n: (i, 0))]
            + _mod_specs(mod, layer, secs, d, lambda i, n: (i, 0), m // tm)
            + [_once((ms, d), lambda i, n: (0, 0))]
            + _mod_specs(mod_s, layer, secs, d, lambda i, n: (0, 0), 1)
            + [
                pl.BlockSpec((None, None, 1, d), lambda i, n: (layer, sub, 0, 0)),
                wspec(0), wspec(1), bspec(0), bspec(1),
            ]
        ),
        out_specs=[pl.BlockSpec((tm, tn), lambda i, n: (i, n)), pl.BlockSpec((ms, tn), _side_col(nn))],
        out_shape=[jax.ShapeDtypeStruct((m, ch), F32), jax.ShapeDtypeStruct((ms, ch), F32)],
        scratch_shapes=[pltpu.VMEM((tm, d), BF16), pltpu.VMEM((ms, d), BF16)],
        compiler_params=_cparams("arbitrary", "arbitrary"),
        name="conv_pw1",
    )(x, mod, mod, xs, mod_s, mod_s, norm_g, w, w, bias3, bias3)


def _ln_silu_into(y_ref, acc_ref, lng_ref, lnb_ref):
    def fn(rows):
        a = acc_ref[rows, :]
        mu = jnp.mean(a, axis=-1, keepdims=True)
        xc = a - mu
        yn = xc * lax.rsqrt(jnp.mean(xc * xc, axis=-1, keepdims=True) + EPS) * lng_ref[...] + lnb_ref[...]
        y_ref[rows, :] = _silu(yn).astype(BF16)

    _row_loop(acc_ref.shape[0], fn)


def _conv_seq_kernel(u_ref, w_ref, bdw_ref, lng_ref, lnb_ref, x_ref, gt_ref, w2_ref, b2_ref, o_ref, tail_ref,
                     win_ref, acc_ref, y_ref, w2b_ref, *, width):
    t = pl.program_id(1)
    nt = pl.num_programs(1) - 1
    tt, ch = u_ref.shape
    n_lt = ch // LANES
    off = CONV_PAD - (width - 1)
    blk = SUBLANES * SUBLANES
    n_slab, _, slab = w2b_ref.shape
    lt_per_slab = n_lt // n_slab

    @pl.when((pl.program_id(0) == 0) & (t == 0))
    def _():
        for s in range(n_slab):
            w2b_ref[s] = w2_ref[:, s * slab:(s + 1) * slab].astype(BF16)

    @pl.when(t == 0)
    def _():
        win_ref[:, 0:CONV_PAD, :] = jnp.zeros((n_lt, CONV_PAD, LANES), F32)

    @pl.when(t < nt)
    def _():
        for l in range(n_lt):
            win_ref[l, CONV_PAD:CONV_PAD + tt, :] = u_ref[:, l * LANES:(l + 1) * LANES]

    def project(s):
        cols = slice(s * slab, (s + 1) * slab)
        out = _dot(y_ref[(t + 1) % 2], w2b_ref[s]) + b2_ref[:, cols]
        o_ref[:, cols] = x_ref[:, cols] + gt_ref[:, cols] * out

    def conv_block(l, r0):
        strided = [win_ref[l, pl.ds(r0 + r, SUBLANES, stride=SUBLANES), :]
                   for r in range(off, off + width - 1 + SUBLANES)]
        accs = [[jnp.broadcast_to(bdw_ref[l], (SUBLANES, LANES)), None] for _ in range(SUBLANES)]
        for tap in range(width):
            w_tap = jnp.broadcast_to(w_ref[l, tap:tap + 1, :], (SUBLANES, LANES))
            for t0 in range(SUBLANES):
                term = strided[t0 + tap] * w_tap
                prev = accs[t0][tap % 2]
                accs[t0][tap % 2] = term if prev is None else prev + term
        for t0 in range(SUBLANES):
            acc_ref[l, pl.ds(r0 + t0, SUBLANES, stride=SUBLANES), :] = accs[t0][0] + accs[t0][1]

    def conv_lane_tiles(lts):
        for l in lts:
            for r0 in range(0, tt, blk):
                conv_block(l, r0)

    @pl.when(t == 0)
    def _():
        def body(i, carry):
            conv_block(i // (tt // blk), pl.multiple_of((i % (tt // blk)) * blk, blk))
            return carry

        lax.fori_loop(0, n_lt * (tt // blk), body, 0, unroll=2)

    @pl.when((t > 0) & (t < nt))
    def _():
        for s in range(n_slab):
            project(s)
            conv_lane_tiles(range(s * lt_per_slab, (s + 1) * lt_per_slab))

    @pl.when(t == nt)
    def _():
        for s in range(n_slab):
            project(s)

    @pl.when(t < nt)
    def _():
        for l in range(n_lt):
            tail_ref[:, l * LANES:(l + 1) * LANES] = win_ref[l, CONV_PAD + tt - (width - 1):CONV_PAD + tt, :]
        win_ref[:, 0:CONV_PAD, :] = win_ref[:, tt:tt + CONV_PAD, :]
        y_cur = y_ref.at[t % 2]

        def fn(rows):
            a = jnp.concatenate([acc_ref[l, rows, :] for l in range(n_lt)], axis=1)
            mu = jnp.mean(a, axis=-1, keepdims=True)
            xc = a - mu
            yn = xc * lax.rsqrt(jnp.mean(xc * xc, axis=-1, keepdims=True) + EPS) * lng_ref[...] + lnb_ref[...]
            y_cur[rows, :] = _silu(yn).astype(BF16)

        _row_loop(tt, fn)


def _conv_seq_call(u, w_dw, b_dw, ln_g, ln_b, x, mod, w2, b2, *, layer, j, batch):
    m, ch = u.shape
    d = x.shape[1]
    t = m // batch
    tt = min(CONV_TILE, t)
    nt = t // tt
    width = w_dw.shape[1]
    n_lt = ch // LANES
    n_slab = d // CONV_SLAB
    vec = lambda a: a.reshape(a.shape[0], 1, a.shape[-1])
    vspec = pl.BlockSpec((None, 1, ch), lambda b, i: (j, 0, 0))
    w_lt = jnp.transpose(w_dw[j].reshape(width, n_lt, LANES), (1, 0, 2))
    b_lt = b_dw[j].reshape(n_lt, 1, LANES)
    conv_tile = lambda b, i: (b * nt + jnp.minimum(i, nt - 1), 0)
    proj_tile = lambda b, i: (b * nt + jnp.maximum(i - 1, 0), 0)
    return pl.pallas_call(
        functools.partial(_conv_seq_kernel, width=width),
        grid=(batch, nt + 1),
        in_specs=[
            pl.BlockSpec((tt, ch), conv_tile),
            pl.BlockSpec((n_lt, width, LANES), lambda b, i: (0, 0, 0)),
            pl.BlockSpec((n_lt, 1, LANES), lambda b, i: (0, 0, 0)),
            vspec, vspec,
            pl.BlockSpec((tt, d), proj_tile),
            pl.BlockSpec((None, None, 1, d), lambda b, i: (layer, b, 0, 3 * 1 + 2)),
            _once((None, ch, d), lambda b, i: (j, 0, 0)),
            pl.BlockSpec((None, 1, d), lambda b, i: (j, 0, 0)),
        ],
        out_specs=[
            pl.BlockSpec((tt, d), proj_tile),
            pl.BlockSpec((None, width - 1, ch), lambda b, i: (b, 0, 0)),
        ],
        out_shape=[jax.ShapeDtypeStruct((m, d), F32), jax.ShapeDtypeStruct((batch, width - 1, ch), F32)],
        scratch_shapes=[
            pltpu.VMEM((n_lt, CONV_PAD + tt, LANES), F32),
            pltpu.VMEM((n_lt, tt, LANES), F32),
            pltpu.VMEM((2, tt, ch), BF16),
            pltpu.VMEM((n_slab, ch, CONV_SLAB), BF16),
        ],
        compiler_params=_cparams("arbitrary", "arbitrary"),
        name="conv_seq",
    )(u, w_lt, b_lt, vec(ln_g), vec(ln_b), x, mod, w2, vec(b2))


def _conv_step_kernel(u_ref, buf_ref, w_ref, bdw_ref, lng_ref, lnb_ref, y_ref, new_ref, acc_ref, *, width):
    u = u_ref[...]
    acc = u * w_ref[width - 1:width, :] + bdw_ref[...]
    for tap in range(width - 1):
        acc = acc + buf_ref[tap] * w_ref[tap:tap + 1, :]
    acc_ref[...] = acc
    for tap in range(width - 2):
        new_ref[tap] = buf_ref[tap + 1]
    new_ref[width - 2] = u
    _ln_silu_into(y_ref, acc_ref, lng_ref, lnb_ref)


def _conv_step_call(u, buf, w_dw, b_dw, ln_g, ln_b, *, j, tb):
    nb, ch = u.shape
    width = w_dw.shape[1]
    vec = lambda a: a.reshape(a.shape[0], 1, ch)
    vspec = pl.BlockSpec((None, 1, ch), lambda i: (j, 0, 0))
    return pl.pallas_call(
        functools.partial(_conv_step_kernel, width=width),
        grid=(nb // tb,),
        in_specs=[
            pl.BlockSpec((tb, ch), lambda i: (i, 0)),
            pl.BlockSpec((None, width - 1, tb, ch), lambda i: (j, 0, i, 0)),
            pl.BlockSpec((None, width, ch), lambda i: (j, 0, 0)),
            vspec, vspec, vspec,
        ],
        out_specs=[
            pl.BlockSpec((tb, ch), lambda i: (i, 0)),
            pl.BlockSpec((width - 1, tb, ch), lambda i: (0, i, 0)),
        ],
        out_shape=[jax.ShapeDtypeStruct((nb, ch), BF16), jax.ShapeDtypeStruct((width - 1, nb, ch), F32)],
        scratch_shapes=[pltpu.VMEM((tb, ch), F32)],
        compiler_params=_cparams("arbitrary"),
        name="conv_step",
    )(u, buf, w_dw, vec(b_dw), vec(ln_g), vec(ln_b))


def kernel(x_prompt, x_sample, state_hgrn, state_conv, c_prompt, c_sample, hgrn_lb, hgrn_w_in, hgrn_b_f, hgrn_g_norm, hgrn_w_out, conv_w_pw1, conv_b_pw1, conv_w_dw, conv_b_dw, conv_ln_g, conv_ln_b, conv_w_pw2, conv_b_pw2, ada_w, ada_b, norm_g, ffn_w_in, ffn_w_out, final_g):
    bp, t, d = x_prompt.shape
    bs = x_sample.shape[0]
    depth = ada_w.shape[0]
    heads = d // LANES
    time_major = lambda s: jnp.transpose(s, (0, 2, 1, 3))

    mod_s, mod_p = _ada_call(c_sample, c_prompt, ada_w, ada_b)
    mod_s = mod_s.reshape(depth, 1, bs, N_SUB * 3 * d)
    mod_p = mod_p.reshape(depth, bp, 1, N_SUB * 3 * d)
    norm_g4 = norm_g.reshape(depth, N_SUB, 1, d)
    conv_buf = time_major(state_conv)

    x, xs = x_prompt.reshape(bp * t, d), x_sample.reshape(bs, d)
    hgrn_p, hgrn_s, conv_p, conv_s = [], [], [], []
    for layer in range(depth):
        x, xs = _ffn_call(x, xs, mod_p, mod_s, norm_g4, ffn_w_in, ffn_w_out, final_g,
                          layer=layer, sub=0, j=0, final=False)
        j = layer // 2
        if layer % 2 == 0:
            outs = _hgrn_in_call(x, xs, mod_p, mod_s, norm_g4, hgrn_w_in, hgrn_b_f, hgrn_lb,
                                 layer=layer, j=j, lb_row=layer)
            y, s = _hgrn_scan_call(*outs[:5], hgrn_g_norm[j], batch=bp, heads=heads)
            ys, ss = _hgrn_step_call(*outs[5:], hgrn_g_norm[j], state_hgrn[j], heads=heads)
            hgrn_p.append(s)
            hgrn_s.append(ss)
            x, xs = _proj_call(y, ys, hgrn_w_out, None, x, xs, mod_p, mod_s, layer=layer, j=j)
        else:
            u, us = _pw1_call(x, xs, mod_p, mod_s, norm_g4, conv_w_pw1, conv_b_pw1, layer=layer, j=j)
            x, s = _conv_seq_call(u, conv_w_dw, conv_b_dw, conv_ln_g, conv_ln_b, x, mod_p, conv_w_pw2, conv_b_pw2,
                                  layer=layer, j=j, batch=bp)
            ys, ss = _conv_step_call(us, conv_buf, conv_w_dw, conv_b_dw, conv_ln_g, conv_ln_b, j=j, tb=4 * SUBLANES)
            conv_p.append(jnp.transpose(s, (1, 0, 2)))
            conv_s.append(ss)
            xs = _proj_rows_call(ys, conv_w_pw2, conv_b_pw2, xs, mod_s, layer=layer, j=j)
        x, xs = _ffn_call(x, xs, mod_p, mod_s, norm_g4, ffn_w_in, ffn_w_out, final_g,
                          layer=layer, sub=2, j=1, final=(layer == depth - 1))
    return (x.reshape(bp, t, d), xs.reshape(bs, 1, d), jnp.stack(hgrn_p), jnp.stack(hgrn_s),
            time_major(jnp.stack(conv_p)), time_major(jnp.stack(conv_s)))
```

```python
import functools

import jax
import jax.numpy as jnp
from jax import lax
from jax.experimental import pallas as pl
from jax.experimental.pallas import tpu as pltpu

F32 = jnp.float32
BF16 = jnp.bfloat16
EPS = 1e-6
N_SUB = 3
LANES = 128
SUBLANES = 8
VMEM_LIMIT_BYTES = 60 * 1024 * 1024

ROW_TILE = 1024
FFN_TILE = 256
IN_TILE = 256
PW1_TILE = 512
OUT_TILE = 2048
OUT_ROW_TILE = 512
SCAN_ROWS = 256
SCAN_SUB = 64
SCAN_SPAN_MAX = 120.0
SCAN_SLOW_ROWS = 16
CONV_PAD = 32
CONV_TILE = 256
CONV_SLAB = 256
NORM_ROWS = 16
NORM_UNROLL = 8


def _cparams(*sem):
    return pltpu.CompilerParams(dimension_semantics=sem, vmem_limit_bytes=VMEM_LIMIT_BYTES)


def _dot(a, b):
    return jnp.dot(a, b, preferred_element_type=F32)


def _dot_nt(a, b):
    return lax.dot_general(a, b, (((1,), (1,)), ((), ())), preferred_element_type=F32)


def _dot_tn(a, b):
    return lax.dot_general(a, b, (((0,), (0,)), ((), ())), preferred_element_type=F32)


def _sigmoid(x):
    return 1.0 / (1.0 + jnp.exp(-x))


def _silu(x):
    return x * _sigmoid(x)


def _rmsnorm(x, g):
    return x * lax.rsqrt(jnp.mean(x * x, axis=-1, keepdims=True) + EPS) * g


def _row_loop(n_rows, fn, in_place=False):
    rows = min(NORM_ROWS * NORM_UNROLL if in_place else NORM_ROWS, n_rows)
    steps = n_rows // rows
    if steps == 1:
        fn(pl.ds(0, rows))
        return

    def body(i, carry):
        fn(pl.ds(pl.multiple_of(i * rows, rows), rows))
        return carry

    lax.fori_loop(0, steps, body, 0, unroll=NORM_UNROLL if not in_place and steps % NORM_UNROLL == 0 else 1)


def _mod_rows(ref, rows):
    return ref[...] if ref.shape[0] == 1 else ref[rows, :]


def _modulate_into(h_ref, x_ref, g_ref, sh_ref, sc_ref):
    per_tile = sc_ref.shape[0] == 1
    gain_tile = g_ref[...] * (1.0 + sc_ref[...]) if per_tile else None

    def fn(rows):
        x = x_ref[rows, :]
        gain = gain_tile if per_tile else g_ref[...] * (1.0 + sc_ref[rows, :])
        y = x * lax.rsqrt(jnp.mean(x * x, axis=-1, keepdims=True) + EPS) * gain + _mod_rows(sh_ref, rows)
        h_ref[rows, :] = y.astype(BF16)

    _row_loop(x_ref.shape[0], fn)


def _once(shape, index_map):
    return pl.BlockSpec(shape, index_map, pipeline_mode=pl.Buffered(1))


def _mod_specs(mod, layer, secs, width, col_of, row_tiles):
    groups, r = mod.shape[1], mod.shape[2]
    tiles_per_group = max(row_tiles // groups, 1)
    ncol = mod.shape[3] // (N_SUB * 3) // width

    def spec(sec):
        def index_map(*idx):
            i, n = col_of(*idx)
            return (layer, i // tiles_per_group, 0, sec * ncol + n)
        if groups == 1 and ncol == 1:
            return _once((None, None, r, width), index_map)
        return pl.BlockSpec((None, None, r, width), index_map)

    return [spec(s) for s in secs]


def _ada_kernel(cs_ref, cp_ref, w_ref, b_ref, os_ref, op_ref):
    w = w_ref[...].astype(BF16)
    os_ref[...] = _dot(_silu(cs_ref[...]).astype(BF16), w) + b_ref[...]
    op_ref[...] = _dot(_silu(cp_ref[...]).astype(BF16), w) + b_ref[...]


def _ada_call(c_sample, c_prompt, ada_w, ada_b, tn=1024):
    depth, d, n = ada_w.shape
    rs, rp = c_sample.shape[0], c_prompt.shape[0]
    return pl.pallas_call(
        _ada_kernel,
        grid=(depth, n // tn),
        in_specs=[
            pl.BlockSpec((rs, d), lambda l, j: (0, 0)),
            pl.BlockSpec((rp, d), lambda l, j: (0, 0)),
            pl.BlockSpec((None, d, tn), lambda l, j: (l, 0, j)),
            pl.BlockSpec((None, 1, tn), lambda l, j: (l, 0, j)),
        ],
        out_specs=[
            pl.BlockSpec((None, rs, tn), lambda l, j: (l, 0, j)),
            pl.BlockSpec((None, rp, tn), lambda l, j: (l, 0, j)),
        ],
        out_shape=[jax.ShapeDtypeStruct((depth, rs, n), F32), jax.ShapeDtypeStruct((depth, rp, n), F32)],
        compiler_params=_cparams("arbitrary", "arbitrary"),
        name="adaln_params",
    )(c_sample, c_prompt, ada_w, ada_b.reshape(depth, 1, n))


def _ffn_kernel(x_ref, sh_ref, sc_ref, gt_ref, xs_ref, shs_ref, scs_ref, gts_ref, g_ref, wa_ref, wu_ref, wo_ref,
                *rest, final):
    fg_ref = rest[0] if final else None
    o_ref, os_ref, h_ref = rest[-3:]
    tm, ms = x_ref.shape[0], xs_ref.shape[0]
    first_tile = pl.program_id(0) == 0
    f = pl.program_id(1)
    last = pl.num_programs(1) - 1

    def start(x_ref, sh_ref, sc_ref, o_ref, row0):
        _modulate_into(h_ref.at[pl.ds(row0, x_ref.shape[0])], x_ref, g_ref, sh_ref, sc_ref)
        o_ref[...] = jnp.zeros_like(o_ref)

    def finish(x_ref, gt_ref, o_ref):
        def fn(rows):
            y = x_ref[rows, :] + 0.5 * _mod_rows(gt_ref, rows) * o_ref[rows, :]
            if fg_ref is not None:
                y = _rmsnorm(y, fg_ref[...])
            o_ref[rows, :] = y

        _row_loop(x_ref.shape[0], fn, in_place=True)

    def hidden(rows):
        h = h_ref[0:rows, :]
        return (_silu(_dot(h, wa_ref[...].astype(BF16))) * _dot(h, wu_ref[...].astype(BF16))).astype(BF16)

    @pl.when(f == 0)
    def _():
        start(x_ref, sh_ref, sc_ref, o_ref, 0)

    @pl.when((f == 0) & first_tile)
    def _():
        start(xs_ref, shs_ref, scs_ref, os_ref, tm)

    @pl.when(jnp.logical_not(first_tile))
    def _():
        o_ref[...] += _dot(hidden(tm), wo_ref[...].astype(BF16))

    @pl.when(first_tile)
    def _():
        act = hidden(tm + ms)
        wo = wo_ref[...].astype(BF16)
        o_ref[...] += _dot(act[0:tm], wo)
        os_ref[...] += _dot(act[tm:tm + ms], wo)

    @pl.when(f == last)
    def _():
        finish(x_ref, gt_ref, o_ref)

    @pl.when((f == last) & first_tile)
    def _():
        finish(xs_ref, gts_ref, os_ref)


def _ffn_call(x, xs, mod, mod_s, norm_g, w_in, w_out, final_g, *, layer, sub, j, final):
    m, d = x.shape
    ms = xs.shape[0]
    tm, tf = ROW_TILE, FFN_TILE
    nf = w_out.shape[2] // tf
    secs = [3 * sub, 3 * sub + 1, 3 * sub + 2]
    in_specs = (
        [pl.BlockSpec((tm, d), lambda i, f: (i, 0))]
        + _mod_specs(mod, layer, secs, d, lambda i, f: (i, 0), m // tm)
        + [_once((ms, d), lambda i, f: (0, 0))]
        + _mod_specs(mod_s, layer, secs, d, lambda i, f: (0, 0), 1)
        + [
            pl.BlockSpec((None, None, 1, d), lambda i, f: (layer, sub, 0, 0)),
            pl.BlockSpec((None, None, d, tf), lambda i, f: (layer, j, 0, f)),
            pl.BlockSpec((None, None, d, tf), lambda i, f: (layer, j, 0, nf + f)),
            pl.BlockSpec((None, None, tf, d), lambda i, f: (layer, j, f, 0)),
        ]
    )
    args = [x, mod, mod, mod, xs, mod_s, mod_s, mod_s, norm_g, w_in, w_in, w_out]
    if final:
        in_specs.append(pl.BlockSpec((1, d), lambda i, f: (0, 0)))
        args.append(final_g.reshape(1, d))
    return pl.pallas_call(
        functools.partial(_ffn_kernel, final=final),
        grid=(m // tm, nf),
        in_specs=in_specs,
        out_specs=[pl.BlockSpec((tm, d), lambda i, f: (i, 0)), _once((ms, d), lambda i, f: (0, 0))],
        out_shape=[jax.ShapeDtypeStruct((m, d), F32), jax.ShapeDtypeStruct((ms, d), F32)],
        scratch_shapes=[pltpu.VMEM((tm + ms, d), BF16)],
        compiler_params=_cparams("arbitrary", "arbitrary"),
        name="ffn",
    )(*args)


def _hgrn_in_rows(x_ref, sh_ref, sc_ref, g_ref, bf_ref, h_ref, outs, ws, lb):
    q_ref, k_ref, v_ref, lf_ref, sg_ref = outs
    wq_ref, wf_ref, wi_ref, wg_ref = ws

    @pl.when(pl.program_id(1) == 0)
    def _():
        _modulate_into(h_ref, x_ref, g_ref, sh_ref, sc_ref)

    h = h_ref[...]
    q_ref[...] = _dot(h, wq_ref[...].astype(BF16))
    z = _dot(h, wf_ref[...].astype(BF16)) + bf_ref[...]
    lf_ref[...] = jnp.log(lb + (1.0 - lb) * _sigmoid(z))
    k_ref[...] = (1.0 - lb) * _sigmoid(-z)
    v_ref[...] = _dot(h, wi_ref[...].astype(BF16)).astype(BF16)
    sg_ref[...] = _silu(_dot(h, wg_ref[...].astype(BF16)))


def _hgrn_in_kernel(x_ref, sh_ref, sc_ref, xs_ref, shs_ref, scs_ref, g_ref, wq_ref, wf_ref, wi_ref, wg_ref,
                    bf_ref, lb_ref, *rest, lb_row):
    outs, outs_s, (h_ref, hs_ref) = rest[0:5], rest[5:10], rest[10:12]
    raw = [lb_ref[i:i + 1, :] for i in range(lb_ref.shape[0])]
    top = functools.reduce(jnp.maximum, raw)
    e = [jnp.exp(r - top) for r in raw]
    lb = sum(e[:lb_row + 1]) / sum(e)
    ws = (wq_ref, wf_ref, wi_ref, wg_ref)
    _hgrn_in_rows(x_ref, sh_ref, sc_ref, g_ref, bf_ref, h_ref, outs, ws, lb)

    @pl.when(pl.program_id(0) == 0)
    def _():
        _hgrn_in_rows(xs_ref, shs_ref, scs_ref, g_ref, bf_ref, hs_ref, outs_s, ws, lb)


def _side_col(nn):
    return lambda i, n: (0, jnp.where(i == 0, n, nn - 1))


def _hgrn_in_call(x, xs, mod, mod_s, norm_g, w_in, b_f, lb_raw, *, layer, j, lb_row):
    m, d = x.shape
    ms = xs.shape[0]
    tm, tn = ROW_TILE, IN_TILE
    nn = d // tn
    sub = 1
    secs = [3 * sub, 3 * sub + 1]

    def wspec(sec):
        return pl.BlockSpec((None, d, tn), lambda i, n: (j, 0, sec * nn + n))

    def out_shapes(rows):
        return [jax.ShapeDtypeStruct((rows, d), dt) for dt in (F32, F32, BF16, F32, F32)]

    return pl.pallas_call(
        functools.partial(_hgrn_in_kernel, lb_row=lb_row),
        grid=(m // tm, nn),
        in_specs=(
            [pl.BlockSpec((tm, d), lambda i, n: (i, 0))]
            + _mod_specs(mod, layer, secs, d, lambda i, n: (i, 0), m // tm)
            + [_once((ms, d), lambda i, n: (0, 0))]
            + _mod_specs(mod_s, layer, secs, d, lambda i, n: (0, 0), 1)
            + [
                pl.BlockSpec((None, None, 1, d), lambda i, n: (layer, sub, 0, 0)),
                wspec(0), wspec(1), wspec(2), wspec(3),
                pl.BlockSpec((None, 1, tn), lambda i, n: (j, 0, n)),
                pl.BlockSpec((lb_raw.shape[0], tn), lambda i, n: (0, n)),
            ]
        ),
        out_specs=[pl.BlockSpec((tm, tn), lambda i, n: (i, n))] * 5 + [pl.BlockSpec((ms, tn), _side_col(nn))] * 5,
        out_shape=out_shapes(m) + out_shapes(ms),
        scratch_shapes=[pltpu.VMEM((tm, d), BF16), pltpu.VMEM((ms, d), BF16)],
        compiler_params=_cparams("arbitrary", "arbitrary"),
        name="hgrn_in",
    )(x, mod, mod, xs, mod_s, mod_s, norm_g, w_in, w_in, w_in, w_in, b_f.reshape(b_f.shape[0], 1, d), lb_raw)


def _hgrn_scan_kernel(q_ref, k_ref, v_ref, lf_ref, sg_ref, gn_ref, y_ref, sfin_ref, st_ref, b_ref, o_ref, *,
                      heads, span_max):
    c = pl.program_id(1)
    rows, d = q_ref.shape
    dk = d // heads
    sub = min(SCAN_SUB, rows)
    n_sub = rows // sub
    n_levels = n_sub.bit_length() - 1
    sub_shift = sub.bit_length() - 1

    @pl.when(c == 0)
    def _():
        st_ref[...] = jnp.zeros_like(st_ref)

    same_head = ((lax.broadcasted_iota(jnp.int32, (2 * dk, 2 * dk), 0) < dk)
                 == (lax.broadcasted_iota(jnp.int32, (2 * dk, 2 * dk), 1) < dk))
    row = lax.broadcasted_iota(jnp.int32, (rows, rows), 0)
    col = lax.broadcasted_iota(jnp.int32, (rows, rows), 1)
    tri = jnp.where(col <= row, 1.0, 0.0).astype(BF16)
    lf = lf_ref[...]
    p1 = lf.astype(BF16)
    r1 = lf - p1.astype(F32)
    p2 = r1.astype(BF16)
    p3 = (r1 - p2.astype(F32)).astype(BF16)
    b_ref[...] = _dot(tri, p1) + _dot(tri, p2) + _dot(tri, p3)

    def first_row(j):
        return slice(j * sub, j * sub + 1)

    def last_row(j):
        return slice((j + 1) * sub - 1, (j + 1) * sub)

    spans = [b_ref[first_row(j), :] - b_ref[last_row(j), :] for j in range(n_sub)]
    safe = jnp.max(functools.reduce(jnp.maximum, spans)) <= span_max

    def gated_out(o, sl):
        return (_rmsnorm(o, gn_ref[:, sl]) * sg_ref[:, sl]).astype(BF16)

    @pl.when(safe)
    def _():
        srl = lax.shift_right_logical
        level = jnp.where((srl(row, sub_shift) == srl(col, sub_shift)) & (col <= row), 0, -1)
        later_half = []
        row_d = lax.broadcasted_iota(jnp.int32, (rows, dk), 0)
        for l in range(1, n_levels + 1):
            s = sub_shift + l - 1
            hit = (srl(row, s + 1) == srl(col, s + 1)) & ((srl(row, s) & 1) == 1) & ((srl(col, s) & 1) == 0)
            level = jnp.where(hit, l, level)
            later_half.append((srl(row_d, s) & 1) == 1)

        def rows_of(pieces, n):
            return jnp.concatenate([jnp.broadcast_to(p, (n, dk)) for p in pieces], axis=0)

        def scores(sl):
            b = b_ref[:, sl]
            q = q_ref[:, sl]
            k = k_ref[:, sl]
            rel = b - rows_of([0.5 * (b_ref[first_row(j), sl] + b_ref[last_row(j), sl]) for j in range(n_sub)], sub)
            att = jnp.where(level == 0,
                            _dot_nt((q * jnp.exp(rel)).astype(BF16), (k * jnp.exp(-rel)).astype(BF16)), 0.0)
            for l in range(1, n_levels + 1):
                m = sub << (l - 1)
                rel = b - rows_of([b_ref[p * 2 * m + m:p * 2 * m + m + 1, sl] for p in range(rows // (2 * m))], 2 * m)
                e = jnp.exp(jnp.minimum(jnp.where(later_half[l - 1], rel, -rel), 0.0))
                att = jnp.where(level == l, _dot_nt((q * e).astype(BF16), (k * e).astype(BF16)), att)
            return att.astype(BF16)

        first_head = lax.broadcasted_iota(jnp.int32, (rows, 2 * dk), 1) < dk
        for pair in range(heads // 2):
            sl0 = slice(2 * pair * dk, (2 * pair + 1) * dk)
            sl1 = slice((2 * pair + 1) * dk, (2 * pair + 2) * dk)
            both = slice(2 * pair * dk, (2 * pair + 2) * dk)
            att = jnp.concatenate([scores(sl0), scores(sl1)], axis=1)
            v = v_ref[:, both]
            zero = jnp.zeros_like(v)
            v_split = jnp.concatenate([jnp.where(first_head, v, zero), jnp.where(first_head, zero, v)], axis=0)
            b = b_ref[:, both]
            b_last = b_ref[rows - 1:rows, both]
            q_st = (q_ref[:, both] * jnp.exp(b)).astype(BF16)
            k_st = (k_ref[:, both] * jnp.exp(b_last - b)).astype(BF16)
            st = st_ref[pair]
            o = _dot(att, v_split) + _dot_nt(q_st, st.astype(BF16))
            st_ref[pair] = st * jnp.exp(b_last) + jnp.where(same_head, _dot_tn(v, k_st), 0.0)
            y_ref[:, sl0] = gated_out(o[:, 0:dk], sl0)
            y_ref[:, sl1] = gated_out(o[:, dk:2 * dk], sl1)

    @pl.when(jnp.logical_not(safe))
    def _():
        grp_rows = min(SCAN_SLOW_ROWS, rows)
        grp_shift = grp_rows.bit_length() - 1
        pick = lax.broadcasted_iota(jnp.int32, (grp_rows, 2 * dk), 0)
        o_ref[...] = jnp.zeros_like(o_ref)

        def token(t, carry):
            r0 = pl.multiple_of(lax.shift_left(lax.shift_right_logical(t, grp_shift), grp_shift), grp_rows)
            sel = pick == t - r0
            grp = pl.ds(r0, grp_rows)
            for pair in range(heads // 2):
                sl = slice(2 * pair * dk, (2 * pair + 2) * dk)
                f = jnp.sum(jnp.where(sel, jnp.exp(lf_ref[grp, sl]), 0.0), axis=0, keepdims=True)
                q1 = jnp.where(sel, q_ref[grp, sl], 0.0).astype(BF16)
                k1 = jnp.where(sel, k_ref[grp, sl], 0.0).astype(BF16)
                v1 = jnp.where(sel, v_ref[grp, sl].astype(F32), 0.0).astype(BF16)
                st = st_ref[pair] * f + jnp.where(same_head, _dot_tn(v1, k1), 0.0)
                st_ref[pair] = st
                o_ref[grp, sl] = jnp.where(sel, _dot_nt(q1, st.astype(BF16)), o_ref[grp, sl])
            return carry

        lax.fori_loop(0, rows, token, 0)
        for h in range(heads):
            sl = slice(h * dk, (h + 1) * dk)
            y_ref[:, sl] = gated_out(o_ref[:, sl], sl)

    @pl.when(c == pl.num_programs(1) - 1)
    def _():
        for h in range(heads):
            lo = (h % 2) * dk
            sfin_ref[h] = st_ref[h // 2, lo:lo + dk, lo:lo + dk].T


def _hgrn_scan_call(q, k, v, lf, sg, g_norm, *, batch, heads, span_max=SCAN_SPAN_MAX):
    m, d = q.shape
    t = m // batch
    rows = min(SCAN_ROWS, t)
    nc = t // rows
    dk = d // heads
    spec = pl.BlockSpec((rows, d), lambda b, c: (b * nc + c, 0))
    return pl.pallas_call(
        functools.partial(_hgrn_scan_kernel, heads=heads, span_max=span_max),
        grid=(batch, nc),
        in_specs=[spec, spec, spec, spec, spec, pl.BlockSpec((1, d), lambda b, c: (0, 0))],
        out_specs=[spec, pl.BlockSpec((None, heads, dk, dk), lambda b, c: (b, 0, 0, 0))],
        out_shape=[jax.ShapeDtypeStruct((m, d), BF16), jax.ShapeDtypeStruct((batch, heads, dk, dk), F32)],
        scratch_shapes=[pltpu.VMEM((heads // 2, 2 * dk, 2 * dk), F32), pltpu.VMEM((rows, d), F32),
                        pltpu.VMEM((rows, d), F32)],
        compiler_params=_cparams("arbitrary", "arbitrary"),
        name="hgrn_scan",
    )(q, k, v, lf, sg, g_norm.reshape(1, d))


def _hgrn_step_kernel(q_ref, k_ref, v_ref, lf_ref, sg_ref, gn_ref, s0_ref, y_ref, s1_ref, o_ref):
    nb = q_ref.shape[0]
    f_t = jnp.exp(lf_ref[...]).T
    k = k_ref[...].astype(BF16)
    v = v_ref[...]
    q = q_ref[...].astype(BF16)
    for b in range(nb):
        s1 = s0_ref[b] * f_t[:, b:b + 1] + _dot_tn(k[b:b + 1, :], v[b:b + 1, :])
        s1_ref[b] = s1
        o_ref[b:b + 1, :] = _dot(q[b:b + 1, :], s1.astype(BF16))
    y = _rmsnorm(o_ref[...], gn_ref[...]) * sg_ref[...]
    y_ref[...] = y.astype(BF16)


def _hgrn_step_call(q, k, v, lf, sg, g_norm, s0, *, heads):
    nb, d = q.shape
    dk = d // heads
    spec = pl.BlockSpec((nb, dk), lambda h: (0, h))
    sspec = pl.BlockSpec((nb, None, dk, dk), lambda h: (0, h, 0, 0))
    return pl.pallas_call(
        _hgrn_step_kernel,
        grid=(heads,),
        in_specs=[spec, spec, spec, spec, spec, pl.BlockSpec((1, dk), lambda h: (0, h)), sspec],
        out_specs=[spec, sspec],
        out_shape=[jax.ShapeDtypeStruct((nb, d), BF16), jax.ShapeDtypeStruct(s0.shape, F32)],
        scratch_shapes=[pltpu.VMEM((nb, dk), F32)],
        compiler_params=_cparams("arbitrary"),
        name="hgrn_step",
    )(q, k, v, lf, sg, g_norm.reshape(1, d), s0)


def _proj_kernel(y_ref, x_ref, gt_ref, ys_ref, xs_ref, gts_ref, w_ref, *rest, has_bias):
    b_ref = rest[0] if has_bias else None
    o_ref, os_ref = rest[-2:]
    def rows(y_ref, x_ref, gt_ref, o_ref):
        out = _dot(y_ref[...], w_ref[...].astype(BF16))
        if has_bias:
            out = out + b_ref[...]
        o_ref[...] = x_ref[...] + gt_ref[...] * out

    rows(y_ref, x_ref, gt_ref, o_ref)

    @pl.when(pl.program_id(1) == 0)
    def _():
        rows(ys_ref, xs_ref, gts_ref, os_ref)


def _proj_call(y, ys, w, bias, x, xs, mod, mod_s, *, layer, j):
    m, kdim = y.shape
    ms = ys.shape[0]
    d = x.shape[1]
    tm, tn = OUT_ROW_TILE, min(OUT_TILE, d)
    gate = [3 * 1 + 2]
    has_bias = bias is not None
    wspec = _once if tn == d else pl.BlockSpec
    in_specs = (
        [pl.BlockSpec((tm, kdim), lambda n, i: (i, 0)), pl.BlockSpec((tm, tn), lambda n, i: (i, n))]
        + _mod_specs(mod, layer, gate, tn, lambda n, i: (i, n), m // tm)
        + [_once((ms, kdim), lambda n, i: (0, 0)), pl.BlockSpec((ms, tn), lambda n, i: (0, n))]
        + _mod_specs(mod_s, layer, gate, tn, lambda n, i: (0, n), 1)
        + [wspec((None, kdim, tn), lambda n, i: (j, 0, n))]
    )
    args = [y, x, mod, ys, xs, mod_s, w]
    if has_bias:
        in_specs.append(pl.BlockSpec((None, 1, tn), lambda n, i: (j, 0, n)))
        args.append(bias.reshape(bias.shape[0], 1, d))
    return pl.pallas_call(
        functools.partial(_proj_kernel, has_bias=has_bias),
        grid=(d // tn, m // tm),
        in_specs=in_specs,
        out_specs=[pl.BlockSpec((tm, tn), lambda n, i: (i, n)), pl.BlockSpec((ms, tn), lambda n, i: (0, n))],
        out_shape=[jax.ShapeDtypeStruct((m, d), F32), jax.ShapeDtypeStruct((ms, d), F32)],
        compiler_params=_cparams("arbitrary", "arbitrary"),
        name="proj_residual",
    )(*args)


def _proj_rows_kernel(y_ref, x_ref, gt_ref, w_ref, b_ref, o_ref):
    o_ref[...] = x_ref[...] + gt_ref[...] * (_dot(y_ref[...], w_ref[...].astype(BF16)) + b_ref[...])


def _proj_rows_call(y, w, bias, x, mod, *, layer, j):
    ms, kdim = y.shape
    d = x.shape[1]
    tn = IN_TILE
    return pl.pallas_call(
        _proj_rows_kernel,
        grid=(d // tn,),
        in_specs=(
            [_once((ms, kdim), lambda n: (0, 0)), pl.BlockSpec((ms, tn), lambda n: (0, n))]
            + _mod_specs(mod, layer, [3 * 1 + 2], tn, lambda n: (0, n), 1)
            + [pl.BlockSpec((None, kdim, tn), lambda n: (j, 0, n)), pl.BlockSpec((None, 1, tn), lambda n: (j, 0, n))]
        ),
        out_specs=pl.BlockSpec((ms, tn), lambda n: (0, n)),
        out_shape=jax.ShapeDtypeStruct((ms, d), F32),
        compiler_params=_cparams("arbitrary"),
        name="proj_rows",
    )(y, x, mod, w, bias.reshape(bias.shape[0], 1, d))


def _pw1_kernel(x_ref, sh_ref, sc_ref, xs_ref, shs_ref, scs_ref, g_ref, wa_ref, wg_ref, ba_ref, bg_ref,
                u_ref, us_ref, h_ref, hs_ref):
    def rows(x_ref, sh_ref, sc_ref, u_ref, h_ref):
        @pl.when(pl.program_id(1) == 0)
        def _():
            _modulate_into(h_ref, x_ref, g_ref, sh_ref, sc_ref)

        h = h_ref[...]
        a = _dot(h, wa_ref[...].astype(BF16)) + ba_ref[...]
        u_ref[...] = a * _sigmoid(_dot(h, wg_ref[...].astype(BF16)) + bg_ref[...])

    rows(x_ref, sh_ref, sc_ref, u_ref, h_ref)

    @pl.when(pl.program_id(0) == 0)
    def _():
        rows(xs_ref, shs_ref, scs_ref, us_ref, hs_ref)


def _pw1_call(x, xs, mod, mod_s, norm_g, w, bias, *, layer, j):
    m, d = x.shape
    ms = xs.shape[0]
    tm, tn = ROW_TILE, PW1_TILE
    ch = w.shape[2] // 2
    nn = ch // tn
    sub = 1
    secs = [3 * sub, 3 * sub + 1]

    def wspec(sec):
        return pl.BlockSpec((None, d, tn), lambda i, n: (j, 0, sec * nn + n))

    def bspec(sec):
        return pl.BlockSpec((None, 1, tn), lambda i, n: (j, 0, sec * nn + n))

    bias3 = bias.reshape(bias.shape[0], 1, 2 * ch)
    return pl.pallas_call(
        _pw1_kernel,
        grid=(m // tm, nn),
        in_specs=(
            [pl.BlockSpec((tm, d), lambda i, n: (i, 0))]
            + _mod_specs(mod, layer, secs, d, lambda i, n: (i, 0), m // tm)
            + [_once((ms, d), lambda i, n: (0, 0))]
            + _mod_specs(mod_s, layer, secs, d, lambda i, n: (0, 0), 1)
            + [
                pl.BlockSpec((None, None, 1, d), lambda i, n: (layer, sub, 0, 0)),
                wspec(0), wspec(1), bspec(0), bspec(1),
            ]
        ),
        out_specs=[pl.BlockSpec((tm, tn), lambda i, n: (i, n)), pl.BlockSpec((ms, tn), _side_col(nn))],
        out_shape=[jax.ShapeDtypeStruct((m, ch), F32), jax.ShapeDtypeStruct((ms, ch), F32)],
        scratch_shapes=[pltpu.VMEM((tm, d), BF16), pltpu.VMEM((ms, d), BF16)],
        compiler_params=_cparams("arbitrary", "arbitrary"),
        name="conv_pw1",
    )(x, mod, mod, xs, mod_s, mod_s, norm_g, w, w, bias3, bias3)


def _ln_silu_into(y_ref, acc_ref, lng_ref, lnb_ref):
    def fn(rows):
        a = acc_ref[rows, :]
        mu = jnp.mean(a, axis=-1, keepdims=True)
        xc = a - mu
        yn = xc * lax.rsqrt(jnp.mean(xc * xc, axis=-1, keepdims=True) + EPS) * lng_ref[...] + lnb_ref[...]
        y_ref[rows, :] = _silu(yn).astype(BF16)

    _row_loop(acc_ref.shape[0], fn)


def _conv_seq_kernel(u_ref, w_ref, bdw_ref, lng_ref, lnb_ref, x_ref, gt_ref, w2_ref, b2_ref, o_ref, tail_ref,
                     win_ref, acc_ref, y_ref, w2b_ref, *, width):
    t = pl.program_id(1)
    nt = pl.num_programs(1) - 1
    tt, ch = u_ref.shape
    n_lt = ch // LANES
    off = CONV_PAD - (width - 1)
    blk = SUBLANES * SUBLANES
    n_slab, _, slab = w2b_ref.shape
    lt_per_slab = n_lt // n_slab

    @pl.when((pl.program_id(0) == 0) & (t == 0))
    def _():
        for s in range(n_slab):
            w2b_ref[s] = w2_ref[:, s * slab:(s + 1) * slab].astype(BF16)

    @pl.when(t == 0)
    def _():
        win_ref[:, 0:CONV_PAD, :] = jnp.zeros((n_lt, CONV_PAD, LANES), F32)

    @pl.when(t < nt)
    def _():
        for l in range(n_lt):
            win_ref[l, CONV_PAD:CONV_PAD + tt, :] = u_ref[:, l * LANES:(l + 1) * LANES]

    def project(s):
        cols = slice(s * slab, (s + 1) * slab)
        out = _dot(y_ref[(t + 1) % 2], w2b_ref[s]) + b2_ref[:, cols]
        o_ref[:, cols] = x_ref[:, cols] + gt_ref[:, cols] * out

    def conv_block(l, r0):
        strided = [win_ref[l, pl.ds(r0 + r, SUBLANES, stride=SUBLANES), :]
                   for r in range(off, off + width - 1 + SUBLANES)]
        accs = [[jnp.broadcast_to(bdw_ref[l], (SUBLANES, LANES)), None] for _ in range(SUBLANES)]
        for tap in range(width):
            w_tap = jnp.broadcast_to(w_ref[l, tap:tap + 1, :], (SUBLANES, LANES))
            for t0 in range(SUBLANES):
                term = strided[t0 + tap] * w_tap
                prev = accs[t0][tap % 2]
                accs[t0][tap % 2] = term if prev is None else prev + term
        for t0 in range(SUBLANES):
            acc_ref[l, pl.ds(r0 + t0, SUBLANES, stride=SUBLANES), :] = accs[t0][0] + accs[t0][1]

    def conv_lane_tiles(lts):
        for l in lts:
            for r0 in range(0, tt, blk):
                conv_block(l, r0)

    @pl.when(t == 0)
    def _():
        def body(i, carry):
            conv_block(i // (tt // blk), pl.multiple_of((i % (tt // blk)) * blk, blk))
            return carry

        lax.fori_loop(0, n_lt * (tt // blk), body, 0, unroll=2)

    @pl.when((t > 0) & (t < nt))
    def _():
        for s in range(n_slab):
            project(s)
            conv_lane_tiles(range(s * lt_per_slab, (s + 1) * lt_per_slab))

    @pl.when(t == nt)
    def _():
        for s in range(n_slab):
            project(s)

    @pl.when(t < nt)
    def _():
        for l in range(n_lt):
            tail_ref[:, l * LANES:(l + 1) * LANES] = win_ref[l, CONV_PAD + tt - (width - 1):CONV_PAD + tt, :]
        win_ref[:, 0:CONV_PAD, :] = win_ref[:, tt:tt + CONV_PAD, :]
        y_cur = y_ref.at[t % 2]

        def fn(rows):
            a = jnp.concatenate([acc_ref[l, rows, :] for l in range(n_lt)], axis=1)
            mu = jnp.mean(a, axis=-1, keepdims=True)
            xc = a - mu
            yn = xc * lax.rsqrt(jnp.mean(xc * xc, axis=-1, keepdims=True) + EPS) * lng_ref[...] + lnb_ref[...]
            y_cur[rows, :] = _silu(yn).astype(BF16)

        _row_loop(tt, fn)


def _conv_seq_call(u, w_dw, b_dw, ln_g, ln_b, x, mod, w2, b2, *, layer, j, batch):
    m, ch = u.shape
    d = x.shape[1]
    t = m // batch
    tt = min(CONV_TILE, t)
    nt = t // tt
    width = w_dw.shape[1]
    n_lt = ch // LANES
    n_slab = d // CONV_SLAB
    vec = lambda a: a.reshape(a.shape[0], 1, a.shape[-1])
    vspec = pl.BlockSpec((None, 1, ch), lambda b, i: (j, 0, 0))
    w_lt = jnp.transpose(w_dw[j].reshape(width, n_lt, LANES), (1, 0, 2))
    b_lt = b_dw[j].reshape(n_lt, 1, LANES)
    conv_tile = lambda b, i: (b * nt + jnp.minimum(i, nt - 1), 0)
    proj_tile = lambda b, i: (b * nt + jnp.maximum(i - 1, 0), 0)
    return pl.pallas_call(
        functools.partial(_conv_seq_kernel, width=width),
        grid=(batch, nt + 1),
        in_specs=[
            pl.BlockSpec((tt, ch), conv_tile),
            pl.BlockSpec((n_lt, width, LANES), lambda b, i: (0, 0, 0)),
            pl.BlockSpec((n_lt, 1, LANES), lambda b, i: (0, 0, 0)),
            vspec, vspec,
            pl.BlockSpec((tt, d), proj_tile),
            pl.BlockSpec((None, None, 1, d), lambda b, i: (layer, b, 0, 3 * 1 + 2)),
            _once((None, ch, d), lambda b, i: (j, 0, 0)),
            pl.BlockSpec((None, 1, d), lambda b, i: (j, 0, 0)),
        ],
        out_specs=[
            pl.BlockSpec((tt, d), proj_tile),
            pl.BlockSpec((None, width - 1, ch), lambda b, i: (b, 0, 0)),
        ],
        out_shape=[jax.ShapeDtypeStruct((m, d), F32), jax.ShapeDtypeStruct((batch, width - 1, ch), F32)],
        scratch_shapes=[
            pltpu.VMEM((n_lt, CONV_PAD + tt, LANES), F32),
            pltpu.VMEM((n_lt, tt, LANES), F32),
            pltpu.VMEM((2, tt, ch), BF16),
            pltpu.VMEM((n_slab, ch, CONV_SLAB), BF16),
        ],
        compiler_params=_cparams("arbitrary", "arbitrary"),
        name="conv_seq",
    )(u, w_lt, b_lt, vec(ln_g), vec(ln_b), x, mod, w2, vec(b2))


def _conv_step_kernel(u_ref, buf_ref, w_ref, bdw_ref, lng_ref, lnb_ref, y_ref, new_ref, acc_ref, *, width):
    u = u_ref[...]
    acc = u * w_ref[width - 1:width, :] + bdw_ref[...]
    for tap in range(width - 1):
        acc = acc + buf_ref[tap] * w_ref[tap:tap + 1, :]
    acc_ref[...] = acc
    for tap in range(width - 2):
        new_ref[tap] = buf_ref[tap + 1]
    new_ref[width - 2] = u
    _ln_silu_into(y_ref, acc_ref, lng_ref, lnb_ref)


def _conv_step_call(u, buf, w_dw, b_dw, ln_g, ln_b, *, j, tb):
    nb, ch = u.shape
    width = w_dw.shape[1]
    vec = lambda a: a.reshape(a.shape[0], 1, ch)
    vspec = pl.BlockSpec((None, 1, ch), lambda i: (j, 0, 0))
    return pl.pallas_call(
        functools.partial(_conv_step_kernel, width=width),
        grid=(nb // tb,),
        in_specs=[
            pl.BlockSpec((tb, ch), lambda i: (i, 0)),
            pl.BlockSpec((None, width - 1, tb, ch), lambda i: (j, 0, i, 0)),
            pl.BlockSpec((None, width, ch), lambda i: (j, 0, 0)),
            vspec, vspec, vspec,
        ],
        out_specs=[
            pl.BlockSpec((tb, ch), lambda i: (i, 0)),
            pl.BlockSpec((width - 1, tb, ch), lambda i: (0, i, 0)),
        ],
        out_shape=[jax.ShapeDtypeStruct((nb, ch), BF16), jax.ShapeDtypeStruct((width - 1, nb, ch), F32)],
        scratch_shapes=[pltpu.VMEM((tb, ch), F32)],
        compiler_params=_cparams("arbitrary"),
        name="conv_step",
    )(u, buf, w_dw, vec(b_dw), vec(ln_g), vec(ln_b))


def kernel(x_prompt, x_sample, state_hgrn, state_conv, c_prompt, c_sample, hgrn_lb, hgrn_w_in, hgrn_b_f, hgrn_g_norm, hgrn_w_out, conv_w_pw1, conv_b_pw1, conv_w_dw, conv_b_dw, conv_ln_g, conv_ln_b, conv_w_pw2, conv_b_pw2, ada_w, ada_b, norm_g, ffn_w_in, ffn_w_out, final_g):
    bp, t, d = x_prompt.shape
    bs = x_sample.shape[0]
    depth = ada_w.shape[0]
    heads = d // LANES
    time_major = lambda s: jnp.transpose(s, (0, 2, 1, 3))

    mod_s, mod_p = _ada_call(c_sample, c_prompt, ada_w, ada_b)
    mod_s = mod_s.reshape(depth, 1, bs, N_SUB * 3 * d)
    mod_p = mod_p.reshape(depth, bp, 1, N_SUB * 3 * d)
    norm_g4 = norm_g.reshape(depth, N_SUB, 1, d)
    conv_buf = time_major(state_conv)

    x, xs = x_prompt.reshape(bp * t, d), x_sample.reshape(bs, d)
    hgrn_p, hgrn_s, conv_p, conv_s = [], [], [], []
    for layer in range(depth):
        x, xs = _ffn_call(x, xs, mod_p, mod_s, norm_g4, ffn_w_in, ffn_w_out, final_g,
                          layer=layer, sub=0, j=0, final=False)
        j = layer // 2
        if layer % 2 == 0:
            outs = _hgrn_in_call(x, xs, mod_p, mod_s, norm_g4, hgrn_w_in, hgrn_b_f, hgrn_lb,
                                 layer=layer, j=j, lb_row=layer)
            y, s = _hgrn_scan_call(*outs[:5], hgrn_g_norm[j], batch=bp, heads=heads)
            ys, ss = _hgrn_step_call(*outs[5:], hgrn_g_norm[j], state_hgrn[j], heads=heads)
            hgrn_p.append(s)
            hgrn_s.append(ss)
            x, xs = _proj_call(y, ys, hgrn_w_out, None, x, xs, mod_p, mod_s, layer=layer, j=j)
        else:
            u, us = _pw1_call(x, xs, mod_p, mod_s, norm_g4, conv_w_pw1, conv_b_pw1, layer=layer, j=j)
            x, s = _conv_seq_call(u, conv_w_dw, conv_b_dw, conv_ln_g, conv_ln_b, x, mod_p, conv_w_pw2, conv_b_pw2,
                                  layer=layer, j=j, batch=bp)
            ys, ss = _conv_step_call(us, conv_buf, conv_w_dw, conv_b_dw, conv_ln_g, conv_ln_b, j=j, tb=4 * SUBLANES)
            conv_p.append(jnp.transpose(s, (1, 0, 2)))
            conv_s.append(ss)
            xs = _proj_rows_call(ys, conv_w_pw2, conv_b_pw2, xs, mod_s, layer=layer, j=j)
        x, xs = _ffn_call(x, xs, mod_p, mod_s, norm_g4, ffn_w_in, ffn_w_out, final_g,
                          layer=layer, sub=2, j=1, final=(layer == depth - 1))
    return (x.reshape(bp, t, d), xs.reshape(bs, 1, d), jnp.stack(hgrn_p), jnp.stack(hgrn_s),
            time_major(jnp.stack(conv_p)), time_major(jnp.stack(conv_s)))
```

```python
import functools

import jax
import jax.numpy as jnp
from jax import lax
from jax.experimental import pallas as pl
from jax.experimental.pallas import tpu as pltpu

F32 = jnp.float32
BF16 = jnp.bfloat16
EPS = 1e-6
N_SUB = 3
LANES = 128
SUBLANES = 8
VMEM_LIMIT_BYTES = 60 * 1024 * 1024

ROW_TILE = 1024
FFN_TILE = 256
IN_TILE = 256
PW1_TILE = 512
OUT_TILE = 2048
OUT_ROW_TILE = 512
SCAN_ROWS = 256
SCAN_SUB = 64
SCAN_SPAN_MAX = 120.0
SCAN_SLOW_ROWS = 16
CONV_PAD = 32
CONV_TILE = 256
CONV_SLAB = 256
NORM_ROWS = 16
NORM_UNROLL = 8


def _cparams(*sem):
    return pltpu.CompilerParams(dimension_semantics=sem, vmem_limit_bytes=VMEM_LIMIT_BYTES)


def _dot(a, b):
    return jnp.dot(a, b, preferred_element_type=F32)


def _dot_nt(a, b):
    return lax.dot_general(a, b, (((1,), (1,)), ((), ())), preferred_element_type=F32)


def _dot_tn(a, b):
    return lax.dot_general(a, b, (((0,), (0,)), ((), ())), preferred_element_type=F32)


def _sigmoid(x):
    return 1.0 / (1.0 + jnp.exp(-x))


def _silu(x):
    return x * _sigmoid(x)


def _rmsnorm(x, g):
    return x * lax.rsqrt(jnp.mean(x * x, axis=-1, keepdims=True) + EPS) * g


def _row_loop(n_rows, fn, in_place=False):
    rows = min(NORM_ROWS * NORM_UNROLL if in_place else NORM_ROWS, n_rows)
    steps = n_rows // rows
    if steps == 1:
        fn(pl.ds(0, rows))
        return

    def body(i, carry):
        fn(pl.ds(pl.multiple_of(i * rows, rows), rows))
        return carry

    lax.fori_loop(0, steps, body, 0, unroll=NORM_UNROLL if not in_place and steps % NORM_UNROLL == 0 else 1)


def _mod_rows(ref, rows):
    return ref[...] if ref.shape[0] == 1 else ref[rows, :]


def _modulate_into(h_ref, x_ref, g_ref, sh_ref, sc_ref):
    per_tile = sc_ref.shape[0] == 1
    gain_tile = g_ref[...] * (1.0 + sc_ref[...]) if per_tile else None

    def fn(rows):
        x = x_ref[rows, :]
        gain = gain_tile if per_tile else g_ref[...] * (1.0 + sc_ref[rows, :])
        y = x * lax.rsqrt(jnp.mean(x * x, axis=-1, keepdims=True) + EPS) * gain + _mod_rows(sh_ref, rows)
        h_ref[rows, :] = y.astype(BF16)

    _row_loop(x_ref.shape[0], fn)


def _once(shape, index_map):
    return pl.BlockSpec(shape, index_map, pipeline_mode=pl.Buffered(1))


def _mod_specs(mod, layer, secs, width, col_of, row_tiles):
    groups, r = mod.shape[1], mod.shape[2]
    tiles_per_group = max(row_tiles // groups, 1)
    ncol = mod.shape[3] // (N_SUB * 3) // width

    def spec(sec):
        def index_map(*idx):
            i, n = col_of(*idx)
            return (layer, i // tiles_per_group, 0, sec * ncol + n)
        if groups == 1 and ncol == 1:
            return _once((None, None, r, width), index_map)
        return pl.BlockSpec((None, None, r, width), index_map)

    return [spec(s) for s in secs]


def _ada_kernel(cs_ref, cp_ref, w_ref, b_ref, os_ref, op_ref):
    w = w_ref[...].astype(BF16)
    os_ref[...] = _dot(_silu(cs_ref[...]).astype(BF16), w) + b_ref[...]
    op_ref[...] = _dot(_silu(cp_ref[...]).astype(BF16), w) + b_ref[...]


def _ada_call(c_sample, c_prompt, ada_w, ada_b, tn=1024):
    depth, d, n = ada_w.shape
    rs, rp = c_sample.shape[0], c_prompt.shape[0]
    return pl.pallas_call(
        _ada_kernel,
        grid=(depth, n // tn),
        in_specs=[
            pl.BlockSpec((rs, d), lambda l, j: (0, 0)),
            pl.BlockSpec((rp, d), lambda l, j: (0, 0)),
            pl.BlockSpec((None, d, tn), lambda l, j: (l, 0, j)),
            pl.BlockSpec((None, 1, tn), lambda l, j: (l, 0, j)),
        ],
        out_specs=[
            pl.BlockSpec((None, rs, tn), lambda l, j: (l, 0, j)),
            pl.BlockSpec((None, rp, tn), lambda l, j: (l, 0, j)),
        ],
        out_shape=[jax.ShapeDtypeStruct((depth, rs, n), F32), jax.ShapeDtypeStruct((depth, rp, n), F32)],
        compiler_params=_cparams("arbitrary", "arbitrary"),
        name="adaln_params",
    )(c_sample, c_prompt, ada_w, ada_b.reshape(depth, 1, n))


def _ffn_kernel(x_ref, sh_ref, sc_ref, gt_ref, xs_ref, shs_ref, scs_ref, gts_ref, g_ref, wa_ref, wu_ref, wo_ref,
                *rest, final):
    fg_ref = rest[0] if final else None
    o_ref, os_ref, h_ref = rest[-3:]
    tm, ms = x_ref.shape[0], xs_ref.shape[0]
    first_tile = pl.program_id(0) == 0
    f = pl.program_id(1)
    last = pl.num_programs(1) - 1

    def start(x_ref, sh_ref, sc_ref, o_ref, row0):
        _modulate_into(h_ref.at[pl.ds(row0, x_ref.shape[0])], x_ref, g_ref, sh_ref, sc_ref)
        o_ref[...] = jnp.zeros_like(o_ref)

    def finish(x_ref, gt_ref, o_ref):
        def fn(rows):
            y = x_ref[rows, :] + 0.5 * _mod_rows(gt_ref, rows) * o_ref[rows, :]
            if fg_ref is not None:
                y = _rmsnorm(y, fg_ref[...])
            o_ref[rows, :] = y

        _row_loop(x_ref.shape[0], fn, in_place=True)

    def hidden(rows):
        h = h_ref[0:rows, :]
        return (_silu(_dot(h, wa_ref[...].astype(BF16))) * _dot(h, wu_ref[...].astype(BF16))).astype(BF16)

    @pl.when(f == 0)
    def _():
        start(x_ref, sh_ref, sc_ref, o_ref, 0)

    @pl.when((f == 0) & first_tile)
    def _():
        start(xs_ref, shs_ref, scs_ref, os_ref, tm)

    @pl.when(jnp.logical_not(first_tile))
    def _():
        o_ref[...] += _dot(hidden(tm), wo_ref[...].astype(BF16))

    @pl.when(first_tile)
    def _():
        act = hidden(tm + ms)
        wo = wo_ref[...].astype(BF16)
        o_ref[...] += _dot(act[0:tm], wo)
        os_ref[...] += _dot(act[tm:tm + ms], wo)

    @pl.when(f == last)
    def _():
        finish(x_ref, gt_ref, o_ref)

    @pl.when((f == last) & first_tile)
    def _():
        finish(xs_ref, gts_ref, os_ref)


def _ffn_call(x, xs, mod, mod_s, norm_g, w_in, w_out, final_g, *, layer, sub, j, final):
    m, d = x.shape
    ms = xs.shape[0]
    tm, tf = ROW_TILE, FFN_TILE
    nf = w_out.shape[2] // tf
    secs = [3 * sub, 3 * sub + 1, 3 * sub + 2]
    in_specs = (
        [pl.BlockSpec((tm, d), lambda i, f: (i, 0))]
        + _mod_specs(mod, layer, secs, d, lambda i, f: (i, 0), m // tm)
        + [_once((ms, d), lambda i, f: (0, 0))]
        + _mod_specs(mod_s, layer, secs, d, lambda i, f: (0, 0), 1)
        + [
            pl.BlockSpec((None, None, 1, d), lambda i, f: (layer, sub, 0, 0)),
            pl.BlockSpec((None, None, d, tf), lambda i, f: (layer, j, 0, f)),
            pl.BlockSpec((None, None, d, tf), lambda i, f: (layer, j, 0, nf + f)),
            pl.BlockSpec((None, None, tf, d), lambda i, f: (layer, j, f, 0)),
        ]
    )
    args = [x, mod, mod, mod, xs, mod_s, mod_s, mod_s, norm_g, w_in, w_in, w_out]
    if final:
        in_specs.append(pl.BlockSpec((1, d), lambda i, f: (0, 0)))
        args.append(final_g.reshape(1, d))
    return pl.pallas_call(
        functools.partial(_ffn_kernel, final=final),
        grid=(m // tm, nf),
        in_specs=in_specs,
        out_specs=[pl.BlockSpec((tm, d), lambda i, f: (i, 0)), _once((ms, d), lambda i, f: (0, 0))],
        out_shape=[jax.ShapeDtypeStruct((m, d), F32), jax.ShapeDtypeStruct((ms, d), F32)],
        scratch_shapes=[pltpu.VMEM((tm + ms, d), BF16)],
        compiler_params=_cparams("arbitrary", "arbitrary"),
        name="ffn",
    )(*args)


def _hgrn_in_rows(x_ref, sh_ref, sc_ref, g_ref, bf_ref, h_ref, outs, ws, lb):
    q_ref, k_ref, v_ref, lf_ref, sg_ref = outs
    wq_ref, wf_ref, wi_ref, wg_ref = ws

    @pl.when(pl.program_id(1) == 0)
    def _():
        _modulate_into(h_ref, x_ref, g_ref, sh_ref, sc_ref)

    h = h_ref[...]
    q_ref[...] = _dot(h, wq_ref[...].astype(BF16))
    z = _dot(h, wf_ref[...].astype(BF16)) + bf_ref[...]
    lf_ref[...] = jnp.log(lb + (1.0 - lb) * _sigmoid(z))
    k_ref[...] = (1.0 - lb) * _sigmoid(-z)
    v_ref[...] = _dot(h, wi_ref[...].astype(BF16)).astype(BF16)
    sg_ref[...] = _silu(_dot(h, wg_ref[...].astype(BF16)))


def _hgrn_in_kernel(x_ref, sh_ref, sc_ref, xs_ref, shs_ref, scs_ref, g_ref, wq_ref, wf_ref, wi_ref, wg_ref,
                    bf_ref, lb_ref, *rest, lb_row):
    outs, outs_s, (h_ref, hs_ref) = rest[0:5], rest[5:10], rest[10:12]
    raw = [lb_ref[i:i + 1, :] for i in range(lb_ref.shape[0])]
    top = functools.reduce(jnp.maximum, raw)
    e = [jnp.exp(r - top) for r in raw]
    lb = sum(e[:lb_row + 1]) / sum(e)
    ws = (wq_ref, wf_ref, wi_ref, wg_ref)
    _hgrn_in_rows(x_ref, sh_ref, sc_ref, g_ref, bf_ref, h_ref, outs, ws, lb)

    @pl.when(pl.program_id(0) == 0)
    def _():
        _hgrn_in_rows(xs_ref, shs_ref, scs_ref, g_ref, bf_ref, hs_ref, outs_s, ws, lb)


def _side_col(nn):
    return lambda i, n: (0, jnp.where(i == 0, n, nn - 1))


def _hgrn_in_call(x, xs, mod, mod_s, norm_g, w_in, b_f, lb_raw, *, layer, j, lb_row):
    m, d = x.shape
    ms = xs.shape[0]
    tm, tn = ROW_TILE, IN_TILE
    nn = d // tn
    sub = 1
    secs = [3 * sub, 3 * sub + 1]

    def wspec(sec):
        return pl.BlockSpec((None, d, tn), lambda i, n: (j, 0, sec * nn + n))

    def out_shapes(rows):
        return [jax.ShapeDtypeStruct((rows, d), dt) for dt in (F32, F32, BF16, F32, F32)]

    return pl.pallas_call(
        functools.partial(_hgrn_in_kernel, lb_row=lb_row),
        grid=(m // tm, nn),
        in_specs=(
            [pl.BlockSpec((tm, d), lambda i, n: (i, 0))]
            + _mod_specs(mod, layer, secs, d, lambda i, n: (i, 0), m // tm)
            + [_once((ms, d), lambda i, n: (0, 0))]
            + _mod_specs(mod_s, layer, secs, d, lambda i, n: (0, 0), 1)
            + [
                pl.BlockSpec((None, None, 1, d), lambda i, n: (layer, sub, 0, 0)),
                wspec(0), wspec(1), wspec(2), wspec(3),
                pl.BlockSpec((None, 1, tn), lambda i, n: (j, 0, n)),
                pl.BlockSpec((lb_raw.shape[0], tn), lambda i, n: (0, n)),
            ]
        ),
        out_specs=[pl.BlockSpec((tm, tn), lambda i, n: (i, n))] * 5 + [pl.BlockSpec((ms, tn), _side_col(nn))] * 5,
        out_shape=out_shapes(m) + out_shapes(ms),
        scratch_shapes=[pltpu.VMEM((tm, d), BF16), pltpu.VMEM((ms, d), BF16)],
        compiler_params=_cparams("arbitrary", "arbitrary"),
        name="hgrn_in",
    )(x, mod, mod, xs, mod_s, mod_s, norm_g, w_in, w_in, w_in, w_in, b_f.reshape(b_f.shape[0], 1, d), lb_raw)


def _hgrn_scan_kernel(q_ref, k_ref, v_ref, lf_ref, sg_ref, gn_ref, y_ref, sfin_ref, st_ref, b_ref, o_ref, *,
                      heads, span_max):
    c = pl.program_id(1)
    rows, d = q_ref.shape
    dk = d // heads
    sub = min(SCAN_SUB, rows)
    n_sub = rows // sub
    n_levels = n_sub.bit_length() - 1
    sub_shift = sub.bit_length() - 1

    @pl.when(c == 0)
    def _():
        st_ref[...] = jnp.zeros_like(st_ref)

    same_head = ((lax.broadcasted_iota(jnp.int32, (2 * dk, 2 * dk), 0) < dk)
                 == (lax.broadcasted_iota(jnp.int32, (2 * dk, 2 * dk), 1) < dk))
    row = lax.broadcasted_iota(jnp.int32, (rows, rows), 0)
    col = lax.broadcasted_iota(jnp.int32, (rows, rows), 1)
    tri = jnp.where(col <= row, 1.0, 0.0).astype(BF16)
    lf = lf_ref[...]
    p1 = lf.astype(BF16)
    r1 = lf - p1.astype(F32)
    p2 = r1.astype(BF16)
    p3 = (r1 - p2.astype(F32)).astype(BF16)
    b_ref[...] = _dot(tri, p1) + _dot(tri, p2) + _dot(tri, p3)

    def first_row(j):
        return slice(j * sub, j * sub + 1)

    def last_row(j):
        return slice((j + 1) * sub - 1, (j + 1) * sub)

    spans = [b_ref[first_row(j), :] - b_ref[last_row(j), :] for j in range(n_sub)]
    safe = jnp.max(functools.reduce(jnp.maximum, spans)) <= span_max

    def gated_out(o, sl):
        return (_rmsnorm(o, gn_ref[:, sl]) * sg_ref[:, sl]).astype(BF16)

    @pl.when(safe)
    def _():
        srl = lax.shift_right_logical
        level = jnp.where((srl(row, sub_shift) == srl(col, sub_shift)) & (col <= row), 0, -1)
        later_half = []
        row_d = lax.broadcasted_iota(jnp.int32, (rows, dk), 0)
        for l in range(1, n_levels + 1):
            s = sub_shift + l - 1
            hit = (srl(row, s + 1) == srl(col, s + 1)) & ((srl(row, s) & 1) == 1) & ((srl(col, s) & 1) == 0)
            level = jnp.where(hit, l, level)
            later_half.append((srl(row_d, s) & 1) == 1)

        def rows_of(pieces, n):
            return jnp.concatenate([jnp.broadcast_to(p, (n, dk)) for p in pieces], axis=0)

        def scores(sl):
            b = b_ref[:, sl]
            q = q_ref[:, sl]
            k = k_ref[:, sl]
            rel = b - rows_of([0.5 * (b_ref[first_row(j), sl] + b_ref[last_row(j), sl]) for j in range(n_sub)], sub)
            att = jnp.where(level == 0,
                            _dot_nt((q * jnp.exp(rel)).astype(BF16), (k * jnp.exp(-rel)).astype(BF16)), 0.0)
            for l in range(1, n_levels + 1):
                m = sub << (l - 1)
                rel = b - rows_of([b_ref[p * 2 * m + m:p * 2 * m + m + 1, sl] for p in range(rows // (2 * m))], 2 * m)
                e = jnp.exp(jnp.minimum(jnp.where(later_half[l - 1], rel, -rel), 0.0))
                att = jnp.where(level == l, _dot_nt((q * e).astype(BF16), (k * e).astype(BF16)), att)
            return att.astype(BF16)

        first_head = lax.broadcasted_iota(jnp.int32, (rows, 2 * dk), 1) < dk
        for pair in range(heads // 2):
            sl0 = slice(2 * pair * dk, (2 * pair + 1) * dk)
            sl1 = slice((2 * pair + 1) * dk, (2 * pair + 2) * dk)
            both = slice(2 * pair * dk, (2 * pair + 2) * dk)
            att = jnp.concatenate([scores(sl0), scores(sl1)], axis=1)
            v = v_ref[:, both]
            zero = jnp.zeros_like(v)
            v_split = jnp.concatenate([jnp.where(first_head, v, zero), jnp.where(first_head, zero, v)], axis=0)
            b = b_ref[:, both]
            b_last = b_ref[rows - 1:rows, both]
            q_st = (q_ref[:, both] * jnp.exp(b)).astype(BF16)
            k_st = (k_ref[:, both] * jnp.exp(b_last - b)).astype(BF16)
            st = st_ref[pair]
            o = _dot(att, v_split) + _dot_nt(q_st, st.astype(BF16))
            st_ref[pair] = st * jnp.exp(b_last) + jnp.where(same_head, _dot_tn(v, k_st), 0.0)
            y_ref[:, sl0] = gated_out(o[:, 0:dk], sl0)
            y_ref[:, sl1] = gated_out(o[:, dk:2 * dk], sl1)

    @pl.when(jnp.logical_not(safe))
    def _():
        grp_rows = min(SCAN_SLOW_ROWS, rows)
        grp_shift = grp_rows.bit_length() - 1
        pick = lax.broadcasted_iota(jnp.int32, (grp_rows, 2 * dk), 0)
        o_ref[...] = jnp.zeros_like(o_ref)

        def token(t, carry):
            r0 = pl.multiple_of(lax.shift_left(lax.shift_right_logical(t, grp_shift), grp_shift), grp_rows)
            sel = pick == t - r0
            grp = pl.ds(r0, grp_rows)
            for pair in range(heads // 2):
                sl = slice(2 * pair * dk, (2 * pair + 2) * dk)
                f = jnp.sum(jnp.where(sel, jnp.exp(lf_ref[grp, sl]), 0.0), axis=0, keepdims=True)
                q1 = jnp.where(sel, q_ref[grp, sl], 0.0).astype(BF16)
                k1 = jnp.where(sel, k_ref[grp, sl], 0.0).astype(BF16)
                v1 = jnp.where(sel, v_ref[grp, sl].astype(F32), 0.0).astype(BF16)
                st = st_ref[pair] * f + jnp.where(same_head, _dot_tn(v1, k1), 0.0)
                st_ref[pair] = st
                o_ref[grp, sl] = jnp.where(sel, _dot_nt(q1, st.astype(BF16)), o_ref[grp, sl])
            return carry

        lax.fori_loop(0, rows, token, 0)
        for h in range(heads):
            sl = slice(h * dk, (h + 1) * dk)
            y_ref[:, sl] = gated_out(o_ref[:, sl], sl)

    @pl.when(c == pl.num_programs(1) - 1)
    def _():
        for h in range(heads):
            lo = (h % 2) * dk
            sfin_ref[h] = st_ref[h // 2, lo:lo + dk, lo:lo + dk].T


def _hgrn_scan_call(q, k, v, lf, sg, g_norm, *, batch, heads, span_max=SCAN_SPAN_MAX):
    m, d = q.shape
    t = m // batch
    rows = min(SCAN_ROWS, t)
    nc = t // rows
    dk = d // heads
    spec = pl.BlockSpec((rows, d), lambda b, c: (b * nc + c, 0))
    return pl.pallas_call(
        functools.partial(_hgrn_scan_kernel, heads=heads, span_max=span_max),
        grid=(batch, nc),
        in_specs=[spec, spec, spec, spec, spec, pl.BlockSpec((1, d), lambda b, c: (0, 0))],
        out_specs=[spec, pl.BlockSpec((None, heads, dk, dk), lambda b, c: (b, 0, 0, 0))],
        out_shape=[jax.ShapeDtypeStruct((m, d), BF16), jax.ShapeDtypeStruct((batch, heads, dk, dk), F32)],
        scratch_shapes=[pltpu.VMEM((heads // 2, 2 * dk, 2 * dk), F32), pltpu.VMEM((rows, d), F32),
                        pltpu.VMEM((rows, d), F32)],
        compiler_params=_cparams("arbitrary", "arbitrary"),
        name="hgrn_scan",
    )(q, k, v, lf, sg, g_norm.reshape(1, d))


def _hgrn_step_kernel(q_ref, k_ref, v_ref, lf_ref, sg_ref, gn_ref, s0_ref, y_ref, s1_ref, o_ref):
    nb = q_ref.shape[0]
    k_t = k_ref[...].T
    f_t = jnp.exp(lf_ref[...]).T
    v = v_ref[...].astype(F32)
    q = q_ref[...].astype(BF16)
    for b in range(nb):
        s1 = s0_ref[b] * f_t[:, b:b + 1] + k_t[:, b:b + 1] * v[b:b + 1, :]
        s1_ref[b] = s1
        o_ref[b:b + 1, :] = _dot(q[b:b + 1, :], s1.astype(BF16))
    y = _rmsnorm(o_ref[...], gn_ref[...]) * sg_ref[...]
    y_ref[...] = y.astype(BF16)


def _hgrn_step_call(q, k, v, lf, sg, g_norm, s0, *, heads):
    nb, d = q.shape
    dk = d // heads
    spec = pl.BlockSpec((nb, dk), lambda h: (0, h))
    sspec = pl.BlockSpec((nb, None, dk, dk), lambda h: (0, h, 0, 0))
    return pl.pallas_call(
        _hgrn_step_kernel,
        grid=(heads,),
        in_specs=[spec, spec, spec, spec, spec, pl.BlockSpec((1, dk), lambda h: (0, h)), sspec],
        out_specs=[spec, sspec],
        out_shape=[jax.ShapeDtypeStruct((nb, d), BF16), jax.ShapeDtypeStruct(s0.shape, F32)],
        scratch_shapes=[pltpu.VMEM((nb, dk), F32)],
        compiler_params=_cparams("arbitrary"),
        name="hgrn_step",
    )(q, k, v, lf, sg, g_norm.reshape(1, d), s0)


def _proj_kernel(y_ref, x_ref, gt_ref, ys_ref, xs_ref, gts_ref, w_ref, *rest, has_bias):
    b_ref = rest[0] if has_bias else None
    o_ref, os_ref = rest[-2:]
    def rows(y_ref, x_ref, gt_ref, o_ref):
        out = _dot(y_ref[...], w_ref[...].astype(BF16))
        if has_bias:
            out = out + b_ref[...]
        o_ref[...] = x_ref[...] + gt_ref[...] * out

    rows(y_ref, x_ref, gt_ref, o_ref)

    @pl.when(pl.program_id(1) == 0)
    def _():
        rows(ys_ref, xs_ref, gts_ref, os_ref)


def _proj_call(y, ys, w, bias, x, xs, mod, mod_s, *, layer, j):
    m, kdim = y.shape
    ms = ys.shape[0]
    d = x.shape[1]
    tm, tn = OUT_ROW_TILE, min(OUT_TILE, d)
    gate = [3 * 1 + 2]
    has_bias = bias is not None
    wspec = _once if tn == d else pl.BlockSpec
    in_specs = (
        [pl.BlockSpec((tm, kdim), lambda n, i: (i, 0)), pl.BlockSpec((tm, tn), lambda n, i: (i, n))]
        + _mod_specs(mod, layer, gate, tn, lambda n, i: (i, n), m // tm)
        + [_once((ms, kdim), lambda n, i: (0, 0)), pl.BlockSpec((ms, tn), lambda n, i: (0, n))]
        + _mod_specs(mod_s, layer, gate, tn, lambda n, i: (0, n), 1)
        + [wspec((None, kdim, tn), lambda n, i: (j, 0, n))]
    )
    args = [y, x, mod, ys, xs, mod_s, w]
    if has_bias:
        in_specs.append(pl.BlockSpec((None, 1, tn), lambda n, i: (j, 0, n)))
        args.append(bias.reshape(bias.shape[0], 1, d))
    return pl.pallas_call(
        functools.partial(_proj_kernel, has_bias=has_bias),
        grid=(d // tn, m // tm),
        in_specs=in_specs,
        out_specs=[pl.BlockSpec((tm, tn), lambda n, i: (i, n)), pl.BlockSpec((ms, tn), lambda n, i: (0, n))],
        out_shape=[jax.ShapeDtypeStruct((m, d), F32), jax.ShapeDtypeStruct((ms, d), F32)],
        compiler_params=_cparams("arbitrary", "arbitrary"),
        name="proj_residual",
    )(*args)


def _proj_rows_kernel(y_ref, x_ref, gt_ref, w_ref, b_ref, o_ref):
    o_ref[...] = x_ref[...] + gt_ref[...] * (_dot(y_ref[...], w_ref[...].astype(BF16)) + b_ref[...])


def _proj_rows_call(y, w, bias, x, mod, *, layer, j):
    ms, kdim = y.shape
    d = x.shape[1]
    tn = IN_TILE
    return pl.pallas_call(
        _proj_rows_kernel,
        grid=(d // tn,),
        in_specs=(
            [_once((ms, kdim), lambda n: (0, 0)), pl.BlockSpec((ms, tn), lambda n: (0, n))]
            + _mod_specs(mod, layer, [3 * 1 + 2], tn, lambda n: (0, n), 1)
            + [pl.BlockSpec((None, kdim, tn), lambda n: (j, 0, n)), pl.BlockSpec((None, 1, tn), lambda n: (j, 0, n))]
        ),
        out_specs=pl.BlockSpec((ms, tn), lambda n: (0, n)),
        out_shape=jax.ShapeDtypeStruct((ms, d), F32),
        compiler_params=_cparams("arbitrary"),
        name="proj_rows",
    )(y, x, mod, w, bias.reshape(bias.shape[0], 1, d))


def _pw1_kernel(x_ref, sh_ref, sc_ref, xs_ref, shs_ref, scs_ref, g_ref, wa_ref, wg_ref, ba_ref, bg_ref,
                u_ref, us_ref, h_ref, hs_ref):
    def rows(x_ref, sh_ref, sc_ref, u_ref, h_ref):
        @pl.when(pl.program_id(1) == 0)
        def _():
            _modulate_into(h_ref, x_ref, g_ref, sh_ref, sc_ref)

        h = h_ref[...]
        a = _dot(h, wa_ref[...].astype(BF16)) + ba_ref[...]
        u_ref[...] = a * _sigmoid(_dot(h, wg_ref[...].astype(BF16)) + bg_ref[...])

    rows(x_ref, sh_ref, sc_ref, u_ref, h_ref)

    @pl.when(pl.program_id(0) == 0)
    def _():
        rows(xs_ref, shs_ref, scs_ref, us_ref, hs_ref)


def _pw1_call(x, xs, mod, mod_s, norm_g, w, bias, *, layer, j):
    m, d = x.shape
    ms = xs.shape[0]
    tm, tn = ROW_TILE, PW1_TILE
    ch = w.shape[2] // 2
    nn = ch // tn
    sub = 1
    secs = [3 * sub, 3 * sub + 1]

    def wspec(sec):
        return pl.BlockSpec((None, d, tn), lambda i, n: (j, 0, sec * nn + n))

    def bspec(sec):
        return pl.BlockSpec((None, 1, tn), lambda i, n: (j, 0, sec * nn + n))

    bias3 = bias.reshape(bias.shape[0], 1, 2 * ch)
    return pl.pallas_call(
        _pw1_kernel,
        grid=(m // tm, nn),
        in_specs=(
            [pl.BlockSpec((tm, d), lambda i, n: (i, 0))]
            + _mod_specs(mod, layer, secs, d, lambda i, n: (i, 0), m // tm)
            + [_once((ms, d), lambda i, n: (0, 0))]
            + _mod_specs(mod_s, layer, secs, d, lambda i, n: (0, 0), 1)
            + [
                pl.BlockSpec((None, None, 1, d), lambda i, n: (layer, sub, 0, 0)),
                wspec(0), wspec(1), bspec(0), bspec(1),
            ]
        ),
        out_specs=[pl.BlockSpec((tm, tn), lambda i, n: (i, n)), pl.BlockSpec((ms, tn), _side_col(nn))],
        out_shape=[jax.ShapeDtypeStruct((m, ch), F32), jax.ShapeDtypeStruct((ms, ch), F32)],
        scratch_shapes=[pltpu.VMEM((tm, d), BF16), pltpu.VMEM((ms, d), BF16)],
        compiler_params=_cparams("arbitrary", "arbitrary"),
        name="conv_pw1",
    )(x, mod, mod, xs, mod_s, mod_s, norm_g, w, w, bias3, bias3)


def _ln_silu_into(y_ref, acc_ref, lng_ref, lnb_ref):
    def fn(rows):
        a = acc_ref[rows, :]
        mu = jnp.mean(a, axis=-1, keepdims=True)
        xc = a - mu
        yn = xc * lax.rsqrt(jnp.mean(xc * xc, axis=-1, keepdims=True) + EPS) * lng_ref[...] + lnb_ref[...]
        y_ref[rows, :] = _silu(yn).astype(BF16)

    _row_loop(acc_ref.shape[0], fn)


def _conv_seq_kernel(u_ref, w_ref, bdw_ref, lng_ref, lnb_ref, x_ref, gt_ref, w2_ref, b2_ref, o_ref, tail_ref,
                     win_ref, acc_ref, y_ref, w2b_ref, *, width, tiles_per_seq):
    t = pl.program_id(1)
    nt = pl.num_programs(1) - 1
    tt, ch = u_ref.shape
    n_lt = ch // LANES
    off = CONV_PAD - (width - 1)
    blk = SUBLANES * SUBLANES
    n_slab, _, slab = w2b_ref.shape
    lt_per_slab = n_lt // n_slab

    @pl.when((pl.program_id(0) == 0) & (t == 0))
    def _():
        for s in range(n_slab):
            w2b_ref[s] = w2_ref[:, s * slab:(s + 1) * slab].astype(BF16)

    @pl.when(lax.rem(t, tiles_per_seq) == 0)
    def _():
        win_ref[:, 0:CONV_PAD, :] = jnp.zeros((n_lt, CONV_PAD, LANES), F32)

    @pl.when(t < nt)
    def _():
        for l in range(n_lt):
            win_ref[l, CONV_PAD:CONV_PAD + tt, :] = u_ref[:, l * LANES:(l + 1) * LANES]

    def project(s):
        cols = slice(s * slab, (s + 1) * slab)
        out = _dot(y_ref[(t + 1) % 2], w2b_ref[s]) + b2_ref[:, cols]
        o_ref[:, cols] = x_ref[:, cols] + gt_ref[:, cols] * out

    def conv_block(l, r0):
        strided = [win_ref[l, pl.ds(r0 + r, SUBLANES, stride=SUBLANES), :]
                   for r in range(off, off + width - 1 + SUBLANES)]
        accs = [[jnp.broadcast_to(bdw_ref[l], (SUBLANES, LANES)), None] for _ in range(SUBLANES)]
        for tap in range(width):
            w_tap = jnp.broadcast_to(w_ref[l, tap:tap + 1, :], (SUBLANES, LANES))
            for t0 in range(SUBLANES):
                term = strided[t0 + tap] * w_tap
                prev = accs[t0][tap % 2]
                accs[t0][tap % 2] = term if prev is None else prev + term
        for t0 in range(SUBLANES):
            acc_ref[l, pl.ds(r0 + t0, SUBLANES, stride=SUBLANES), :] = accs[t0][0] + accs[t0][1]

    def conv_lane_tiles(lts):
        for l in lts:
            for r0 in range(0, tt, blk):
                conv_block(l, r0)

    @pl.when(t == 0)
    def _():
        def body(i, carry):
            conv_block(i // (tt // blk), pl.multiple_of((i % (tt // blk)) * blk, blk))
            return carry

        lax.fori_loop(0, n_lt * (tt // blk), body, 0, unroll=2)

    @pl.when((t > 0) & (t < nt))
    def _():
        for s in range(n_slab):
            project(s)
            conv_lane_tiles(range(s * lt_per_slab, (s + 1) * lt_per_slab))

    @pl.when(t == nt)
    def _():
        for s in range(n_slab):
            project(s)

    @pl.when(t < nt)
    def _():
        for l in range(n_lt):
            tail_ref[:, l * LANES:(l + 1) * LANES] = win_ref[l, CONV_PAD + tt - (width - 1):CONV_PAD + tt, :]
        win_ref[:, 0:CONV_PAD, :] = win_ref[:, tt:tt + CONV_PAD, :]
        y_cur = y_ref.at[t % 2]

        def fn(rows):
            a = jnp.concatenate([acc_ref[l, rows, :] for l in range(n_lt)], axis=1)
            mu = jnp.mean(a, axis=-1, keepdims=True)
            xc = a - mu
            yn = xc * lax.rsqrt(jnp.mean(xc * xc, axis=-1, keepdims=True) + EPS) * lng_ref[...] + lnb_ref[...]
            y_cur[rows, :] = _silu(yn).astype(BF16)

        _row_loop(tt, fn)


def _conv_seq_call(u, w_dw, b_dw, ln_g, ln_b, x, mod, w2, b2, *, layer, j, batch):
    m, ch = u.shape
    d = x.shape[1]
    t = m // batch
    tt = min(CONV_TILE, t)
    nt = t // tt
    width = w_dw.shape[1]
    n_lt = ch // LANES
    n_slab = d // CONV_SLAB
    vec = lambda a: a.reshape(a.shape[0], 1, a.shape[-1])
    vspec = pl.BlockSpec((None, 1, ch), lambda b, i: (j, 0, 0))
    w_lt = jnp.transpose(w_dw[j].reshape(width, n_lt, LANES), (1, 0, 2))
    b_lt = b_dw[j].reshape(n_lt, 1, LANES)
    total = batch * nt
    conv_tile = lambda b, i: (jnp.minimum(i, total - 1), 0)
    proj_tile = lambda b, i: (jnp.maximum(i - 1, 0), 0)
    return pl.pallas_call(
        functools.partial(_conv_seq_kernel, width=width, tiles_per_seq=nt),
        grid=(1, total + 1),
        in_specs=[
            pl.BlockSpec((tt, ch), conv_tile),
            pl.BlockSpec((n_lt, width, LANES), lambda b, i: (0, 0, 0)),
            pl.BlockSpec((n_lt, 1, LANES), lambda b, i: (0, 0, 0)),
            vspec, vspec,
            pl.BlockSpec((tt, d), proj_tile),
            pl.BlockSpec((None, None, 1, d), lambda b, i: (layer, jnp.maximum(i - 1, 0) // nt, 0, 3 * 1 + 2)),
            _once((None, ch, d), lambda b, i: (j, 0, 0)),
            pl.BlockSpec((None, 1, d), lambda b, i: (j, 0, 0)),
        ],
        out_specs=[
            pl.BlockSpec((tt, d), proj_tile),
            pl.BlockSpec((None, width - 1, ch), lambda b, i: (jnp.minimum(i, total - 1) // nt, 0, 0)),
        ],
        out_shape=[jax.ShapeDtypeStruct((m, d), F32), jax.ShapeDtypeStruct((batch, width - 1, ch), F32)],
        scratch_shapes=[
            pltpu.VMEM((n_lt, CONV_PAD + tt, LANES), F32),
            pltpu.VMEM((n_lt, tt, LANES), F32),
            pltpu.VMEM((2, tt, ch), BF16),
            pltpu.VMEM((n_slab, ch, CONV_SLAB), BF16),
        ],
        compiler_params=_cparams("arbitrary", "arbitrary"),
        name="conv_seq",
    )(u, w_lt, b_lt, vec(ln_g), vec(ln_b), x, mod, w2, vec(b2))


def _conv_step_kernel(u_ref, buf_ref, w_ref, bdw_ref, lng_ref, lnb_ref, y_ref, new_ref, acc_ref, *, width):
    u = u_ref[...]
    acc = u * w_ref[width - 1:width, :] + bdw_ref[...]
    for tap in range(width - 1):
        acc = acc + buf_ref[tap] * w_ref[tap:tap + 1, :]
    acc_ref[...] = acc
    for tap in range(width - 2):
        new_ref[tap] = buf_ref[tap + 1]
    new_ref[width - 2] = u
    _ln_silu_into(y_ref, acc_ref, lng_ref, lnb_ref)


def _conv_step_call(u, buf, w_dw, b_dw, ln_g, ln_b, *, j, tb):
    nb, ch = u.shape
    width = w_dw.shape[1]
    vec = lambda a: a.reshape(a.shape[0], 1, ch)
    vspec = pl.BlockSpec((None, 1, ch), lambda i: (j, 0, 0))
    return pl.pallas_call(
        functools.partial(_conv_step_kernel, width=width),
        grid=(nb // tb,),
        in_specs=[
            pl.BlockSpec((tb, ch), lambda i: (i, 0)),
            pl.BlockSpec((None, width - 1, tb, ch), lambda i: (j, 0, i, 0)),
            pl.BlockSpec((None, width, ch), lambda i: (j, 0, 0)),
            vspec, vspec, vspec,
        ],
        out_specs=[
            pl.BlockSpec((tb, ch), lambda i: (i, 0)),
            pl.BlockSpec((width - 1, tb, ch), lambda i: (0, i, 0)),
        ],
        out_shape=[jax.ShapeDtypeStruct((nb, ch), BF16), jax.ShapeDtypeStruct((width - 1, nb, ch), F32)],
        scratch_shapes=[pltpu.VMEM((tb, ch), F32)],
        compiler_params=_cparams("arbitrary"),
        name="conv_step",
    )(u, buf, w_dw, vec(b_dw), vec(ln_g), vec(ln_b))


def kernel(x_prompt, x_sample, state_hgrn, state_conv, c_prompt, c_sample, hgrn_lb, hgrn_w_in, hgrn_b_f, hgrn_g_norm, hgrn_w_out, conv_w_pw1, conv_b_pw1, conv_w_dw, conv_b_dw, conv_ln_g, conv_ln_b, conv_w_pw2, conv_b_pw2, ada_w, ada_b, norm_g, ffn_w_in, ffn_w_out, final_g):
    bp, t, d = x_prompt.shape
    bs = x_sample.shape[0]
    depth = ada_w.shape[0]
    heads = d // LANES
    time_major = lambda s: jnp.transpose(s, (0, 2, 1, 3))

    mod_s, mod_p = _ada_call(c_sample, c_prompt, ada_w, ada_b)
    mod_s = mod_s.reshape(depth, 1, bs, N_SUB * 3 * d)
    mod_p = mod_p.reshape(depth, bp, 1, N_SUB * 3 * d)
    norm_g4 = norm_g.reshape(depth, N_SUB, 1, d)
    conv_buf = time_major(state_conv)

    x, xs = x_prompt.reshape(bp * t, d), x_sample.reshape(bs, d)
    hgrn_p, hgrn_s, conv_p, conv_s = [], [], [], []
    for layer in range(depth):
        x, xs = _ffn_call(x, xs, mod_p, mod_s, norm_g4, ffn_w_in, ffn_w_out, final_g,
                          layer=layer, sub=0, j=0, final=False)
        j = layer // 2
        if layer % 2 == 0:
            outs = _hgrn_in_call(x, xs, mod_p, mod_s, norm_g4, hgrn_w_in, hgrn_b_f, hgrn_lb,
                                 layer=layer, j=j, lb_row=layer)
            y, s = _hgrn_scan_call(*outs[:5], hgrn_g_norm[j], batch=bp, heads=heads)
            ys, ss = _hgrn_step_call(*outs[5:], hgrn_g_norm[j], state_hgrn[j], heads=heads)
            hgrn_p.append(s)
            hgrn_s.append(ss)
            x, xs = _proj_call(y, ys, hgrn_w_out, None, x, xs, mod_p, mod_s, layer=layer, j=j)
        else:
            u, us = _pw1_call(x, xs, mod_p, mod_s, norm_g4, conv_w_pw1, conv_b_pw1, layer=layer, j=j)
            x, s = _conv_seq_call(u, conv_w_dw, conv_b_dw, conv_ln_g, conv_ln_b, x, mod_p, conv_w_pw2, conv_b_pw2,
                                  layer=layer, j=j, batch=bp)
            ys, ss = _conv_step_call(us, conv_buf, conv_w_dw, conv_b_dw, conv_ln_g, conv_ln_b, j=j, tb=4 * SUBLANES)
            conv_p.append(jnp.transpose(s, (1, 0, 2)))
            conv_s.append(ss)
            xs = _proj_rows_call(ys, conv_w_pw2, conv_b_pw2, xs, mod_s, layer=layer, j=j)
        x, xs = _ffn_call(x, xs, mod_p, mod_s, norm_g4, ffn_w_in, ffn_w_out, final_g,
                          layer=layer, sub=2, j=1, final=(layer == depth - 1))
    return (x.reshape(bp, t, d), xs.reshape(bs, 1, d), jnp.stack(hgrn_p), jnp.stack(hgrn_s),
            time_major(jnp.stack(conv_p)), time_major(jnp.stack(conv_s)))
```
